```python
import math
import jax, jax.numpy as jnp
from jax import lax
import numpy as np

D_MODEL = 2048
BATCH = 4
SEQ = 2048
DEPTH = 4
DEC_BATCH = 128
DEC_SEQ = 4
PAST_LEN = 16384
PAGE_SIZE = 128

N_EVEN = (DEPTH + 1) // 2
N_ODD = DEPTH // 2

POOL_WINDOWS = (2, 4, 8, 16)
POOL_GROUPS = len(POOL_WINDOWS)
POOL_WIDTH = D_MODEL // 4
POOL_GROUP_DIM = POOL_WIDTH // POOL_GROUPS
POOL_HIST = max(POOL_WINDOWS) - 1
RET_WIDTH = D_MODEL - POOL_WIDTH
RET_HEADS = 6
RET_DV = RET_WIDTH // RET_HEADS
RET_DK = RET_DV // 2
RET_QK = RET_HEADS * RET_DK
RET_CHUNK = 128
ROPE_BASE = 10000.0
SG_WIDTH = D_MODEL // 2
SG_CHUNK = 128
SG_GROUPS = 4
SG_GROUP_DIM = SG_WIDTH // SG_GROUPS
CONV_CH = D_MODEL // 2
CONV_K = 31
CONV_HIST = CONV_K - 1
D_FF = 4 * D_MODEL
EPS = 1e-6

EVEN_IN = POOL_WIDTH + 2 * RET_QK + 2 * RET_WIDTH
ODD_IN = 2 * SG_WIDTH + 2 * CONV_CH

kernel_name = "pool_retention_sgmlp_conv_hybrid_step"


def _rms(x, g):
    xf = x.astype(jnp.float32)
    y = xf * lax.rsqrt(jnp.mean(xf * xf, axis=-1, keepdims=True) + EPS)
    return (y * g.astype(jnp.float32)).astype(x.dtype)


def _ln(x, g, b):
    xf = x.astype(jnp.float32)
    xc = xf - jnp.mean(xf, axis=-1, keepdims=True)
    y = xc * lax.rsqrt(jnp.mean(xc * xc, axis=-1, keepdims=True) + EPS)
    return (y * g.astype(jnp.float32) + b.astype(jnp.float32)).astype(x.dtype)


def _rotary(x, pos):
    half = x.shape[-1] // 2
    inv = ROPE_BASE ** (-jnp.arange(half, dtype=jnp.float32) / half)
    ang = pos.astype(jnp.float32)[:, None] * inv[None, :]
    cos = jnp.cos(ang)[None, :, None, :]
    sin = jnp.sin(ang)[None, :, None, :]
    x1, x2 = x[..., :half], x[..., half:]
    return jnp.concatenate([x1 * cos - x2 * sin, x1 * sin + x2 * cos], axis=-1)


def _pool_mixer(p, hist, pos0, w_pool, s_pool):
    B, T, C = p.shape
    ext = jnp.concatenate([hist, p], axis=1)
    extf = ext.astype(jnp.float32)
    cs = jnp.concatenate([jnp.zeros((B, 1, C), jnp.float32), jnp.cumsum(extf, axis=1)], axis=1)
    end = cs[:, POOL_HIST + 1:]
    pos = pos0 + jnp.arange(T)
    means = []
    for g, w in enumerate(POOL_WINDOWS):
        sl = slice(g * POOL_GROUP_DIM, (g + 1) * POOL_GROUP_DIM)
        start = cs[:, POOL_HIST + 1 - w:POOL_HIST + 1 - w + T, sl]
        cnt = jnp.minimum(w, pos + 1).astype(jnp.float32)[None, :, None]
        means.append((end[..., sl] - start) / cnt)
    d = (jnp.concatenate(means, axis=-1) - extf[:, POOL_HIST:]).reshape(B, T, POOL_GROUPS, POOL_GROUP_DIM)
    y = jnp.einsum('btgc,gcd->btgd', d, w_pool.astype(jnp.float32)).reshape(B, T, C)
    y = y * s_pool.astype(jnp.float32)
    return y.astype(p.dtype), ext[:, -POOL_HIST:]


def _retention(q, k, v, s0):
    B, T, H, dk = q.shape
    dv = v.shape[-1]
    L = RET_CHUNK if T % RET_CHUNK == 0 else T
    n = T // L
    log_g = jnp.log1p(-jnp.exp2(-5.0 - jnp.arange(H, dtype=jnp.float32)))
    idx = jnp.arange(L, dtype=jnp.float32)
    diff = idx[:, None] - idx[None, :]
    mask = jnp.where(diff[None] >= 0, jnp.exp(log_g[:, None, None] * jnp.maximum(diff, 0.0)[None]), 0.0)
    q_decay = jnp.exp(log_g[:, None] * (idx + 1.0))[None, :, :, None]
    k_decay = jnp.exp(log_g[:, None] * (L - 1.0 - idx))[None, :, :, None]
    chunk_decay = jnp.exp(log_g * L)[None, :, None, None]

    def to_chunks(a):
        return a.reshape(B, n, L, H, a.shape[-1]).transpose(1, 0, 3, 2, 4)

    def step(s, inp):
        qc, kc, vc = inp
        scores = jnp.einsum('bhld,bhmd->bhlm', qc, kc) * mask[None]
        o = jnp.einsum('bhlm,bhme->bhle', scores, vc) + jnp.einsum('bhld,bhde->bhle', qc * q_decay, s)
        s = s * chunk_decay + jnp.einsum('bhld,bhle->bhde', kc * k_decay, vc)
        return s, o

    s, o = lax.scan(step, s0, (to_chunks(q), to_chunks(k), to_chunks(v)))
    o = o.transpose(1, 0, 3, 2, 4).reshape(B, T, H, dv)
    return o, s


def _even_mixer(h, pos0, pool_hist, ret_state, w_in, w_pool, s_pool, w_out):
    B, T, _ = h.shape
    z = h @ w_in
    p, q, k, v, g = jnp.split(z, [POOL_WIDTH, POOL_WIDTH + RET_QK, POOL_WIDTH + 2 * RET_QK,
                                  POOL_WIDTH + 2 * RET_QK + RET_WIDTH], axis=-1)
    ya, new_hist = _pool_mixer(p, pool_hist, pos0, w_pool, s_pool)
    pos = pos0 + jnp.arange(T)
    q = _rotary(q.reshape(B, T, RET_HEADS, RET_DK).astype(jnp.float32), pos)
    k = _rotary(k.reshape(B, T, RET_HEADS, RET_DK).astype(jnp.float32), pos) * (RET_DK ** -0.5)
    v = v.reshape(B, T, RET_HEADS, RET_DV).astype(jnp.float32)
    o, s = _retention(q, k, v, ret_state.astype(jnp.float32))
    o = o * lax.rsqrt(jnp.mean(o * o, axis=-1, keepdims=True) + EPS)
    yb = (jax.nn.silu(g.astype(jnp.float32)) * o.reshape(B, T, RET_WIDTH)).astype(h.dtype)
    y = jnp.concatenate([ya, yb], axis=-1) @ w_out
    return y, new_hist, s.astype(ret_state.dtype)


def _odd_mixer(h, conv_hist, w_in, sg_ln_g, sg_ln_b, sg_w, sg_b, dw_w, dw_b, cv_ln_g, cv_ln_b, w_out):
    B, T, _ = h.shape
    z = h @ w_in
    u, v, a, gate = jnp.split(z, [SG_WIDTH, 2 * SG_WIDTH, 2 * SG_WIDTH + CONV_CH], axis=-1)
    u = jax.nn.gelu(u)
    v = _ln(jax.nn.gelu(v), sg_ln_g, sg_ln_b)
    L = SG_CHUNK if T % SG_CHUNK == 0 else T
    n = T // L
    causal = jnp.tril(jnp.ones((L, L), dtype=bool))
    ws = jnp.where(causal[None], sg_w[:, :L, :L].astype(jnp.float32), 0.0)
    vc = v.reshape(B, n, L, SG_GROUPS, SG_GROUP_DIM).astype(jnp.float32)
    mixed = jnp.einsum('gij,bnjgc->bnigc', ws, vc) + sg_b[:, :L].astype(jnp.float32).T[None, None, :, :, None]
    yc = (u.astype(jnp.float32) * mixed.reshape(B, T, SG_WIDTH)).astype(h.dtype)
    glu = a * jax.nn.sigmoid(gate)
    ext = jnp.concatenate([conv_hist.astype(glu.dtype), glu], axis=1)
    conv = lax.conv_general_dilated(ext, dw_w[:, None, :].astype(ext.dtype), window_strides=(1,),
                                    padding='VALID', dimension_numbers=('NWC', 'WIO', 'NWC'),
                                    feature_group_count=CONV_CH) + dw_b
    yd = jax.nn.silu(_ln(conv, cv_ln_g, cv_ln_b))
    y = jnp.concatenate([yc, yd.astype(h.dtype)], axis=-1) @ w_out
    return y, ext[:, -CONV_HIST:], v


def _trunk(x, pos0, pool_hist, ret_state, conv_hist, w):
    (norm_mix_pre, norm_mix_post, norm_ffn_pre, norm_ffn_post, w_in_even, w_pool, s_pool, w_out_even,
     w_in_odd, sg_ln_g, sg_ln_b, sg_w, sg_b, dw_w, dw_b, cv_ln_g, cv_ln_b, w_out_odd, w_up, w_down) = w
    h = x
    pools, rets, convs, sgvs = [], [], [], []
    for l in range(DEPTH):
        i = l // 2
        a = _rms(h, norm_mix_pre[l])
        if l % 2 == 0:
            y, ph, rs = _even_mixer(a, pos0, pool_hist[i], ret_state[i], w_in_even[i], w_pool[i],
                                    s_pool[i], w_out_even[i])
            pools.append(ph)
            rets.append(rs)
        else:
            y, ch, sv = _odd_mixer(a, conv_hist[i], w_in_odd[i], sg_ln_g[i], sg_ln_b[i], sg_w[i], sg_b[i],
                                   dw_w[i], dw_b[i], cv_ln_g[i], cv_ln_b[i], w_out_odd[i])
            convs.append(ch)
            sgvs.append(sv)
        h = h + _rms(y, norm_mix_post[l])
        f = _rms(h, norm_ffn_pre[l])
        f = jnp.square(jax.nn.relu(f @ w_up[l])) @ w_down[l]
        h = h + _rms(f, norm_ffn_post[l])
    return h, jnp.stack(pools), jnp.stack(rets), jnp.stack(convs), jnp.stack(sgvs)


def setup_inputs(seed: int = 0) -> dict:
    key = jax.random.key(seed)
    ks = jax.random.split(key, 32)
    f32 = jnp.float32

    def nrm(k, shape, scale):
        return jax.random.normal(k, shape, f32) * scale

    def gain(k, shape):
        return 1.0 + 0.05 * jax.random.normal(k, shape, f32)

    return {
        "x_prompt": nrm(ks[0], (BATCH, SEQ, D_MODEL), 1.0),
        "x_sample": nrm(ks[1], (DEC_BATCH, DEC_SEQ, D_MODEL), 1.0),
        "state_pool": nrm(ks[2], (N_EVEN, DEC_BATCH, POOL_HIST, POOL_WIDTH), 1.0),
        "state_ret": nrm(ks[3], (N_EVEN, DEC_BATCH, RET_HEADS, RET_DK, RET_DV), 0.5),
        "state_conv": nrm(ks[4], (N_ODD, DEC_BATCH, CONV_HIST, CONV_CH), 1.0),
        "norm_mix_pre": gain(ks[5], (DEPTH, D_MODEL)),
        "norm_mix_post": gain(ks[6], (DEPTH, D_MODEL)),
        "norm_ffn_pre": gain(ks[7], (DEPTH, D_MODEL)),
        "norm_ffn_post": gain(ks[8], (DEPTH, D_MODEL)),
        "w_in_even": nrm(ks[9], (N_EVEN, D_MODEL, EVEN_IN), D_MODEL ** -0.5),
        "w_pool": nrm(ks[10], (N_EVEN, POOL_GROUPS, POOL_GROUP_DIM, POOL_GROUP_DIM), POOL_GROUP_DIM ** -0.5),
        "s_pool": gain(ks[11], (N_EVEN, POOL_WIDTH)),
        "w_out_even": nrm(ks[12], (N_EVEN, POOL_WIDTH + RET_WIDTH, D_MODEL), (POOL_WIDTH + RET_WIDTH) ** -0.5),
        "w_in_odd": nrm(ks[13], (N_ODD, D_MODEL, ODD_IN), D_MODEL ** -0.5),
        "sg_ln_g": gain(ks[14], (N_ODD, SG_WIDTH)),
        "sg_ln_b": nrm(ks[15], (N_ODD, SG_WIDTH), 0.02),
        "sg_w": nrm(ks[16], (N_ODD, SG_GROUPS, SG_CHUNK, SG_CHUNK), SG_CHUNK ** -0.5),
        "sg_b": gain(ks[17], (N_ODD, SG_GROUPS, SG_CHUNK)),
        "dw_w": nrm(ks[18], (N_ODD, CONV_K, CONV_CH), CONV_K ** -0.5),
        "dw_b": nrm(ks[19], (N_ODD, CONV_CH), 0.02),
        "cv_ln_g": gain(ks[20], (N_ODD, CONV_CH)),
        "cv_ln_b": nrm(ks[21], (N_ODD, CONV_CH), 0.02),
        "w_out_odd": nrm(ks[22], (N_ODD, SG_WIDTH + CONV_CH, D_MODEL), (SG_WIDTH + CONV_CH) ** -0.5),
        "w_up": nrm(ks[23], (DEPTH, D_MODEL, D_FF), D_MODEL ** -0.5),
        "w_down": nrm(ks[24], (DEPTH, D_FF, D_MODEL), D_FF ** -0.5),
    }


def reference(x_prompt, x_sample, state_pool, state_ret, state_conv, norm_mix_pre, norm_mix_post,
              norm_ffn_pre, norm_ffn_post, w_in_even, w_pool, s_pool, w_out_even, w_in_odd, sg_ln_g,
              sg_ln_b, sg_w, sg_b, dw_w, dw_b, cv_ln_g, cv_ln_b, w_out_odd, w_up, w_down):
    w = (norm_mix_pre, norm_mix_post, norm_ffn_pre, norm_ffn_post, w_in_even, w_pool, s_pool, w_out_even,
         w_in_odd, sg_ln_g, sg_ln_b, sg_w, sg_b, dw_w, dw_b, cv_ln_g, cv_ln_b, w_out_odd, w_up, w_down)
    dt = x_prompt.dtype
    pool0 = jnp.zeros((N_EVEN, BATCH, POOL_HIST, POOL_WIDTH), dt)
    ret0 = jnp.zeros((N_EVEN, BATCH, RET_HEADS, RET_DK, RET_DV), state_ret.dtype)
    conv0 = jnp.zeros((N_ODD, BATCH, CONV_HIST, CONV_CH), dt)
    y_prompt, pool_p, ret_p, conv_p, _ = _trunk(x_prompt, 0, pool0, ret0, conv0, w)
    y_sample, pool_s, ret_s, conv_s, sgv_s = _trunk(x_sample, PAST_LEN, state_pool, state_ret, state_conv, w)
    return (y_prompt, y_sample, pool_p, pool_s, ret_p, ret_s, conv_p, conv_s, sgv_s)
```

```python
import functools
import math

import jax
import jax.numpy as jnp
import numpy as np
from jax import lax
from jax.experimental import pallas as pl
from jax.experimental.pallas import tpu as pltpu

F32 = jnp.float32
BF16 = jnp.bfloat16

D_MODEL = 2048
BATCH = 4
SEQ = 2048
DEPTH = 4
DEC_BATCH = 128
DEC_SEQ = 4
PAST_LEN = 16384

N_EVEN = (DEPTH + 1) // 2
N_ODD = DEPTH // 2

POOL_WINDOWS = (2, 4, 8, 16)
POOL_GROUPS = len(POOL_WINDOWS)
POOL_WIDTH = D_MODEL // 4
POOL_GROUP_DIM = POOL_WIDTH // POOL_GROUPS
POOL_HIST = max(POOL_WINDOWS) - 1
RET_WIDTH = D_MODEL - POOL_WIDTH
RET_HEADS = 6
RET_DV = RET_WIDTH // RET_HEADS
RET_DK = RET_DV // 2
RET_QK = RET_HEADS * RET_DK
RET_CHUNK = 128
ROPE_BASE = 10000.0
SG_WIDTH = D_MODEL // 2
SG_CHUNK = 128
SG_GROUPS = 4
SG_GROUP_DIM = SG_WIDTH // SG_GROUPS
CONV_CH = D_MODEL // 2
CONV_K = 31
CONV_HIST = CONV_K - 1
D_FF = 4 * D_MODEL
EPS = 1e-6

EVEN_IN = POOL_WIDTH + 2 * RET_QK + 2 * RET_WIDTH
ODD_IN = 2 * SG_WIDTH + 2 * CONV_CH

Q_OFF = POOL_WIDTH
K_OFF = Q_OFF + RET_QK
V_OFF = K_OFF + RET_QK
G_OFF = V_OFF + RET_WIDTH

N_PROMPT = BATCH * SEQ
N_SAMPLE = DEC_BATCH * DEC_SEQ
N_TOK = N_PROMPT + N_SAMPLE

CHUNK = 128
N_CHUNKS = SEQ // CHUNK
SAMPLE_BB = 4
SAMPLE_ROWS = SAMPLE_BB * DEC_SEQ
POOL_PAD = 16
CONV_PAD = 32

TM_IN = 1088
TN_IN = 1024
TM_OUT = 512
TF = 1024
VMEM_LIMIT = 56 * 1024 * 1024

assert N_TOK % TM_IN == 0 and N_PROMPT % TM_OUT == 0 and N_SAMPLE == TM_OUT
assert POOL_PAD >= POOL_HIST and CONV_PAD >= CONV_HIST


def _params(*sem):
    return pltpu.CompilerParams(dimension_semantics=sem, vmem_limit_bytes=VMEM_LIMIT)


def _rms_scale(x, g):
    return x * lax.rsqrt(jnp.mean(x * x, axis=-1, keepdims=True) + EPS) * g


def _layer_norm(x, g, b):
    xc = x - jnp.mean(x, axis=-1, keepdims=True)
    return xc * lax.rsqrt(jnp.mean(xc * xc, axis=-1, keepdims=True) + EPS) * g + b


def _dot(a, b):
    return jnp.dot(a, b, preferred_element_type=F32)


def _dot_nt(a, b):
    return lax.dot_general(a, b, (((1,), (1,)), ((), ())), preferred_element_type=F32)


def _dot_tn(a, b):
    return lax.dot_general(a, b, (((0,), (0,)), ((), ())), preferred_element_type=F32)


def _norm_matmul_kernel(x_ref, g_ref, w_ref, o_ref, xn_ref):
    @pl.when(pl.program_id(1) == 0)
    def _():
        xn_ref[...] = _rms_scale(x_ref[...], g_ref[...]).astype(BF16)

    o_ref[...] = _dot(xn_ref[...], w_ref[...])


def _norm_matmul(h, g, w):
    n_out = w.shape[1]
    return pl.pallas_call(
        _norm_matmul_kernel,
        grid=(N_TOK // TM_IN, n_out // TN_IN),
        in_specs=[
            pl.BlockSpec((TM_IN, D_MODEL), lambda i, j: (i, 0)),
            pl.BlockSpec((1, D_MODEL), lambda i, j: (0, 0)),
            pl.BlockSpec((D_MODEL, TN_IN), lambda i, j: (0, j)),
        ],
        out_specs=pl.BlockSpec((TM_IN, TN_IN), lambda i, j: (i, j)),
        out_shape=jax.ShapeDtypeStruct((N_TOK, n_out), F32),
        scratch_shapes=[pltpu.VMEM((TM_IN, D_MODEL), BF16)],
        compiler_params=_params("parallel", "arbitrary"),
        name="norm_in_proj",
    )(h, g.reshape(1, D_MODEL), w)


def _out_proj_kernel(yp_ref, ys_ref, w_ref, g_ref, h_ref, o_ref):
    def finish(y):
        r = _dot(y, w_ref[...])
        o_ref[...] = h_ref[...] + _rms_scale(r, g_ref[...])

    is_prompt = pl.program_id(0) < N_PROMPT // TM_OUT

    @pl.when(is_prompt)
    def _():
        finish(yp_ref[...])

    @pl.when(jnp.logical_not(is_prompt))
    def _():
        finish(ys_ref[...])


def _out_proj(y_prompt, y_sample, w, g, h):
    last_prompt = N_PROMPT // TM_OUT - 1
    return pl.pallas_call(
        _out_proj_kernel,
        grid=(N_TOK // TM_OUT,),
        in_specs=[
            pl.BlockSpec((TM_OUT, D_MODEL), lambda i: (jnp.minimum(i, last_prompt), 0)),
            pl.BlockSpec((TM_OUT, D_MODEL), lambda i: (0, 0)),
            pl.BlockSpec((D_MODEL, D_MODEL), lambda i: (0, 0)),
            pl.BlockSpec((1, D_MODEL), lambda i: (0, 0)),
            pl.BlockSpec((TM_OUT, D_MODEL), lambda i: (i, 0)),
        ],
        out_specs=pl.BlockSpec((TM_OUT, D_MODEL), lambda i: (i, 0)),
        out_shape=jax.ShapeDtypeStruct((N_TOK, D_MODEL), F32),
        compiler_params=_params("parallel"),
        name="out_proj_norm_residual",
    )(y_prompt, y_sample, w, g.reshape(1, D_MODEL), h)


def _ffn_kernel(h_ref, g1_ref, wu_ref, wd_ref, g2_ref, o_ref, f_ref):
    k = pl.program_id(1)

    @pl.when(k == 0)
    def _():
        f_ref[...] = _rms_scale(h_ref[...], g1_ref[...]).astype(BF16)
        o_ref[...] = jnp.zeros_like(o_ref)

    u = _dot(f_ref[...], wu_ref[...])
    u = jnp.square(jnp.maximum(u, 0.0)).astype(BF16)
    o_ref[...] += _dot(u, wd_ref[...])

    @pl.when(k == pl.num_programs(1) - 1)
    def _():
        o_ref[...] = h_ref[...] + _rms_scale(o_ref[...], g2_ref[...])


def _ffn(h, g1, w_up, w_down, g2):
    return pl.pallas_call(
        _ffn_kernel,
        grid=(N_TOK // TM_OUT, D_FF // TF),
        in_specs=[
            pl.BlockSpec((TM_OUT, D_MODEL), lambda i, k: (i, 0)),
            pl.BlockSpec((1, D_MODEL), lambda i, k: (0, 0)),
            pl.BlockSpec((D_MODEL, TF), lambda i, k: (0, k)),
            pl.BlockSpec((TF, D_MODEL), lambda i, k: (k, 0)),
            pl.BlockSpec((1, D_MODEL), lambda i, k: (0, 0)),
        ],
        out_specs=pl.BlockSpec((TM_OUT, D_MODEL), lambda i, k: (i, 0)),
        out_shape=jax.ShapeDtypeStruct((N_TOK, D_MODEL), F32),
        scratch_shapes=[pltpu.VMEM((TM_OUT, D_MODEL), BF16)],
        compiler_params=_params("parallel", "arbitrary"),
        name="relu2_mlp",
    )(h, g1.reshape(1, D_MODEL), w_up, w_down, g2.reshape(1, D_MODEL))


def _rotary_tables(pos):
    half = RET_DK // 2
    inv = ROPE_BASE ** (-np.arange(half, dtype=np.float64) / half)
    ang = np.asarray(pos, np.float64)[:, None] * inv[None, :]
    cos = np.concatenate([np.cos(ang), np.cos(ang)], axis=-1)
    sin = np.concatenate([-np.sin(ang), np.sin(ang)], axis=-1)
    return jnp.asarray(cos, F32), jnp.asarray(sin, F32)


def _log_gamma():
    return np.log1p(-np.exp2(-5.0 - np.arange(RET_HEADS, dtype=np.float64)))


def _retention_tables(length):
    log_g = _log_gamma()
    idx = np.arange(length, dtype=np.float64)
    diff = idx[:, None] - idx[None, :]
    mask = np.where(diff[None] >= 0, np.exp(log_g[:, None, None] * np.maximum(diff, 0.0)[None]), 0.0)
    qd = np.exp(log_g[:, None] * (idx + 1.0))
    kd = np.exp(log_g[:, None] * (length - 1.0 - idx))
    cd = np.exp(log_g * length)
    return mask, qd, kd, cd


def _lane_bcast(a, width):
    return np.repeat(a[..., None], width, axis=-1)


def _retention_head_out(o, gate):
    o = o * lax.rsqrt(jnp.mean(o * o, axis=-1, keepdims=True) + EPS)
    return (gate * jax.nn.sigmoid(gate) * o).astype(BF16)


def _rotate(x, cos, sin):
    return x * cos + pltpu.roll(x, RET_DK // 2, 1) * sin


def _even_prompt_kernel(z_ref, cos_ref, sin_ref, mask_ref, qd_ref, kd_ref, wp_ref, sp_ref,
                        y_ref, hist_ref, st_ref, pext_ref, s_ref, *, chunk_decay):
    n = pl.program_id(1)
    last = pl.num_programs(1) - 1

    @pl.when(n == 0)
    def _():
        pext_ref[0:POOL_PAD, :] = jnp.zeros((POOL_PAD, POOL_WIDTH), F32)
        s_ref[...] = jnp.zeros_like(s_ref)

    pext_ref[POOL_PAD:POOL_PAD + CHUNK, :] = z_ref[:, 0:POOL_WIDTH]
    pos = n * CHUNK + lax.broadcasted_iota(jnp.int32, (CHUNK, 1), 0)
    for g, w in enumerate(POOL_WINDOWS):
        lanes = slice(g * POOL_GROUP_DIM, (g + 1) * POOL_GROUP_DIM)
        p = pext_ref[POOL_PAD:POOL_PAD + CHUNK, lanes]
        acc = p
        for i in range(1, w):
            acc = acc + pext_ref[POOL_PAD - i:POOL_PAD - i + CHUNK, lanes]
        cnt = jnp.minimum(w, pos + 1).astype(F32)
        d = acc / cnt - p
        yg = _dot(d.astype(BF16), wp_ref[g].astype(BF16)) * sp_ref[:, lanes]
        y_ref[:, lanes] = yg.astype(BF16)

    @pl.when(n == last)
    def _():
        hist_ref[0] = pext_ref[POOL_PAD + CHUNK - POOL_HIST:POOL_PAD + CHUNK, :]

    pext_ref[0:POOL_PAD, :] = pext_ref[CHUNK:CHUNK + POOL_PAD, :]

    cos = cos_ref[...]
    sin = sin_ref[...]
    for h in range(RET_HEADS):
        q = _rotate(z_ref[:, Q_OFF + h * RET_DK:Q_OFF + (h + 1) * RET_DK], cos, sin)
        k = _rotate(z_ref[:, K_OFF + h * RET_DK:K_OFF + (h + 1) * RET_DK], cos, sin) * (RET_DK ** -0.5)
        v = z_ref[:, V_OFF + h * RET_DV:V_OFF + (h + 1) * RET_DV].astype(BF16)
        gate = z_ref[:, G_OFF + h * RET_DV:G_OFF + (h + 1) * RET_DV]
        s = s_ref[h]
        scores = _dot_nt(q.astype(BF16), k.astype(BF16)) * mask_ref[h]
        o = _dot(scores.astype(BF16), v) + _dot((q * qd_ref[h]).astype(BF16), s.astype(BF16))
        s_ref[h] = s * chunk_decay[h] + _dot_tn((k * kd_ref[h]).astype(BF16), v)
        y_ref[:, POOL_WIDTH + h * RET_DV:POOL_WIDTH + (h + 1) * RET_DV] = _retention_head_out(o, gate)

    @pl.when(n == last)
    def _():
        st_ref[0] = s_ref[...]


def _even_prompt(z, w_pool, s_pool):
    cos, sin = _rotary_tables(np.arange(SEQ))
    mask, qd, kd, cd = _retention_tables(CHUNK)
    const = lambda *shape: pl.BlockSpec(shape, lambda b, n: (0,) * len(shape))
    return pl.pallas_call(
        functools.partial(_even_prompt_kernel, chunk_decay=tuple(float(c) for c in cd)),
        grid=(BATCH, N_CHUNKS),
        in_specs=[
            pl.BlockSpec((CHUNK, EVEN_IN), lambda b, n: (b * N_CHUNKS + n, 0)),
            pl.BlockSpec((CHUNK, RET_DK), lambda b, n: (n, 0)),
            pl.BlockSpec((CHUNK, RET_DK), lambda b, n: (n, 0)),
            const(RET_HEADS, CHUNK, CHUNK),
            const(RET_HEADS, CHUNK, RET_DK),
            const(RET_HEADS, CHUNK, RET_DK),
            const(POOL_GROUPS, POOL_GROUP_DIM, POOL_GROUP_DIM),
            const(1, POOL_WIDTH),
        ],
        out_specs=[
            pl.BlockSpec((CHUNK, D_MODEL), lambda b, n: (b * N_CHUNKS + n, 0)),
            pl.BlockSpec((1, POOL_HIST, POOL_WIDTH), lambda b, n: (b, 0, 0)),
            pl.BlockSpec((1, RET_HEADS, RET_DK, RET_DV), lambda b, n: (b, 0, 0, 0)),
        ],
        out_shape=[
            jax.ShapeDtypeStruct((N_PROMPT, D_MODEL), BF16),
            jax.ShapeDtypeStruct((BATCH, POOL_HIST, POOL_WIDTH), F32),
            jax.ShapeDtypeStruct((BATCH, RET_HEADS, RET_DK, RET_DV), F32),
        ],
        scratch_shapes=[
            pltpu.VMEM((POOL_PAD + CHUNK, POOL_WIDTH), F32),
            pltpu.VMEM((RET_HEADS, RET_DK, RET_DV), F32),
        ],
        compiler_params=_params("parallel", "arbitrary"),
        name="even_mixer_prompt",
    )(z, cos, sin, jnp.asarray(mask, F32), jnp.asarray(_lane_bcast(qd, RET_DK), F32),
      jnp.asarray(_lane_bcast(kd, RET_DK), F32), w_pool, s_pool.reshape(1, POOL_WIDTH))


def _even_sample_kernel(z_ref, cos_ref, sin_ref, mask_ref, qd_ref, kd_ref, wp_ref, sp_ref, hist_ref, st_ref,
                        y_ref, nhist_ref, nst_ref, pext_ref, d_ref, *, chunk_decay):
    t_idx = lax.broadcasted_iota(jnp.int32, (DEC_SEQ, 1), 0)
    for b in range(SAMPLE_BB):
        rows = slice(b * DEC_SEQ, (b + 1) * DEC_SEQ)
        pext_ref[0:POOL_HIST, :] = hist_ref[b]
        pext_ref[POOL_HIST:POOL_HIST + DEC_SEQ, :] = z_ref[rows, 0:POOL_WIDTH]
        for g, w in enumerate(POOL_WINDOWS):
            lanes = slice(g * POOL_GROUP_DIM, (g + 1) * POOL_GROUP_DIM)
            p = pext_ref[POOL_HIST:POOL_HIST + DEC_SEQ, lanes]
            acc = p
            for i in range(1, w):
                acc = acc + pext_ref[POOL_HIST - i:POOL_HIST - i + DEC_SEQ, lanes]
            cnt = jnp.minimum(w, PAST_LEN + t_idx + 1).astype(F32)
            d_ref[rows, lanes] = acc / cnt - p
        nhist_ref[b] = pext_ref[DEC_SEQ:DEC_SEQ + POOL_HIST, :]
    for g in range(POOL_GROUPS):
        lanes = slice(g * POOL_GROUP_DIM, (g + 1) * POOL_GROUP_DIM)
        yg = _dot(d_ref[:, lanes].astype(BF16), wp_ref[g].astype(BF16)) * sp_ref[:, lanes]
        y_ref[:, lanes] = yg.astype(BF16)

    cos = cos_ref[...]
    sin = sin_ref[...]
    seq_of_row = lax.broadcasted_iota(jnp.int32, (SAMPLE_ROWS, 1), 0) // DEC_SEQ
    for h in range(RET_HEADS):
        q = _rotate(z_ref[:, Q_OFF + h * RET_DK:Q_OFF + (h + 1) * RET_DK], cos, sin)
        k = _rotate(z_ref[:, K_OFF + h * RET_DK:K_OFF + (h + 1) * RET_DK], cos, sin) * (RET_DK ** -0.5)
        v = z_ref[:, V_OFF + h * RET_DV:V_OFF + (h + 1) * RET_DV].astype(BF16)
        gate = z_ref[:, G_OFF + h * RET_DV:G_OFF + (h + 1) * RET_DV]
        scores = _dot_nt(q.astype(BF16), k.astype(BF16)) * mask_ref[h]
        o = _dot(scores.astype(BF16), v)
        q_dec = (q * qd_ref[h]).astype(BF16)
        k_dec = k * kd_ref[h]
        for b in range(SAMPLE_BB):
            own = seq_of_row == b
            s = st_ref[b, h]
            o = o + jnp.where(own, _dot(q_dec, s.astype(BF16)), 0.0)
            nst_ref[b, h] = s * chunk_decay[h] + _dot_tn(jnp.where(own, k_dec, 0.0).astype(BF16), v)
        y_ref[:, POOL_WIDTH + h * RET_DV:POOL_WIDTH + (h + 1) * RET_DV] = _retention_head_out(o, gate)


def _even_sample(z, w_pool, s_pool, state_pool, state_ret, layer):
    pos = PAST_LEN + np.arange(DEC_SEQ)
    cos, sin = _rotary_tables(np.tile(pos, SAMPLE_BB))
    mask, qd, kd, cd = _retention_tables(DEC_SEQ)
    mask = np.stack([np.kron(np.eye(SAMPLE_BB), m) for m in mask])
    qd = _lane_bcast(np.tile(qd, (1, SAMPLE_BB)), RET_DK)
    kd = _lane_bcast(np.tile(kd, (1, SAMPLE_BB)), RET_DK)
    const = lambda *shape: pl.BlockSpec(shape, lambda i: (0,) * len(shape))
    first = N_PROMPT // SAMPLE_ROWS
    return pl.pallas_call(
        functools.partial(_even_sample_kernel, chunk_decay=tuple(float(c) for c in cd)),
        grid=(DEC_BATCH // SAMPLE_BB,),
        in_specs=[
            pl.BlockSpec((SAMPLE_ROWS, EVEN_IN), lambda i: (first + i, 0)),
            const(SAMPLE_ROWS, RET_DK),
            const(SAMPLE_ROWS, RET_DK),
            const(RET_HEADS, SAMPLE_ROWS, SAMPLE_ROWS),
            const(RET_HEADS, SAMPLE_ROWS, RET_DK),
            const(RET_HEADS, SAMPLE_ROWS, RET_DK),
            const(POOL_GROUPS, POOL_GROUP_DIM, POOL_GROUP_DIM),
            const(1, POOL_WIDTH),
            pl.BlockSpec((None, SAMPLE_BB, POOL_HIST, POOL_WIDTH), lambda i: (layer, i, 0, 0)),
            pl.BlockSpec((None, SAMPLE_BB, RET_HEADS, RET_DK, RET_DV), lambda i: (layer, i, 0, 0, 0)),
        ],
        out_specs=[
            pl.BlockSpec((SAMPLE_ROWS, D_MODEL), lambda i: (i, 0)),
            pl.BlockSpec((SAMPLE_BB, POOL_HIST, POOL_WIDTH), lambda i: (i, 0, 0)),
            pl.BlockSpec((SAMPLE_BB, RET_HEADS, RET_DK, RET_DV), lambda i: (i, 0, 0, 0)),
        ],
        out_shape=[
            jax.ShapeDtypeStruct((N_SAMPLE, D_MODEL), BF16),
            jax.ShapeDtypeStruct((DEC_BATCH, POOL_HIST, POOL_WIDTH), F32),
            jax.ShapeDtypeStruct((DEC_BATCH, RET_HEADS, RET_DK, RET_DV), F32),
        ],
        scratch_shapes=[
            pltpu.VMEM((POOL_HIST + DEC_SEQ, POOL_WIDTH), F32),
            pltpu.VMEM((SAMPLE_ROWS, POOL_WIDTH), F32),
        ],
        compiler_params=_params("parallel"),
        name="even_mixer_sample",
    )(z, cos, sin, jnp.asarray(mask, F32), jnp.asarray(qd, F32), jnp.asarray(kd, F32),
      w_pool, s_pool.reshape(1, POOL_WIDTH), state_pool, state_ret)


CONV_LANES = 128


def _depthwise_conv(ext_ref, first_row, n_rows, dw_ref, dwb_ref, out_ref, out_row):
    for c in range(0, CONV_CH, CONV_LANES):
        lanes = slice(c, c + CONV_LANES)
        acc = jnp.broadcast_to(dwb_ref[:, lanes], (n_rows, CONV_LANES))
        for j in range(CONV_K):
            acc = acc + ext_ref[first_row + j:first_row + j + n_rows, lanes] * dw_ref[j:j + 1, lanes]
        out_ref[out_row:out_row + n_rows, lanes] = acc


def _odd_prompt_kernel(z_ref, lng_ref, lnb_ref, sgw_ref, sgb_ref, dw_ref, dwb_ref, cvg_ref, cvb_ref,
                       y_ref, cst_ref, ext_ref, cv_ref):
    n = pl.program_id(1)

    @pl.when(n == 0)
    def _():
        ext_ref[0:CONV_PAD, :] = jnp.zeros((CONV_PAD, CONV_CH), F32)

    u = jax.nn.gelu(z_ref[:, 0:SG_WIDTH])
    v = _layer_norm(jax.nn.gelu(z_ref[:, SG_WIDTH:2 * SG_WIDTH]), lng_ref[...], lnb_ref[...])
    row = lax.broadcasted_iota(jnp.int32, (CHUNK, CHUNK), 0)
    col = lax.broadcasted_iota(jnp.int32, (CHUNK, CHUNK), 1)
    for g in range(SG_GROUPS):
        lanes = slice(g * SG_GROUP_DIM, (g + 1) * SG_GROUP_DIM)
        ws = jnp.where(col <= row, sgw_ref[g], 0.0).astype(BF16)
        mixed = _dot(ws, v[:, lanes].astype(BF16)) + sgb_ref[:, g:g + 1]
        y_ref[:, lanes] = (u[:, lanes] * mixed).astype(BF16)

    a = z_ref[:, 2 * SG_WIDTH:2 * SG_WIDTH + CONV_CH]
    gate = z_ref[:, 2 * SG_WIDTH + CONV_CH:2 * SG_WIDTH + 2 * CONV_CH]
    ext_ref[CONV_PAD:CONV_PAD + CHUNK, :] = a * jax.nn.sigmoid(gate)
    _depthwise_conv(ext_ref, CONV_PAD - CONV_HIST, CHUNK, dw_ref, dwb_ref, cv_ref, 0)
    yd = _layer_norm(cv_ref[...], cvg_ref[...], cvb_ref[...])
    y_ref[:, SG_WIDTH:SG_WIDTH + CONV_CH] = (yd * jax.nn.sigmoid(yd)).astype(BF16)

    @pl.when(n == pl.num_programs(1) - 1)
    def _():
        cst_ref[0] = ext_ref[CONV_PAD + CHUNK - CONV_HIST:CONV_PAD + CHUNK, :]

    ext_ref[0:CONV_PAD, :] = ext_ref[CHUNK:CHUNK + CONV_PAD, :]


def _odd_prompt(z, sg_ln_g, sg_ln_b, sg_w, sg_b, dw_w, dw_b, cv_ln_g, cv_ln_b):
    const = lambda *shape: pl.BlockSpec(shape, lambda b, n: (0,) * len(shape))
    return pl.pallas_call(
        _odd_prompt_kernel,
        grid=(BATCH, N_CHUNKS),
        in_specs=[
            pl.BlockSpec((CHUNK, ODD_IN), lambda b, n: (b * N_CHUNKS + n, 0)),
            const(1, SG_WIDTH),
            const(1, SG_WIDTH),
            const(SG_GROUPS, SG_CHUNK, SG_CHUNK),
            const(SG_CHUNK, SG_GROUPS),
            const(CONV_K, CONV_CH),
            const(1, CONV_CH),
            const(1, CONV_CH),
            const(1, CONV_CH),
        ],
        out_specs=[
            pl.BlockSpec((CHUNK, D_MODEL), lambda b, n: (b * N_CHUNKS + n, 0)),
            pl.BlockSpec((1, CONV_HIST, CONV_CH), lambda b, n: (b, 0, 0)),
        ],
        out_shape=[
            jax.ShapeDtypeStruct((N_PROMPT, D_MODEL), BF16),
            jax.ShapeDtypeStruct((BATCH, CONV_HIST, CONV_CH), F32),
        ],
        scratch_shapes=[
            pltpu.VMEM((CONV_PAD + CHUNK, CONV_CH), F32),
            pltpu.VMEM((CHUNK, CONV_CH), F32),
        ],
        compiler_params=_params("parallel", "arbitrary"),
        name="odd_mixer_prompt",
    )(z, sg_ln_g.reshape(1, -1), sg_ln_b.reshape(1, -1), sg_w, sg_b.T, dw_w, dw_b.reshape(1, -1),
      cv_ln_g.reshape(1, -1), cv_ln_b.reshape(1, -1))


SG_SHIFT_PAD = 8


def _odd_sample_kernel(sgw_ref, sgb_ref, z_ref, lng_ref, lnb_ref, dw_ref, dwb_ref, cvg_ref, cvb_ref, cst_ref,
                       y_ref, sgv_ref, ncst_ref, vs_ref, ext_ref, cv_ref):
    u = jax.nn.gelu(z_ref[:, 0:SG_WIDTH])
    v = _layer_norm(jax.nn.gelu(z_ref[:, SG_WIDTH:2 * SG_WIDTH]), lng_ref[...], lnb_ref[...])
    sgv_ref[...] = v
    vs_ref[0:SG_SHIFT_PAD, :] = jnp.zeros((SG_SHIFT_PAD, SG_WIDTH), F32)
    vs_ref[SG_SHIFT_PAD:SG_SHIFT_PAD + SAMPLE_ROWS, :] = v
    t_of_row = lax.broadcasted_iota(jnp.int32, (SAMPLE_ROWS, 1), 0) % DEC_SEQ
    for g in range(SG_GROUPS):
        lanes = slice(g * SG_GROUP_DIM, (g + 1) * SG_GROUP_DIM)
        mixed = jnp.zeros((SAMPLE_ROWS, SG_GROUP_DIM), F32)
        for back in range(DEC_SEQ):
            coef = jnp.zeros((SAMPLE_ROWS, 1), F32)
            for t in range(back, DEC_SEQ):
                coef = jnp.where(t_of_row == t, sgw_ref[(g * DEC_SEQ + t) * DEC_SEQ + t - back], coef)
            shifted = vs_ref[SG_SHIFT_PAD - back:SG_SHIFT_PAD - back + SAMPLE_ROWS, lanes]
            mixed = mixed + coef * shifted
        bias = jnp.zeros((SAMPLE_ROWS, 1), F32)
        for t in range(DEC_SEQ):
            bias = jnp.where(t_of_row == t, sgb_ref[g * DEC_SEQ + t], bias)
        y_ref[:, lanes] = (u[:, lanes] * (mixed + bias)).astype(BF16)

    a = z_ref[:, 2 * SG_WIDTH:2 * SG_WIDTH + CONV_CH]
    gate = z_ref[:, 2 * SG_WIDTH + CONV_CH:2 * SG_WIDTH + 2 * CONV_CH]
    glu = a * jax.nn.sigmoid(gate)
    for b in range(SAMPLE_BB):
        rows = slice(b * DEC_SEQ, (b + 1) * DEC_SEQ)
        ext_ref[0:CONV_HIST, :] = cst_ref[b]
        ext_ref[CONV_HIST:CONV_HIST + DEC_SEQ, :] = glu[rows, :]
        _depthwise_conv(ext_ref, 0, DEC_SEQ, dw_ref, dwb_ref, cv_ref, b * DEC_SEQ)
        ncst_ref[b] = ext_ref[DEC_SEQ:DEC_SEQ + CONV_HIST, :]
    yd = _layer_norm(cv_ref[...], cvg_ref[...], cvb_ref[...])
    y_ref[:, SG_WIDTH:SG_WIDTH + CONV_CH] = (yd * jax.nn.sigmoid(yd)).astype(BF16)


def _odd_sample(z, sg_ln_g, sg_ln_b, sg_w, sg_b, dw_w, dw_b, cv_ln_g, cv_ln_b, state_conv, layer):
    const = lambda *shape: pl.BlockSpec(shape, lambda i: (0,) * len(shape))
    smem = pl.BlockSpec(memory_space=pltpu.SMEM)
    first = N_PROMPT // SAMPLE_ROWS
    return pl.pallas_call(
        _odd_sample_kernel,
        grid=(DEC_BATCH // SAMPLE_BB,),
        in_specs=[
            smem,
            smem,
            pl.BlockSpec((SAMPLE_ROWS, ODD_IN), lambda i: (first + i, 0)),
            const(1, SG_WIDTH),
            const(1, SG_WIDTH),
            const(CONV_K, CONV_CH),
            const(1, CONV_CH),
            const(1, CONV_CH),
            const(1, CONV_CH),
            pl.BlockSpec((None, SAMPLE_BB, CONV_HIST, CONV_CH), lambda i: (layer, i, 0, 0)),
        ],
        out_specs=[
            pl.BlockSpec((SAMPLE_ROWS, D_MODEL), lambda i: (i, 0)),
            pl.BlockSpec((SAMPLE_ROWS, SG_WIDTH), lambda i: (i, 0)),
            pl.BlockSpec((SAMPLE_BB, CONV_HIST, CONV_CH), lambda i: (i, 0, 0)),
        ],
        out_shape=[
            jax.ShapeDtypeStruct((N_SAMPLE, D_MODEL), BF16),
            jax.ShapeDtypeStruct((N_SAMPLE, SG_WIDTH), F32),
            jax.ShapeDtypeStruct((DEC_BATCH, CONV_HIST, CONV_CH), F32),
        ],
        scratch_shapes=[
            pltpu.VMEM((SG_SHIFT_PAD + SAMPLE_ROWS, SG_WIDTH), F32),
            pltpu.VMEM((CONV_HIST + DEC_SEQ, CONV_CH), F32),
            pltpu.VMEM((SAMPLE_ROWS, CONV_CH), F32),
        ],
        compiler_params=_params("parallel"),
        name="odd_mixer_sample",
    )(sg_w[:, :DEC_SEQ, :DEC_SEQ].reshape(-1), sg_b[:, :DEC_SEQ].reshape(-1), z,
      sg_ln_g.reshape(1, -1), sg_ln_b.reshape(1, -1), dw_w, dw_b.reshape(1, -1),
      cv_ln_g.reshape(1, -1), cv_ln_b.reshape(1, -1), state_conv)


def kernel(x_prompt, x_sample, state_pool, state_ret, state_conv, norm_mix_pre, norm_mix_post, norm_ffn_pre, norm_ffn_post, w_in_even, w_pool, s_pool, w_out_even, w_in_odd, sg_ln_g, sg_ln_b, sg_w, sg_b, dw_w, dw_b, cv_ln_g, cv_ln_b, w_out_odd, w_up, w_down):
    w_in_even, w_out_even, w_in_odd, w_out_odd, w_up, w_down = (
        w.astype(BF16) for w in (w_in_even, w_out_even, w_in_odd, w_out_odd, w_up, w_down))

    h = jnp.concatenate([x_prompt.reshape(N_PROMPT, D_MODEL), x_sample.reshape(N_SAMPLE, D_MODEL)], axis=0)
    pool_p, pool_s, ret_p, ret_s, conv_p, conv_s, sgv_s = [], [], [], [], [], [], []
    for l in range(DEPTH):
        i = l // 2
        if l % 2 == 0:
            z = _norm_matmul(h, norm_mix_pre[l], w_in_even[i])
            y_p, hist_p, st_p = _even_prompt(z, w_pool[i], s_pool[i])
            y_s, hist_s, st_s = _even_sample(z, w_pool[i], s_pool[i], state_pool, state_ret, i)
            pool_p.append(hist_p)
            pool_s.append(hist_s)
            ret_p.append(st_p)
            ret_s.append(st_s)
            w_out = w_out_even[i]
        else:
            z = _norm_matmul(h, norm_mix_pre[l], w_in_odd[i])
            y_p, cst_p = _odd_prompt(z, sg_ln_g[i], sg_ln_b[i], sg_w[i], sg_b[i], dw_w[i], dw_b[i],
                                     cv_ln_g[i], cv_ln_b[i])
            y_s, v_s, cst_s = _odd_sample(z, sg_ln_g[i], sg_ln_b[i], sg_w[i], sg_b[i], dw_w[i], dw_b[i],
                                          cv_ln_g[i], cv_ln_b[i], state_conv, i)
            conv_p.append(cst_p)
            conv_s.append(cst_s)
            sgv_s.append(v_s.reshape(DEC_BATCH, DEC_SEQ, SG_WIDTH))
            w_out = w_out_odd[i]
        h = _out_proj(y_p, y_s, w_out, norm_mix_post[l], h)
        h = _ffn(h, norm_ffn_pre[l], w_up[l], w_down[l], norm_ffn_post[l])

    y_prompt = h[:N_PROMPT].reshape(BATCH, SEQ, D_MODEL)
    y_sample = h[N_PROMPT:].reshape(DEC_BATCH, DEC_SEQ, D_MODEL)
    return (y_prompt, y_sample, jnp.stack(pool_p), jnp.stack(pool_s), jnp.stack(ret_p), jnp.stack(ret_s),
            jnp.stack(conv_p), jnp.stack(conv_s), jnp.stack(sgv_s))
```

```python
import functools

import jax
import jax.numpy as jnp
import numpy as np
from jax import lax
from jax.experimental import pallas as pl
from jax.experimental.pallas import tpu as pltpu

F32 = jnp.float32
BF16 = jnp.bfloat16

D_MODEL = 2048
BATCH = 4
SEQ = 2048
DEPTH = 4
DEC_BATCH = 128
DEC_SEQ = 4
PAST_LEN = 16384

N_EVEN = (DEPTH + 1) // 2
N_ODD = DEPTH // 2

POOL_WINDOWS = (2, 4, 8, 16)
POOL_GROUPS = len(POOL_WINDOWS)
POOL_WIDTH = D_MODEL // 4
POOL_GROUP_DIM = POOL_WIDTH // POOL_GROUPS
POOL_HIST = max(POOL_WINDOWS) - 1
RET_WIDTH = D_MODEL - POOL_WIDTH
RET_HEADS = 6
RET_DV = RET_WIDTH // RET_HEADS
RET_DK = RET_DV // 2
RET_QK = RET_HEADS * RET_DK
RET_CHUNK = 128
ROPE_BASE = 10000.0
SG_WIDTH = D_MODEL // 2
SG_CHUNK = 128
SG_GROUPS = 4
SG_GROUP_DIM = SG_WIDTH // SG_GROUPS
CONV_CH = D_MODEL // 2
CONV_K = 31
CONV_HIST = CONV_K - 1
D_FF = 4 * D_MODEL
EPS = 1e-6

EVEN_IN = POOL_WIDTH + 2 * RET_QK + 2 * RET_WIDTH
ODD_IN = 2 * SG_WIDTH + 2 * CONV_CH

Q_OFF = POOL_WIDTH
K_OFF = Q_OFF + RET_QK
V_OFF = K_OFF + RET_QK
G_OFF = V_OFF + RET_WIDTH

N_PROMPT = BATCH * SEQ
N_SAMPLE = DEC_BATCH * DEC_SEQ
N_TOK = N_PROMPT + N_SAMPLE

CHUNK = 128
N_CHUNKS = SEQ // CHUNK
SAMPLE_BB = 4
SAMPLE_ROWS = SAMPLE_BB * DEC_SEQ
POOL_PAD = 16
CONV_PAD = 32

TM_IN = 1088
TN_IN = 1024
TM_OUT = 512
SLAB = 128
TF = 1024
VMEM_LIMIT = 56 * 1024 * 1024

N_PROMPT_TILES = N_PROMPT // TM_OUT
assert N_TOK % TM_IN == 0 and N_PROMPT % TM_OUT == 0 and N_SAMPLE == TM_OUT and TM_OUT % SLAB == 0
assert POOL_PAD >= POOL_HIST and CONV_PAD >= CONV_HIST


def _params(*sem):
    return pltpu.CompilerParams(dimension_semantics=sem, vmem_limit_bytes=VMEM_LIMIT)


def _rms_scale(x, g):
    return x * lax.rsqrt(jnp.mean(x * x, axis=-1, keepdims=True) + EPS) * g


def _layer_norm(x, g, b):
    xc = x - jnp.mean(x, axis=-1, keepdims=True)
    return xc * lax.rsqrt(jnp.mean(xc * xc, axis=-1, keepdims=True) + EPS) * g + b


def _dot(a, b):
    return jnp.dot(a, b, preferred_element_type=F32)


def _dot_nt(a, b):
    return lax.dot_general(a, b, (((1,), (1,)), ((), ())), preferred_element_type=F32)


def _dot_tn(a, b):
    return lax.dot_general(a, b, (((0,), (0,)), ((), ())), preferred_element_type=F32)


def _skip_aliased(body, n_in, n_aliased):
    def wrapped(*refs):
        return body(*refs[:n_in], *refs[n_in + n_aliased:])
    return wrapped


def _call_stacked(body, *, name, grid, in_specs, args, out_specs, out_shape, stacked, sem, scratch_shapes):
    prev = [(o, p) for o, p in sorted(stacked.items()) if p is not None]
    n_in = len(args)
    return pl.pallas_call(
        _skip_aliased(body, n_in, len(prev)),
        grid=grid,
        in_specs=list(in_specs) + [pl.BlockSpec(memory_space=pl.ANY)] * len(prev),
        out_specs=out_specs,
        out_shape=out_shape,
        input_output_aliases={n_in + j: o for j, (o, _) in enumerate(prev)},
        scratch_shapes=scratch_shapes,
        compiler_params=_params(*sem),
        name=name,
    )(*args, *[p for _, p in prev])


def _is_prompt_tile():
    return pl.program_id(0) < N_PROMPT_TILES


def _prenorm_kernel(xp_ref, xs_ref, g_ref, h_ref, a_ref):
    def emit(x_ref):
        x = x_ref[...]
        h_ref[...] = x
        a_ref[...] = _rms_scale(x, g_ref[...]).astype(BF16)

    pl.when(_is_prompt_tile())(lambda: emit(xp_ref))
    pl.when(jnp.logical_not(_is_prompt_tile()))(lambda: emit(xs_ref))


def _prenorm(x_prompt, x_sample, g):
    return pl.pallas_call(
        _prenorm_kernel,
        grid=(N_TOK // TM_OUT,),
        in_specs=[
            pl.BlockSpec((TM_OUT, D_MODEL), lambda i: (jnp.minimum(i, N_PROMPT_TILES - 1), 0)),
            pl.BlockSpec((TM_OUT, D_MODEL), lambda i: (0, 0)),
            pl.BlockSpec((1, D_MODEL), lambda i: (0, 0)),
        ],
        out_specs=[
            pl.BlockSpec((TM_OUT, D_MODEL), lambda i: (i, 0)),
            pl.BlockSpec((TM_OUT, D_MODEL), lambda i: (i, 0)),
        ],
        out_shape=[
            jax.ShapeDtypeStruct((N_TOK, D_MODEL), F32),
            jax.ShapeDtypeStruct((N_TOK, D_MODEL), BF16),
        ],
        compiler_params=_params("parallel"),
        name="join_prenorm",
    )(x_prompt.reshape(N_PROMPT, D_MODEL), x_sample.reshape(N_SAMPLE, D_MODEL), g.reshape(1, D_MODEL))


def _in_proj_kernel(a_ref, w_ref, o_ref):
    o_ref[...] = _dot(a_ref[...], w_ref[...])


def _in_proj(a, w, layer):
    n_out = w.shape[2]
    return pl.pallas_call(
        _in_proj_kernel,
        grid=(N_TOK // TM_IN, n_out // TN_IN),
        in_specs=[
            pl.BlockSpec((TM_IN, D_MODEL), lambda i, j: (i, 0)),
            pl.BlockSpec((None, D_MODEL, TN_IN), lambda i, j: (layer, 0, j)),
        ],
        out_specs=pl.BlockSpec((TM_IN, TN_IN), lambda i, j: (i, j)),
        out_shape=jax.ShapeDtypeStruct((N_TOK, n_out), F32),
        compiler_params=_params("parallel", "arbitrary"),
        name="in_proj",
    )(a, w)


def _out_proj_kernel(yp_ref, ys_ref, w_ref, g_ref, gf_ref, h_ref, o_ref, f_ref):
    def finish(y_ref):
        for r in range(0, TM_OUT, SLAB):
            rows = slice(r, r + SLAB)
            hn = h_ref[rows, :] + _rms_scale(_dot(y_ref[rows, :], w_ref[...]), g_ref[...])
            o_ref[rows, :] = hn
            f_ref[rows, :] = _rms_scale(hn, gf_ref[...]).astype(BF16)

    pl.when(_is_prompt_tile())(lambda: finish(yp_ref))
    pl.when(jnp.logical_not(_is_prompt_tile()))(lambda: finish(ys_ref))


def _out_proj(y_prompt, y_sample, w, layer, g, g_ffn, h):
    return pl.pallas_call(
        _out_proj_kernel,
        grid=(N_TOK // TM_OUT,),
        in_specs=[
            pl.BlockSpec((TM_OUT, D_MODEL), lambda i: (jnp.minimum(i, N_PROMPT_TILES - 1), 0)),
            pl.BlockSpec((TM_OUT, D_MODEL), lambda i: (0, 0)),
            pl.BlockSpec((None, D_MODEL, D_MODEL), lambda i: (layer, 0, 0)),
            pl.BlockSpec((1, D_MODEL), lambda i: (0, 0)),
            pl.BlockSpec((1, D_MODEL), lambda i: (0, 0)),
            pl.BlockSpec((TM_OUT, D_MODEL), lambda i: (i, 0)),
        ],
        out_specs=[
            pl.BlockSpec((TM_OUT, D_MODEL), lambda i: (i, 0)),
            pl.BlockSpec((TM_OUT, D_MODEL), lambda i: (i, 0)),
        ],
        out_shape=[
            jax.ShapeDtypeStruct((N_TOK, D_MODEL), F32),
            jax.ShapeDtypeStruct((N_TOK, D_MODEL), BF16),
        ],
        compiler_params=_params("parallel"),
        name="out_proj_norm_residual",
    )(y_prompt, y_sample, w, g.reshape(1, D_MODEL), g_ffn.reshape(1, D_MODEL), h)


def _ffn_accumulate(f_ref, wu_ref, wd_ref, acc_ref):
    k = pl.program_id(1)
    last = pl.num_programs(1) - 1
    u = jnp.square(jnp.maximum(_dot(f_ref[...], wu_ref[...]), 0.0)).astype(BF16)

    @pl.when(k == 0)
    def _():
        acc_ref[...] = _dot(u, wd_ref[...])

    @pl.when(jnp.logical_and(k > 0, k < last))
    def _():
        acc_ref[...] += _dot(u, wd_ref[...])

    return u, k == last


def _ffn_kernel(f_ref, h_ref, wu_ref, wd_ref, g2_ref, gn_ref, o_ref, a_ref, acc_ref):
    u, is_last = _ffn_accumulate(f_ref, wu_ref, wd_ref, acc_ref)

    @pl.when(is_last)
    def _():
        for r in range(0, TM_OUT, SLAB):
            rows = slice(r, r + SLAB)
            total = acc_ref[rows, :] + _dot(u[rows, :], wd_ref[...])
            hn = h_ref[rows, :] + _rms_scale(total, g2_ref[...])
            o_ref[rows, :] = hn
            a_ref[rows, :] = _rms_scale(hn, gn_ref[...]).astype(BF16)


def _ffn_final_kernel(f_ref, h_ref, wu_ref, wd_ref, g2_ref, yp_ref, ys_ref, acc_ref):
    u, is_last = _ffn_accumulate(f_ref, wu_ref, wd_ref, acc_ref)

    def finish(o_ref):
        for r in range(0, TM_OUT, SLAB):
            rows = slice(r, r + SLAB)
            total = acc_ref[rows, :] + _dot(u[rows, :], wd_ref[...])
            o_ref[rows, :] = h_ref[rows, :] + _rms_scale(total, g2_ref[...])

    pl.when(jnp.logical_and(is_last, _is_prompt_tile()))(lambda: finish(yp_ref))
    pl.when(jnp.logical_and(is_last, jnp.logical_not(_is_prompt_tile())))(lambda: finish(ys_ref))


def _ffn_in_specs(layer, n_gains):
    return [
        pl.BlockSpec((TM_OUT, D_MODEL), lambda i, k: (i, 0)),
        pl.BlockSpec((TM_OUT, D_MODEL), lambda i, k: (i, 0)),
        pl.BlockSpec((None, D_MODEL, TF), lambda i, k: (layer, 0, k)),
        pl.BlockSpec((None, TF, D_MODEL), lambda i, k: (layer, k, 0)),
    ] + [pl.BlockSpec((1, D_MODEL), lambda i, k: (0, 0))] * n_gains


def _ffn(f, h, w_up, w_down, layer, g2, g_next):
    return pl.pallas_call(
        _ffn_kernel,
        grid=(N_TOK // TM_OUT, D_FF // TF),
        in_specs=_ffn_in_specs(layer, 2),
        out_specs=[
            pl.BlockSpec((TM_OUT, D_MODEL), lambda i, k: (i, 0)),
            pl.BlockSpec((TM_OUT, D_MODEL), lambda i, k: (i, 0)),
        ],
        out_shape=[
            jax.ShapeDtypeStruct((N_TOK, D_MODEL), F32),
            jax.ShapeDtypeStruct((N_TOK, D_MODEL), BF16),
        ],
        scratch_shapes=[pltpu.VMEM((TM_OUT, D_MODEL), F32)],
        compiler_params=_params("parallel", "arbitrary"),
        name="relu2_mlp",
    )(f, h, w_up, w_down, g2.reshape(1, D_MODEL), g_next.reshape(1, D_MODEL))


def _ffn_final(f, h, w_up, w_down, layer, g2):
    return pl.pallas_call(
        _ffn_final_kernel,
        grid=(N_TOK // TM_OUT, D_FF // TF),
        in_specs=_ffn_in_specs(layer, 1),
        out_specs=[
            pl.BlockSpec((TM_OUT, D_MODEL), lambda i, k: (jnp.minimum(i, N_PROMPT_TILES - 1), 0)),
            pl.BlockSpec((TM_OUT, D_MODEL), lambda i, k: (0, 0)),
        ],
        out_shape=[
            jax.ShapeDtypeStruct((N_PROMPT, D_MODEL), F32),
            jax.ShapeDtypeStruct((N_SAMPLE, D_MODEL), F32),
        ],
        scratch_shapes=[pltpu.VMEM((TM_OUT, D_MODEL), F32)],
        compiler_params=_params("arbitrary", "arbitrary"),
        name="relu2_mlp_final",
    )(f, h, w_up, w_down, g2.reshape(1, D_MODEL))


def _rotary_tables(pos):
    half = RET_DK // 2
    inv = ROPE_BASE ** (-np.arange(half, dtype=np.float64) / half)
    ang = np.asarray(pos, np.float64)[:, None] * inv[None, :]
    cos = np.concatenate([np.cos(ang), np.cos(ang)], axis=-1)
    sin = np.concatenate([-np.sin(ang), np.sin(ang)], axis=-1)
    return jnp.asarray(cos, F32), jnp.asarray(sin, F32)


def _log_gamma():
    return np.log1p(-np.exp2(-5.0 - np.arange(RET_HEADS, dtype=np.float64)))


def _retention_tables(length):
    log_g = _log_gamma()
    idx = np.arange(length, dtype=np.float64)
    diff = idx[:, None] - idx[None, :]
    mask = np.where(diff[None] >= 0, np.exp(log_g[:, None, None] * np.maximum(diff, 0.0)[None]), 0.0)
    qd = np.exp(log_g[:, None] * (idx + 1.0))
    kd = np.exp(log_g[:, None] * (length - 1.0 - idx))
    cd = np.exp(log_g * length)
    return mask, qd, kd, cd


def _lane_bcast(a, width):
    return np.repeat(a[..., None], width, axis=-1)


def _retention_head_out(o, gate):
    o = o * lax.rsqrt(jnp.mean(o * o, axis=-1, keepdims=True) + EPS)
    return (gate * jax.nn.sigmoid(gate) * o).astype(BF16)


def _rotate(x, cos, sin):
    return x * cos + pltpu.roll(x, RET_DK // 2, 1) * sin


def _even_prompt_kernel(z_ref, cos_ref, sin_ref, mask_ref, qd_ref, kd_ref, wp_ref, sp_ref,
                        y_ref, hist_ref, st_ref, pext_ref, s_ref, *, chunk_decay):
    n = pl.program_id(1)
    last = pl.num_programs(1) - 1

    @pl.when(n == 0)
    def _():
        pext_ref[0:POOL_PAD, :] = jnp.zeros((POOL_PAD, POOL_WIDTH), F32)
        s_ref[...] = jnp.zeros_like(s_ref)

    pext_ref[POOL_PAD:POOL_PAD + CHUNK, :] = z_ref[:, 0:POOL_WIDTH]
    pos = n * CHUNK + lax.broadcasted_iota(jnp.int32, (CHUNK, 1), 0)
    for g, w in enumerate(POOL_WINDOWS):
        lanes = slice(g * POOL_GROUP_DIM, (g + 1) * POOL_GROUP_DIM)
        p = pext_ref[POOL_PAD:POOL_PAD + CHUNK, lanes]
        acc = p
        for i in range(1, w):
            acc = acc + pext_ref[POOL_PAD - i:POOL_PAD - i + CHUNK, lanes]
        cnt = jnp.minimum(w, pos + 1).astype(F32)
        d = acc / cnt - p
        yg = _dot(d.astype(BF16), wp_ref[g].astype(BF16)) * sp_ref[:, lanes]
        y_ref[:, lanes] = yg.astype(BF16)

    @pl.when(n == last)
    def _():
        hist_ref[0] = pext_ref[POOL_PAD + CHUNK - POOL_HIST:POOL_PAD + CHUNK, :]

    pext_ref[0:POOL_PAD, :] = pext_ref[CHUNK:CHUNK + POOL_PAD, :]

    cos = cos_ref[...]
    sin = sin_ref[...]
    for h in range(RET_HEADS):
        q = _rotate(z_ref[:, Q_OFF + h * RET_DK:Q_OFF + (h + 1) * RET_DK], cos, sin)
        k = _rotate(z_ref[:, K_OFF + h * RET_DK:K_OFF + (h + 1) * RET_DK], cos, sin) * (RET_DK ** -0.5)
        v = z_ref[:, V_OFF + h * RET_DV:V_OFF + (h + 1) * RET_DV].astype(BF16)
        gate = z_ref[:, G_OFF + h * RET_DV:G_OFF + (h + 1) * RET_DV]
        s = s_ref[h]
        scores = _dot_nt(q.astype(BF16), k.astype(BF16)) * mask_ref[h]
        o = _dot(scores.astype(BF16), v) + _dot((q * qd_ref[h]).astype(BF16), s.astype(BF16))
        s_ref[h] = s * chunk_decay[h] + _dot_tn((k * kd_ref[h]).astype(BF16), v)
        y_ref[:, POOL_WIDTH + h * RET_DV:POOL_WIDTH + (h + 1) * RET_DV] = _retention_head_out(o, gate)

    @pl.when(n == last)
    def _():
        st_ref[0] = s_ref[...]


def _even_prompt(z, w_pool, s_pool, layer, prev_hist, prev_state):
    cos, sin = _rotary_tables(np.arange(SEQ))
    mask, qd, kd, cd = _retention_tables(CHUNK)
    const = lambda *shape: pl.BlockSpec(shape, lambda b, n: (0,) * len(shape))
    return _call_stacked(
        functools.partial(_even_prompt_kernel, chunk_decay=tuple(float(c) for c in cd)),
        name="even_mixer_prompt",
        grid=(BATCH, N_CHUNKS),
        in_specs=[
            pl.BlockSpec((CHUNK, EVEN_IN), lambda b, n: (b * N_CHUNKS + n, 0)),
            pl.BlockSpec((CHUNK, RET_DK), lambda b, n: (n, 0)),
            pl.BlockSpec((CHUNK, RET_DK), lambda b, n: (n, 0)),
            const(RET_HEADS, CHUNK, CHUNK),
            const(RET_HEADS, CHUNK, RET_DK),
            const(RET_HEADS, CHUNK, RET_DK),
            pl.BlockSpec((None, POOL_GROUPS, POOL_GROUP_DIM, POOL_GROUP_DIM), lambda b, n: (layer, 0, 0, 0)),
            pl.BlockSpec((None, 1, POOL_WIDTH), lambda b, n: (layer, 0, 0)),
        ],
        args=(z, cos, sin, jnp.asarray(mask, F32), jnp.asarray(_lane_bcast(qd, RET_DK), F32),
              jnp.asarray(_lane_bcast(kd, RET_DK), F32), w_pool, s_pool.reshape(N_EVEN, 1, POOL_WIDTH)),
        out_specs=[
            pl.BlockSpec((CHUNK, D_MODEL), lambda b, n: (b * N_CHUNKS + n, 0)),
            pl.BlockSpec((None, 1, POOL_HIST, POOL_WIDTH), lambda b, n: (layer, b, 0, 0)),
            pl.BlockSpec((None, 1, RET_HEADS, RET_DK, RET_DV), lambda b, n: (layer, b, 0, 0, 0)),
        ],
        out_shape=[
            jax.ShapeDtypeStruct((N_PROMPT, D_MODEL), BF16),
            jax.ShapeDtypeStruct((N_EVEN, BATCH, POOL_HIST, POOL_WIDTH), F32),
            jax.ShapeDtypeStruct((N_EVEN, BATCH, RET_HEADS, RET_DK, RET_DV), F32),
        ],
        stacked={1: prev_hist, 2: prev_state},
        sem=("parallel", "arbitrary"),
        scratch_shapes=[
            pltpu.VMEM((POOL_PAD + CHUNK, POOL_WIDTH), F32),
            pltpu.VMEM((RET_HEADS, RET_DK, RET_DV), F32),
        ],
    )


def _even_sample_kernel(z_ref, cos_ref, sin_ref, mask_ref, qd_ref, kd_ref, wp_ref, sp_ref, hist_ref, st_ref,
                        y_ref, nhist_ref, nst_ref, pext_ref, d_ref, *, chunk_decay):
    t_idx = lax.broadcasted_iota(jnp.int32, (DEC_SEQ, 1), 0)
    for b in range(SAMPLE_BB):
        rows = slice(b * DEC_SEQ, (b + 1) * DEC_SEQ)
        pext_ref[0:POOL_HIST, :] = hist_ref[b]
        pext_ref[POOL_HIST:POOL_HIST + DEC_SEQ, :] = z_ref[rows, 0:POOL_WIDTH]
        for g, w in enumerate(POOL_WINDOWS):
            lanes = slice(g * POOL_GROUP_DIM, (g + 1) * POOL_GROUP_DIM)
            p = pext_ref[POOL_HIST:POOL_HIST + DEC_SEQ, lanes]
            acc = p
            for i in range(1, w):
                acc = acc + pext_ref[POOL_HIST - i:POOL_HIST - i + DEC_SEQ, lanes]
            cnt = jnp.minimum(w, PAST_LEN + t_idx + 1).astype(F32)
            d_ref[rows, lanes] = acc / cnt - p
        nhist_ref[b] = pext_ref[DEC_SEQ:DEC_SEQ + POOL_HIST, :]
    for g in range(POOL_GROUPS):
        lanes = slice(g * POOL_GROUP_DIM, (g + 1) * POOL_GROUP_DIM)
        yg = _dot(d_ref[:, lanes].astype(BF16), wp_ref[g].astype(BF16)) * sp_ref[:, lanes]
        y_ref[:, lanes] = yg.astype(BF16)

    cos = cos_ref[...]
    sin = sin_ref[...]
    seq_of_row = lax.broadcasted_iota(jnp.int32, (SAMPLE_ROWS, 1), 0) // DEC_SEQ
    for h in range(RET_HEADS):
        q = _rotate(z_ref[:, Q_OFF + h * RET_DK:Q_OFF + (h + 1) * RET_DK], cos, sin)
        k = _rotate(z_ref[:, K_OFF + h * RET_DK:K_OFF + (h + 1) * RET_DK], cos, sin) * (RET_DK ** -0.5)
        v = z_ref[:, V_OFF + h * RET_DV:V_OFF + (h + 1) * RET_DV].astype(BF16)
        gate = z_ref[:, G_OFF + h * RET_DV:G_OFF + (h + 1) * RET_DV]
        scores = _dot_nt(q.astype(BF16), k.astype(BF16)) * mask_ref[h]
        o = _dot(scores.astype(BF16), v)
        q_dec = (q * qd_ref[h]).astype(BF16)
        k_dec = k * kd_ref[h]
        for b in range(SAMPLE_BB):
            own = seq_of_row == b
            s = st_ref[b, h]
            o = o + jnp.where(own, _dot(q_dec, s.astype(BF16)), 0.0)
            nst_ref[b, h] = s * chunk_decay[h] + _dot_tn(jnp.where(own, k_dec, 0.0).astype(BF16), v)
        y_ref[:, POOL_WIDTH + h * RET_DV:POOL_WIDTH + (h + 1) * RET_DV] = _retention_head_out(o, gate)


def _even_sample(z, w_pool, s_pool, state_pool, state_ret, layer, prev_hist, prev_state):
    pos = PAST_LEN + np.arange(DEC_SEQ)
    cos, sin = _rotary_tables(np.tile(pos, SAMPLE_BB))
    mask, qd, kd, cd = _retention_tables(DEC_SEQ)
    mask = np.stack([np.kron(np.eye(SAMPLE_BB), m) for m in mask])
    qd = _lane_bcast(np.tile(qd, (1, SAMPLE_BB)), RET_DK)
    kd = _lane_bcast(np.tile(kd, (1, SAMPLE_BB)), RET_DK)
    const = lambda *shape: pl.BlockSpec(shape, lambda i: (0,) * len(shape))
    first = N_PROMPT // SAMPLE_ROWS
    return _call_stacked(
        functools.partial(_even_sample_kernel, chunk_decay=tuple(float(c) for c in cd)),
        name="even_mixer_sample",
        grid=(DEC_BATCH // SAMPLE_BB,),
        in_specs=[
            pl.BlockSpec((SAMPLE_ROWS, EVEN_IN), lambda i: (first + i, 0)),
            const(SAMPLE_ROWS, RET_DK),
            const(SAMPLE_ROWS, RET_DK),
            const(RET_HEADS, SAMPLE_ROWS, SAMPLE_ROWS),
            const(RET_HEADS, SAMPLE_ROWS, RET_DK),
            const(RET_HEADS, SAMPLE_ROWS, RET_DK),
            pl.BlockSpec((None, POOL_GROUPS, POOL_GROUP_DIM, POOL_GROUP_DIM), lambda i: (layer, 0, 0, 0)),
            pl.BlockSpec((None, 1, POOL_WIDTH), lambda i: (layer, 0, 0)),
            pl.BlockSpec((None, SAMPLE_BB, POOL_HIST, POOL_WIDTH), lambda i: (layer, i, 0, 0)),
            pl.BlockSpec((None, SAMPLE_BB, RET_HEADS, RET_DK, RET_DV), lambda i: (layer, i, 0, 0, 0)),
        ],
        args=(z, cos, sin, jnp.asarray(mask, F32), jnp.asarray(qd, F32), jnp.asarray(kd, F32),
              w_pool, s_pool.reshape(N_EVEN, 1, POOL_WIDTH), state_pool, state_ret),
        out_specs=[
            pl.BlockSpec((SAMPLE_ROWS, D_MODEL), lambda i: (i, 0)),
            pl.BlockSpec((None, SAMPLE_BB, POOL_HIST, POOL_WIDTH), lambda i: (layer, i, 0, 0)),
            pl.BlockSpec((None, SAMPLE_BB, RET_HEADS, RET_DK, RET_DV), lambda i: (layer, i, 0, 0, 0)),
        ],
        out_shape=[
            jax.ShapeDtypeStruct((N_SAMPLE, D_MODEL), BF16),
            jax.ShapeDtypeStruct((N_EVEN, DEC_BATCH, POOL_HIST, POOL_WIDTH), F32),
            jax.ShapeDtypeStruct((N_EVEN, DEC_BATCH, RET_HEADS, RET_DK, RET_DV), F32),
        ],
        stacked={1: prev_hist, 2: prev_state},
        sem=("parallel",),
        scratch_shapes=[
            pltpu.VMEM((POOL_HIST + DEC_SEQ, POOL_WIDTH), F32),
            pltpu.VMEM((SAMPLE_ROWS, POOL_WIDTH), F32),
        ],
    )


CONV_LANES = 128


def _depthwise_conv(ext_ref, first_row, n_rows, dw_ref, dwb_ref, out_ref, out_row):
    for c in range(0, CONV_CH, CONV_LANES):
        lanes = slice(c, c + CONV_LANES)
        acc = jnp.broadcast_to(dwb_ref[:, lanes], (n_rows, CONV_LANES))
        for j in range(CONV_K):
            acc = acc + ext_ref[first_row + j:first_row + j + n_rows, lanes] * dw_ref[j:j + 1, lanes]
        out_ref[out_row:out_row + n_rows, lanes] = acc


SUBLANES = 8
SHIFTED_ROWS = CONV_PAD + CHUNK - SUBLANES


def _depthwise_conv_chunk(ext_ref, xs_ref, dw_ref, dwb_ref, out_ref):
    for s in range(1, SUBLANES):
        xs_ref[s - 1] = ext_ref[s:s + SHIFTED_ROWS, :]
    for c in range(0, CONV_CH, CONV_LANES):
        lanes = slice(c, c + CONV_LANES)
        acc = jnp.broadcast_to(dwb_ref[:, lanes], (CHUNK, CONV_LANES))
        for j in range(CONV_K):
            tile, s = divmod(CONV_PAD - CONV_HIST + j, SUBLANES)
            rows = slice(tile * SUBLANES, tile * SUBLANES + CHUNK)
            window = ext_ref[rows, lanes] if s == 0 else xs_ref[s - 1, rows, lanes]
            acc = acc + window * dw_ref[j:j + 1, lanes]
        out_ref[:, lanes] = acc


def _odd_prompt_kernel(z_ref, lng_ref, lnb_ref, sgw_ref, sgb_ref, dw_ref, dwb_ref, cvg_ref, cvb_ref,
                       y_ref, cst_ref, ext_ref, cv_ref, xs_ref):
    n = pl.program_id(1)

    @pl.when(n == 0)
    def _():
        ext_ref[0:CONV_PAD, :] = jnp.zeros((CONV_PAD, CONV_CH), F32)

    u = jax.nn.gelu(z_ref[:, 0:SG_WIDTH])
    v = _layer_norm(jax.nn.gelu(z_ref[:, SG_WIDTH:2 * SG_WIDTH]), lng_ref[...], lnb_ref[...])
    row = lax.broadcasted_iota(jnp.int32, (CHUNK, CHUNK), 0)
    col = lax.broadcasted_iota(jnp.int32, (CHUNK, CHUNK), 1)
    for g in range(SG_GROUPS):
        lanes = slice(g * SG_GROUP_DIM, (g + 1) * SG_GROUP_DIM)
        ws = jnp.where(col <= row, sgw_ref[g], 0.0).astype(BF16)
        mixed = _dot(ws, v[:, lanes].astype(BF16)) + sgb_ref[:, g:g + 1]
        y_ref[:, lanes] = (u[:, lanes] * mixed).astype(BF16)

    a = z_ref[:, 2 * SG_WIDTH:2 * SG_WIDTH + CONV_CH]
    gate = z_ref[:, 2 * SG_WIDTH + CONV_CH:2 * SG_WIDTH + 2 * CONV_CH]
    ext_ref[CONV_PAD:CONV_PAD + CHUNK, :] = a * jax.nn.sigmoid(gate)
    _depthwise_conv_chunk(ext_ref, xs_ref, dw_ref, dwb_ref, cv_ref)
    yd = _layer_norm(cv_ref[...], cvg_ref[...], cvb_ref[...])
    y_ref[:, SG_WIDTH:SG_WIDTH + CONV_CH] = (yd * jax.nn.sigmoid(yd)).astype(BF16)

    @pl.when(n == pl.num_programs(1) - 1)
    def _():
        cst_ref[0] = ext_ref[CONV_PAD + CHUNK - CONV_HIST:CONV_PAD + CHUNK, :]

    ext_ref[0:CONV_PAD, :] = ext_ref[CHUNK:CHUNK + CONV_PAD, :]


def _odd_weight_specs(layer):
    per_layer = lambda *shape: pl.BlockSpec((None,) + shape, lambda *_: (layer,) + (0,) * len(shape))
    return dict(
        ln=per_layer(1, SG_WIDTH),
        sgw=per_layer(SG_GROUPS, SG_CHUNK, SG_CHUNK),
        sgb=per_layer(SG_CHUNK, SG_GROUPS),
        dw=per_layer(CONV_K, CONV_CH),
        ch=per_layer(1, CONV_CH),
    )


def _odd_prompt(z, sg_ln_g, sg_ln_b, sg_w, sg_b, dw_w, dw_b, cv_ln_g, cv_ln_b, layer, prev_cst):
    spec = _odd_weight_specs(layer)
    row = lambda a: a.reshape(N_ODD, 1, -1)
    return _call_stacked(
        _odd_prompt_kernel,
        name="odd_mixer_prompt",
        grid=(BATCH, N_CHUNKS),
        in_specs=[
            pl.BlockSpec((CHUNK, ODD_IN), lambda b, n: (b * N_CHUNKS + n, 0)),
            spec["ln"], spec["ln"], spec["sgw"], spec["sgb"], spec["dw"], spec["ch"], spec["ch"], spec["ch"],
        ],
        args=(z, row(sg_ln_g), row(sg_ln_b), sg_w, jnp.swapaxes(sg_b, 1, 2), dw_w, row(dw_b),
              row(cv_ln_g), row(cv_ln_b)),
        out_specs=[
            pl.BlockSpec((CHUNK, D_MODEL), lambda b, n: (b * N_CHUNKS + n, 0)),
            pl.BlockSpec((None, 1, CONV_HIST, CONV_CH), lambda b, n: (layer, b, 0, 0)),
        ],
        out_shape=[
            jax.ShapeDtypeStruct((N_PROMPT, D_MODEL), BF16),
            jax.ShapeDtypeStruct((N_ODD, BATCH, CONV_HIST, CONV_CH), F32),
        ],
        stacked={1: prev_cst},
        sem=("parallel", "arbitrary"),
        scratch_shapes=[
            pltpu.VMEM((CONV_PAD + CHUNK, CONV_CH), F32),
            pltpu.VMEM((CHUNK, CONV_CH), F32),
            pltpu.VMEM((SUBLANES - 1, SHIFTED_ROWS, CONV_CH), F32),
        ],
    )


SG_SHIFT_PAD = 8


def _odd_sample_kernel(sgw_ref, sgb_ref, z_ref, lng_ref, lnb_ref, dw_ref, dwb_ref, cvg_ref, cvb_ref, cst_ref,
                       y_ref, sgv_ref, ncst_ref, vs_ref, ext_ref, cv_ref):
    u = jax.nn.gelu(z_ref[:, 0:SG_WIDTH])
    v = _layer_norm(jax.nn.gelu(z_ref[:, SG_WIDTH:2 * SG_WIDTH]), lng_ref[...], lnb_ref[...])
    sgv_ref[...] = v
    vs_ref[0:SG_SHIFT_PAD, :] = jnp.zeros((SG_SHIFT_PAD, SG_WIDTH), F32)
    vs_ref[SG_SHIFT_PAD:SG_SHIFT_PAD + SAMPLE_ROWS, :] = v
    t_of_row = lax.broadcasted_iota(jnp.int32, (SAMPLE_ROWS, 1), 0) % DEC_SEQ
    for g in range(SG_GROUPS):
        lanes = slice(g * SG_GROUP_DIM, (g + 1) * SG_GROUP_DIM)
        mixed = jnp.zeros((SAMPLE_ROWS, SG_GROUP_DIM), F32)
        for back in range(DEC_SEQ):
            coef = jnp.zeros((SAMPLE_ROWS, 1), F32)
            for t in range(back, DEC_SEQ):
                coef = jnp.where(t_of_row == t, sgw_ref[(g * DEC_SEQ + t) * DEC_SEQ + t - back], coef)
            shifted = vs_ref[SG_SHIFT_PAD - back:SG_SHIFT_PAD - back + SAMPLE_ROWS, lanes]
            mixed = mixed + coef * shifted
        bias = jnp.zeros((SAMPLE_ROWS, 1), F32)
        for t in range(DEC_SEQ):
            bias = jnp.where(t_of_row == t, sgb_ref[g * DEC_SEQ + t], bias)
        y_ref[:, lanes] = (u[:, lanes] * (mixed + bias)).astype(BF16)

    a = z_ref[:, 2 * SG_WIDTH:2 * SG_WIDTH + CONV_CH]
    gate = z_ref[:, 2 * SG_WIDTH + CONV_CH:2 * SG_WIDTH + 2 * CONV_CH]
    glu = a * jax.nn.sigmoid(gate)
    for b in range(SAMPLE_BB):
        rows = slice(b * DEC_SEQ, (b + 1) * DEC_SEQ)
        ext_ref[0:CONV_HIST, :] = cst_ref[b]
        ext_ref[CONV_HIST:CONV_HIST + DEC_SEQ, :] = glu[rows, :]
        _depthwise_conv(ext_ref, 0, DEC_SEQ, dw_ref, dwb_ref, cv_ref, b * DEC_SEQ)
        ncst_ref[b] = ext_ref[DEC_SEQ:DEC_SEQ + CONV_HIST, :]
    yd = _layer_norm(cv_ref[...], cvg_ref[...], cvb_ref[...])
    y_ref[:, SG_WIDTH:SG_WIDTH + CONV_CH] = (yd * jax.nn.sigmoid(yd)).astype(BF16)


def _odd_sample(z, sg_ln_g, sg_ln_b, sg_w, sg_b, dw_w, dw_b, cv_ln_g, cv_ln_b, state_conv, layer,
                prev_sgv, prev_cst):
    spec = _odd_weight_specs(layer)
    row = lambda a: a.reshape(N_ODD, 1, -1)
    smem = pl.BlockSpec(memory_space=pltpu.SMEM)
    first = N_PROMPT // SAMPLE_ROWS
    return _call_stacked(
        _odd_sample_kernel,
        name="odd_mixer_sample",
        grid=(DEC_BATCH // SAMPLE_BB,),
        in_specs=[
            smem,
            smem,
            pl.BlockSpec((SAMPLE_ROWS, ODD_IN), lambda i: (first + i, 0)),
            spec["ln"], spec["ln"], spec["dw"], spec["ch"], spec["ch"], spec["ch"],
            pl.BlockSpec((None, SAMPLE_BB, CONV_HIST, CONV_CH), lambda i: (layer, i, 0, 0)),
        ],
        args=(sg_w[layer, :, :DEC_SEQ, :DEC_SEQ].reshape(-1), sg_b[layer, :, :DEC_SEQ].reshape(-1), z,
              row(sg_ln_g), row(sg_ln_b), dw_w, row(dw_b), row(cv_ln_g), row(cv_ln_b), state_conv),
        out_specs=[
            pl.BlockSpec((SAMPLE_ROWS, D_MODEL), lambda i: (i, 0)),
            pl.BlockSpec((None, SAMPLE_ROWS, SG_WIDTH), lambda i: (layer, i, 0)),
            pl.BlockSpec((None, SAMPLE_BB, CONV_HIST, CONV_CH), lambda i: (layer, i, 0, 0)),
        ],
        out_shape=[
            jax.ShapeDtypeStruct((N_SAMPLE, D_MODEL), BF16),
            jax.ShapeDtypeStruct((N_ODD, N_SAMPLE, SG_WIDTH), F32),
            jax.ShapeDtypeStruct((N_ODD, DEC_BATCH, CONV_HIST, CONV_CH), F32),
        ],
        stacked={1: prev_sgv, 2: prev_cst},
        sem=("parallel",),
        scratch_shapes=[
            pltpu.VMEM((SG_SHIFT_PAD + SAMPLE_ROWS, SG_WIDTH), F32),
            pltpu.VMEM((CONV_HIST + DEC_SEQ, CONV_CH), F32),
            pltpu.VMEM((SAMPLE_ROWS, CONV_CH), F32),
        ],
    )


def kernel(x_prompt, x_sample, state_pool, state_ret, state_conv, norm_mix_pre, norm_mix_post, norm_ffn_pre, norm_ffn_post, w_in_even, w_pool, s_pool, w_out_even, w_in_odd, sg_ln_g, sg_ln_b, sg_w, sg_b, dw_w, dw_b, cv_ln_g, cv_ln_b, w_out_odd, w_up, w_down):
    w_in_even, w_out_even, w_in_odd, w_out_odd, w_up, w_down = (
        w.astype(BF16) for w in (w_in_even, w_out_even, w_in_odd, w_out_odd, w_up, w_down))

    h, a = _prenorm(x_prompt, x_sample, norm_mix_pre[0])
    pool_p = pool_s = ret_p = ret_s = conv_p = conv_s = sgv_s = None
    for l in range(DEPTH):
        i = l // 2
        if l % 2 == 0:
            z = _in_proj(a, w_in_even, i)
            y_p, pool_p, ret_p = _even_prompt(z, w_pool, s_pool, i, pool_p, ret_p)
            y_s, pool_s, ret_s = _even_sample(z, w_pool, s_pool, state_pool, state_ret, i, pool_s, ret_s)
            w_out = w_out_even
        else:
            z = _in_proj(a, w_in_odd, i)
            y_p, conv_p = _odd_prompt(z, sg_ln_g, sg_ln_b, sg_w, sg_b, dw_w, dw_b, cv_ln_g, cv_ln_b, i, conv_p)
            y_s, sgv_s, conv_s = _odd_sample(z, sg_ln_g, sg_ln_b, sg_w, sg_b, dw_w, dw_b, cv_ln_g, cv_ln_b,
                                             state_conv, i, sgv_s, conv_s)
            w_out = w_out_odd
        h, f = _out_proj(y_p, y_s, w_out, i, norm_mix_post[l], norm_ffn_pre[l], h)
        if l + 1 < DEPTH:
            h, a = _ffn(f, h, w_up, w_down, l, norm_ffn_post[l], norm_mix_pre[l + 1])
        else:
            y_prompt, y_sample = _ffn_final(f, h, w_up, w_down, l, norm_ffn_post[l])

    return (y_prompt.reshape(BATCH, SEQ, D_MODEL), y_sample.reshape(DEC_BATCH, DEC_SEQ, D_MODEL),
            pool_p, pool_s, ret_p, ret_s, conv_p, conv_s, sgv_s.reshape(N_ODD, DEC_BATCH, DEC_SEQ, SG_WIDTH))
```

```python
import functools

import jax
import jax.numpy as jnp
import numpy as np
from jax import lax
from jax.experimental import pallas as pl
from jax.experimental.pallas import tpu as pltpu

F32 = jnp.float32
BF16 = jnp.bfloat16

D_MODEL = 2048
BATCH = 4
SEQ = 2048
DEPTH = 4
DEC_BATCH = 128
DEC_SEQ = 4
PAST_LEN = 16384

N_EVEN = (DEPTH + 1) // 2
N_ODD = DEPTH // 2

POOL_WINDOWS = (2, 4, 8, 16)
POOL_GROUPS = len(POOL_WINDOWS)
POOL_WIDTH = D_MODEL // 4
POOL_GROUP_DIM = POOL_WIDTH // POOL_GROUPS
POOL_HIST = max(POOL_WINDOWS) - 1
RET_WIDTH = D_MODEL - POOL_WIDTH
RET_HEADS = 6
RET_DV = RET_WIDTH // RET_HEADS
RET_DK = RET_DV // 2
RET_QK = RET_HEADS * RET_DK
RET_CHUNK = 128
ROPE_BASE = 10000.0
SG_WIDTH = D_MODEL // 2
SG_CHUNK = 128
SG_GROUPS = 4
SG_GROUP_DIM = SG_WIDTH // SG_GROUPS
CONV_CH = D_MODEL // 2
CONV_K = 31
CONV_HIST = CONV_K - 1
D_FF = 4 * D_MODEL
EPS = 1e-6

EVEN_IN = POOL_WIDTH + 2 * RET_QK + 2 * RET_WIDTH
ODD_IN = 2 * SG_WIDTH + 2 * CONV_CH

Q_OFF = POOL_WIDTH
K_OFF = Q_OFF + RET_QK
V_OFF = K_OFF + RET_QK
G_OFF = V_OFF + RET_WIDTH

N_PROMPT = BATCH * SEQ
N_SAMPLE = DEC_BATCH * DEC_SEQ
N_TOK = N_PROMPT + N_SAMPLE

CHUNK = 128
N_CHUNKS = SEQ // CHUNK
SAMPLE_BB = 4
SAMPLE_ROWS = SAMPLE_BB * DEC_SEQ
POOL_PAD = 16
CONV_PAD = 32

TN_IN = 1024
TM_OUT = 512
SLAB = 128
TF = 1024
VMEM_LIMIT = 56 * 1024 * 1024

N_PROMPT_TILES = N_PROMPT // TM_OUT
assert N_PROMPT % TM_OUT == 0 and N_SAMPLE == TM_OUT and TM_OUT % SLAB == 0
assert POOL_PAD >= POOL_HIST and CONV_PAD >= CONV_HIST


def _params(*sem):
    return pltpu.CompilerParams(dimension_semantics=sem, vmem_limit_bytes=VMEM_LIMIT)


def _rms_scale(x, g):
    return x * lax.rsqrt(jnp.mean(x * x, axis=-1, keepdims=True) + EPS) * g


def _layer_norm(x, g, b):
    xc = x - jnp.mean(x, axis=-1, keepdims=True)
    return xc * lax.rsqrt(jnp.mean(xc * xc, axis=-1, keepdims=True) + EPS) * g + b


def _dot(a, b):
    return jnp.dot(a, b, preferred_element_type=F32)


def _dot_nt(a, b):
    return lax.dot_general(a, b, (((1,), (1,)), ((), ())), preferred_element_type=F32)


def _dot_tn(a, b):
    return lax.dot_general(a, b, (((0,), (0,)), ((), ())), preferred_element_type=F32)


def _skip_aliased(body, n_in, n_aliased):
    def wrapped(*refs):
        return body(*refs[:n_in], *refs[n_in + n_aliased:])
    return wrapped


def _call_stacked(body, *, name, grid, in_specs, args, out_specs, out_shape, stacked, sem, scratch_shapes):
    prev = [(o, p) for o, p in sorted(stacked.items()) if p is not None]
    n_in = len(args)
    return pl.pallas_call(
        _skip_aliased(body, n_in, len(prev)),
        grid=grid,
        in_specs=list(in_specs) + [pl.BlockSpec(memory_space=pl.ANY)] * len(prev),
        out_specs=out_specs,
        out_shape=out_shape,
        input_output_aliases={n_in + j: o for j, (o, _) in enumerate(prev)},
        scratch_shapes=scratch_shapes,
        compiler_params=_params(*sem),
        name=name,
    )(*args, *[p for _, p in prev])


def _is_prompt_tile():
    return pl.program_id(0) < N_PROMPT_TILES


def _prenorm_kernel(xp_ref, xs_ref, g_ref, h_ref, a_ref):
    def emit(x_ref):
        x = x_ref[...]
        h_ref[...] = x
        a_ref[...] = _rms_scale(x, g_ref[...]).astype(BF16)

    pl.when(_is_prompt_tile())(lambda: emit(xp_ref))
    pl.when(jnp.logical_not(_is_prompt_tile()))(lambda: emit(xs_ref))


def _prenorm(x_prompt, x_sample, g):
    return pl.pallas_call(
        _prenorm_kernel,
        grid=(N_TOK // TM_OUT,),
        in_specs=[
            pl.BlockSpec((TM_OUT, D_MODEL), lambda i: (jnp.minimum(i, N_PROMPT_TILES - 1), 0)),
            pl.BlockSpec((TM_OUT, D_MODEL), lambda i: (0, 0)),
            pl.BlockSpec((1, D_MODEL), lambda i: (0, 0)),
        ],
        out_specs=[
            pl.BlockSpec((TM_OUT, D_MODEL), lambda i: (i, 0)),
            pl.BlockSpec((TM_OUT, D_MODEL), lambda i: (i, 0)),
        ],
        out_shape=[
            jax.ShapeDtypeStruct((N_TOK, D_MODEL), F32),
            jax.ShapeDtypeStruct((N_TOK, D_MODEL), BF16),
        ],
        compiler_params=_params("parallel"),
        name="join_prenorm",
    )(x_prompt.reshape(N_PROMPT, D_MODEL), x_sample.reshape(N_SAMPLE, D_MODEL), g.reshape(1, D_MODEL))


def _in_proj_kernel(a_ref, w_ref, o_ref):
    o_ref[...] = _dot(a_ref[...], w_ref[...])


def _in_proj_sample(a, w, layer):
    n_out = w.shape[2]
    return pl.pallas_call(
        _in_proj_kernel,
        grid=(n_out // TN_IN,),
        in_specs=[
            pl.BlockSpec((N_SAMPLE, D_MODEL), lambda j: (N_PROMPT // N_SAMPLE, 0)),
            pl.BlockSpec((None, D_MODEL, TN_IN), lambda j: (layer, 0, j)),
        ],
        out_specs=pl.BlockSpec((N_SAMPLE, TN_IN), lambda j: (0, j)),
        out_shape=jax.ShapeDtypeStruct((N_SAMPLE, n_out), F32),
        compiler_params=_params("parallel"),
        name="in_proj_sample",
    )(a, w)


def _out_proj_kernel(yp_ref, ys_ref, w_ref, g_ref, gf_ref, h_ref, o_ref, f_ref):
    def finish(y_ref):
        for r in range(0, TM_OUT, SLAB):
            rows = slice(r, r + SLAB)
            hn = h_ref[rows, :] + _rms_scale(_dot(y_ref[rows, :], w_ref[...]), g_ref[...])
            o_ref[rows, :] = hn
            f_ref[rows, :] = _rms_scale(hn, gf_ref[...]).astype(BF16)

    pl.when(_is_prompt_tile())(lambda: finish(yp_ref))
    pl.when(jnp.logical_not(_is_prompt_tile()))(lambda: finish(ys_ref))


def _out_proj(y_prompt, y_sample, w, layer, g, g_ffn, h):
    return pl.pallas_call(
        _out_proj_kernel,
        grid=(N_TOK // TM_OUT,),
        in_specs=[
            pl.BlockSpec((TM_OUT, D_MODEL), lambda i: (jnp.minimum(i, N_PROMPT_TILES - 1), 0)),
            pl.BlockSpec((TM_OUT, D_MODEL), lambda i: (0, 0)),
            pl.BlockSpec((None, D_MODEL, D_MODEL), lambda i: (layer, 0, 0)),
            pl.BlockSpec((1, D_MODEL), lambda i: (0, 0)),
            pl.BlockSpec((1, D_MODEL), lambda i: (0, 0)),
            pl.BlockSpec((TM_OUT, D_MODEL), lambda i: (i, 0)),
        ],
        out_specs=[
            pl.BlockSpec((TM_OUT, D_MODEL), lambda i: (i, 0)),
            pl.BlockSpec((TM_OUT, D_MODEL), lambda i: (i, 0)),
        ],
        out_shape=[
            jax.ShapeDtypeStruct((N_TOK, D_MODEL), F32),
            jax.ShapeDtypeStruct((N_TOK, D_MODEL), BF16),
        ],
        compiler_params=_params("parallel"),
        name="out_proj_norm_residual",
    )(y_prompt, y_sample, w, g.reshape(1, D_MODEL), g_ffn.reshape(1, D_MODEL), h)


def _ffn_accumulate(f_ref, wu_ref, wd_ref, acc_ref):
    k = pl.program_id(1)

    @pl.when(k == 0)
    def _():
        acc_ref[...] = jnp.zeros_like(acc_ref)

    u = jnp.square(jnp.maximum(_dot(f_ref[...], wu_ref[...]), 0.0)).astype(BF16)
    acc_ref[...] += _dot(u, wd_ref[...])
    return k == pl.num_programs(1) - 1


def _ffn_kernel(f_ref, h_ref, wu_ref, wd_ref, g2_ref, gn_ref, o_ref, a_ref, acc_ref):
    is_last = _ffn_accumulate(f_ref, wu_ref, wd_ref, acc_ref)

    @pl.when(is_last)
    def _():
        for r in range(0, TM_OUT, SLAB):
            rows = slice(r, r + SLAB)
            hn = h_ref[rows, :] + _rms_scale(acc_ref[rows, :], g2_ref[...])
            o_ref[rows, :] = hn
            a_ref[rows, :] = _rms_scale(hn, gn_ref[...]).astype(BF16)


def _ffn_final_kernel(f_ref, h_ref, wu_ref, wd_ref, g2_ref, yp_ref, ys_ref, acc_ref):
    is_last = _ffn_accumulate(f_ref, wu_ref, wd_ref, acc_ref)

    def finish(o_ref):
        for r in range(0, TM_OUT, SLAB):
            rows = slice(r, r + SLAB)
            o_ref[rows, :] = h_ref[rows, :] + _rms_scale(acc_ref[rows, :], g2_ref[...])

    pl.when(jnp.logical_and(is_last, _is_prompt_tile()))(lambda: finish(yp_ref))
    pl.when(jnp.logical_and(is_last, jnp.logical_not(_is_prompt_tile())))(lambda: finish(ys_ref))


def _ffn_in_specs(layer, n_gains):
    return [
        pl.BlockSpec((TM_OUT, D_MODEL), lambda i, k: (i, 0)),
        pl.BlockSpec((TM_OUT, D_MODEL), lambda i, k: (i, 0)),
        pl.BlockSpec((None, D_MODEL, TF), lambda i, k: (layer, 0, k)),
        pl.BlockSpec((None, TF, D_MODEL), lambda i, k: (layer, k, 0)),
    ] + [pl.BlockSpec((1, D_MODEL), lambda i, k: (0, 0))] * n_gains


def _ffn(f, h, w_up, w_down, layer, g2, g_next):
    return pl.pallas_call(
        _ffn_kernel,
        grid=(N_TOK // TM_OUT, D_FF // TF),
        in_specs=_ffn_in_specs(layer, 2),
        out_specs=[
            pl.BlockSpec((TM_OUT, D_MODEL), lambda i, k: (i, 0)),
            pl.BlockSpec((TM_OUT, D_MODEL), lambda i, k: (i, 0)),
        ],
        out_shape=[
            jax.ShapeDtypeStruct((N_TOK, D_MODEL), F32),
            jax.ShapeDtypeStruct((N_TOK, D_MODEL), BF16),
        ],
        scratch_shapes=[pltpu.VMEM((TM_OUT, D_MODEL), F32)],
        compiler_params=_params("parallel", "arbitrary"),
        name="relu2_mlp",
    )(f, h, w_up, w_down, g2.reshape(1, D_MODEL), g_next.reshape(1, D_MODEL))


def _ffn_final(f, h, w_up, w_down, layer, g2):
    return pl.pallas_call(
        _ffn_final_kernel,
        grid=(N_TOK // TM_OUT, D_FF // TF),
        in_specs=_ffn_in_specs(layer, 1),
        out_specs=[
            pl.BlockSpec((TM_OUT, D_MODEL), lambda i, k: (jnp.minimum(i, N_PROMPT_TILES - 1), 0)),
            pl.BlockSpec((TM_OUT, D_MODEL), lambda i, k: (0, 0)),
        ],
        out_shape=[
            jax.ShapeDtypeStruct((N_PROMPT, D_MODEL), F32),
            jax.ShapeDtypeStruct((N_SAMPLE, D_MODEL), F32),
        ],
        scratch_shapes=[pltpu.VMEM((TM_OUT, D_MODEL), F32)],
        compiler_params=_params("arbitrary", "arbitrary"),
        name="relu2_mlp_final",
    )(f, h, w_up, w_down, g2.reshape(1, D_MODEL))


def _rotary_tables(pos):
    half = RET_DK // 2
    inv = ROPE_BASE ** (-np.arange(half, dtype=np.float64) / half)
    ang = np.asarray(pos, np.float64)[:, None] * inv[None, :]
    cos = np.concatenate([np.cos(ang), np.cos(ang)], axis=-1)
    sin = np.concatenate([-np.sin(ang), np.sin(ang)], axis=-1)
    return jnp.asarray(cos, F32), jnp.asarray(sin, F32)


def _log_gamma():
    return np.log1p(-np.exp2(-5.0 - np.arange(RET_HEADS, dtype=np.float64)))


def _retention_tables(length):
    log_g = _log_gamma()
    idx = np.arange(length, dtype=np.float64)
    diff = idx[:, None] - idx[None, :]
    mask = np.where(diff[None] >= 0, np.exp(log_g[:, None, None] * np.maximum(diff, 0.0)[None]), 0.0)
    qd = np.exp(log_g[:, None] * (idx + 1.0))
    kd = np.exp(log_g[:, None] * (length - 1.0 - idx))
    cd = np.exp(log_g * length)
    return mask, qd, kd, cd


def _lane_bcast(a, width):
    return np.repeat(a[..., None], width, axis=-1)


N_PROMPT_CHUNKS = N_PROMPT // CHUNK
PROJ_COLS = 512


def _projected_chunk(step):
    return jnp.minimum(step, N_PROMPT_CHUNKS - 1)


def _mixed_chunk(step):
    return jnp.maximum(step - 1, 0)


def _interleave(first, second):
    i = j = 0
    while i < len(first) or j < len(second):
        if j >= len(second) or (i < len(first) and i * len(second) <= j * len(first)):
            first[i]()
            i += 1
        else:
            second[j]()
            j += 1


def _skewed(a_ref, w_ref, z_refs, n_cols, mix):
    step = pl.program_id(0)
    n = (step + N_CHUNKS - 1) % N_CHUNKS

    @pl.when(step == 0)
    def _():
        z_refs[1][...] = jnp.zeros_like(z_refs[1])

    for parity in range(2):
        z_next, z_mixed = z_refs[parity], z_refs[1 - parity]

        def body(z_next=z_next, z_mixed=z_mixed):
            a = a_ref[...]

            def project(c):
                z_next[:, c:c + PROJ_COLS] = _dot(a, w_ref[:, c:c + PROJ_COLS])

            mix(step, n, z_mixed, [functools.partial(project, c) for c in range(0, n_cols, PROJ_COLS)])

        pl.when(step % 2 == parity)(body)


def _retention_head_out(o, gate):
    o = o * lax.rsqrt(jnp.mean(o * o, axis=-1, keepdims=True) + EPS)
    return (gate * jax.nn.sigmoid(gate) * o).astype(BF16)


def _rotate(x, cos, sin):
    return x * cos + pltpu.roll(x, RET_DK // 2, 1) * sin


def _even_prompt_kernel(a_ref, w_ref, cos_ref, sin_ref, mask_ref, qd_ref, kd_ref, wp_ref, sp_ref,
                        y_ref, hist_ref, st_ref, z0_ref, z1_ref, pext_ref, s_ref, *, chunk_decay):
    step = pl.program_id(0)
    n = (step + N_CHUNKS - 1) % N_CHUNKS

    @pl.when(jnp.logical_or(n == 0, step == 0))
    def _():
        pext_ref[0:POOL_PAD, :] = jnp.zeros((POOL_PAD, POOL_WIDTH), F32)
        s_ref[...] = jnp.zeros_like(s_ref)

    _skewed(a_ref, w_ref, (z0_ref, z1_ref), EVEN_IN,
            functools.partial(_even_prompt_mix, cos_ref, sin_ref, mask_ref, qd_ref, kd_ref, wp_ref, sp_ref, y_ref,
                              pext_ref, s_ref, chunk_decay))

    @pl.when(jnp.logical_and(n == N_CHUNKS - 1, step > 0))
    def _():
        hist_ref[0] = pext_ref[POOL_PAD + CHUNK - POOL_HIST:POOL_PAD + CHUNK, :]
        st_ref[0] = s_ref[...]


def _even_prompt_mix(cos_ref, sin_ref, mask_ref, qd_ref, kd_ref, wp_ref, sp_ref, y_ref, pext_ref, s_ref,
                     chunk_decay, step, n, zc, project):
    def pool():
        pext_ref[POOL_PAD:POOL_PAD + CHUNK, :] = zc[:, 0:POOL_WIDTH]
        pos = n * CHUNK + lax.broadcasted_iota(jnp.int32, (CHUNK, 1), 0)
        for g, w in enumerate(POOL_WINDOWS):
            lanes = slice(g * POOL_GROUP_DIM, (g + 1) * POOL_GROUP_DIM)
            p = pext_ref[POOL_PAD:POOL_PAD + CHUNK, lanes]
            acc = p
            for i in range(1, w):
                acc = acc + pext_ref[POOL_PAD - i:POOL_PAD - i + CHUNK, lanes]
            cnt = jnp.minimum(w, pos + 1).astype(F32)
            d = acc / cnt - p
            yg = _dot(d.astype(BF16), wp_ref[g].astype(BF16)) * sp_ref[:, lanes]
            y_ref[:, lanes] = yg.astype(BF16)
        pext_ref[0:POOL_PAD, :] = pext_ref[CHUNK:CHUNK + POOL_PAD, :]

    live = {}

    def scores_stage(h):
        cos = cos_ref[...]
        sin = sin_ref[...]
        q = _rotate(zc[:, Q_OFF + h * RET_DK:Q_OFF + (h + 1) * RET_DK], cos, sin)
        k = _rotate(zc[:, K_OFF + h * RET_DK:K_OFF + (h + 1) * RET_DK], cos, sin) * (RET_DK ** -0.5)
        v = zc[:, V_OFF + h * RET_DV:V_OFF + (h + 1) * RET_DV].astype(BF16)
        live[h] = (_dot_nt(q.astype(BF16), k.astype(BF16)), (q * qd_ref[h]).astype(BF16),
                   (k * kd_ref[h]).astype(BF16), v)

    def output_stage(h):
        scores, q_dec, k_dec, v = live[h]
        s = s_ref[h]
        o = _dot((scores * mask_ref[h]).astype(BF16), v) + _dot(q_dec, s.astype(BF16))
        s_ref[h] = s * chunk_decay[h] + _dot_tn(k_dec, v)
        live[h] = o

    def norm_stage(h):
        gate = zc[:, G_OFF + h * RET_DV:G_OFF + (h + 1) * RET_DV]
        y_ref[:, POOL_WIDTH + h * RET_DV:POOL_WIDTH + (h + 1) * RET_DV] = _retention_head_out(live.pop(h), gate)

    def heads_step(i):
        for stage, h in ((norm_stage, i - 2), (output_stage, i - 1), (scores_stage, i)):
            if 0 <= h < RET_HEADS:
                stage(h)

    _interleave(project, [pool] + [functools.partial(heads_step, i) for i in range(RET_HEADS + 2)])


def _even_prompt(a, w_in, w_pool, s_pool, layer, prev_hist, prev_state):
    cos, sin = _rotary_tables(np.arange(SEQ))
    mask, qd, kd, cd = _retention_tables(CHUNK)
    const = lambda *shape: pl.BlockSpec(shape, lambda t: (0,) * len(shape))
    return _call_stacked(
        functools.partial(_even_prompt_kernel, chunk_decay=tuple(float(c) for c in cd)),
        name="even_mixer_prompt",
        grid=(N_PROMPT_CHUNKS + 1,),
        in_specs=[
            pl.BlockSpec((CHUNK, D_MODEL), lambda t: (_projected_chunk(t), 0)),
            pl.BlockSpec((None, D_MODEL, EVEN_IN), lambda t: (layer, 0, 0), pipeline_mode=pl.Buffered(1)),
            pl.BlockSpec((CHUNK, RET_DK), lambda t: (_mixed_chunk(t) % N_CHUNKS, 0)),
            pl.BlockSpec((CHUNK, RET_DK), lambda t: (_mixed_chunk(t) % N_CHUNKS, 0)),
            const(RET_HEADS, CHUNK, CHUNK),
            const(RET_HEADS, CHUNK, RET_DK),
            const(RET_HEADS, CHUNK, RET_DK),
            pl.BlockSpec((None, POOL_GROUPS, POOL_GROUP_DIM, POOL_GROUP_DIM), lambda t: (layer, 0, 0, 0)),
            pl.BlockSpec((None, 1, POOL_WIDTH), lambda t: (layer, 0, 0)),
        ],
        args=(a, w_in, cos, sin, jnp.asarray(mask, F32), jnp.asarray(_lane_bcast(qd, RET_DK), F32),
              jnp.asarray(_lane_bcast(kd, RET_DK), F32), w_pool, s_pool.reshape(N_EVEN, 1, POOL_WIDTH)),
        out_specs=[
            pl.BlockSpec((CHUNK, D_MODEL), lambda t: (_mixed_chunk(t), 0)),
            pl.BlockSpec((None, 1, POOL_HIST, POOL_WIDTH), lambda t: (layer, _mixed_chunk(t) // N_CHUNKS, 0, 0)),
            pl.BlockSpec((None, 1, RET_HEADS, RET_DK, RET_DV),
                         lambda t: (layer, _mixed_chunk(t) // N_CHUNKS, 0, 0, 0)),
        ],
        out_shape=[
            jax.ShapeDtypeStruct((N_PROMPT, D_MODEL), BF16),
            jax.ShapeDtypeStruct((N_EVEN, BATCH, POOL_HIST, POOL_WIDTH), F32),
            jax.ShapeDtypeStruct((N_EVEN, BATCH, RET_HEADS, RET_DK, RET_DV), F32),
        ],
        stacked={1: prev_hist, 2: prev_state},
        sem=("arbitrary",),
        scratch_shapes=[
            pltpu.VMEM((CHUNK, EVEN_IN), F32),
            pltpu.VMEM((CHUNK, EVEN_IN), F32),
            pltpu.VMEM((POOL_PAD + CHUNK, POOL_WIDTH), F32),
            pltpu.VMEM((RET_HEADS, RET_DK, RET_DV), F32),
        ],
    )


def _even_sample_kernel(z_ref, cos_ref, sin_ref, mask_ref, qd_ref, kd_ref, wp_ref, sp_ref, hist_ref, st_ref,
                        y_ref, nhist_ref, nst_ref, pext_ref, d_ref, *, chunk_decay):
    t_idx = lax.broadcasted_iota(jnp.int32, (DEC_SEQ, 1), 0)
    for b in range(SAMPLE_BB):
        rows = slice(b * DEC_SEQ, (b + 1) * DEC_SEQ)
        pext_ref[0:POOL_HIST, :] = hist_ref[b]
        pext_ref[POOL_HIST:POOL_HIST + DEC_SEQ, :] = z_ref[rows, 0:POOL_WIDTH]
        for g, w in enumerate(POOL_WINDOWS):
            lanes = slice(g * POOL_GROUP_DIM, (g + 1) * POOL_GROUP_DIM)
            p = pext_ref[POOL_HIST:POOL_HIST + DEC_SEQ, lanes]
            acc = p
            for i in range(1, w):
                acc = acc + pext_ref[POOL_HIST - i:POOL_HIST - i + DEC_SEQ, lanes]
            cnt = jnp.minimum(w, PAST_LEN + t_idx + 1).astype(F32)
            d_ref[rows, lanes] = acc / cnt - p
        nhist_ref[b] = pext_ref[DEC_SEQ:DEC_SEQ + POOL_HIST, :]
    for g in range(POOL_GROUPS):
        lanes = slice(g * POOL_GROUP_DIM, (g + 1) * POOL_GROUP_DIM)
        yg = _dot(d_ref[:, lanes].astype(BF16), wp_ref[g].astype(BF16)) * sp_ref[:, lanes]
        y_ref[:, lanes] = yg.astype(BF16)

    cos = cos_ref[...]
    sin = sin_ref[...]
    seq_of_row = lax.broadcasted_iota(jnp.int32, (SAMPLE_ROWS, 1), 0) // DEC_SEQ
    for h in range(RET_HEADS):
        q = _rotate(z_ref[:, Q_OFF + h * RET_DK:Q_OFF + (h + 1) * RET_DK], cos, sin)
        k = _rotate(z_ref[:, K_OFF + h * RET_DK:K_OFF + (h + 1) * RET_DK], cos, sin) * (RET_DK ** -0.5)
        v = z_ref[:, V_OFF + h * RET_DV:V_OFF + (h + 1) * RET_DV].astype(BF16)
        gate = z_ref[:, G_OFF + h * RET_DV:G_OFF + (h + 1) * RET_DV]
        scores = _dot_nt(q.astype(BF16), k.astype(BF16)) * mask_ref[h]
        o = _dot(scores.astype(BF16), v)
        q_dec = (q * qd_ref[h]).astype(BF16)
        k_dec = k * kd_ref[h]
        for b in range(SAMPLE_BB):
            own = seq_of_row == b
            s = st_ref[b, h]
            o = o + jnp.where(own, _dot(q_dec, s.astype(BF16)), 0.0)
            nst_ref[b, h] = s * chunk_decay[h] + _dot_tn(jnp.where(own, k_dec, 0.0).astype(BF16), v)
        y_ref[:, POOL_WIDTH + h * RET_DV:POOL_WIDTH + (h + 1) * RET_DV] = _retention_head_out(o, gate)


def _even_sample(z, w_pool, s_pool, state_pool, state_ret, layer, prev_hist, prev_state):
    pos = PAST_LEN + np.arange(DEC_SEQ)
    cos, sin = _rotary_tables(np.tile(pos, SAMPLE_BB))
    mask, qd, kd, cd = _retention_tables(DEC_SEQ)
    mask = np.stack([np.kron(np.eye(SAMPLE_BB), m) for m in mask])
    qd = _lane_bcast(np.tile(qd, (1, SAMPLE_BB)), RET_DK)
    kd = _lane_bcast(np.tile(kd, (1, SAMPLE_BB)), RET_DK)
    const = lambda *shape: pl.BlockSpec(shape, lambda i: (0,) * len(shape))
    return _call_stacked(
        functools.partial(_even_sample_kernel, chunk_decay=tuple(float(c) for c in cd)),
        name="even_mixer_sample",
        grid=(DEC_BATCH // SAMPLE_BB,),
        in_specs=[
            pl.BlockSpec((SAMPLE_ROWS, EVEN_IN), lambda i: (i, 0)),
            const(SAMPLE_ROWS, RET_DK),
            const(SAMPLE_ROWS, RET_DK),
            const(RET_HEADS, SAMPLE_ROWS, SAMPLE_ROWS),
            const(RET_HEADS, SAMPLE_ROWS, RET_DK),
            const(RET_HEADS, SAMPLE_ROWS, RET_DK),
            pl.BlockSpec((None, POOL_GROUPS, POOL_GROUP_DIM, POOL_GROUP_DIM), lambda i: (layer, 0, 0, 0)),
            pl.BlockSpec((None, 1, POOL_WIDTH), lambda i: (layer, 0, 0)),
            pl.BlockSpec((None, SAMPLE_BB, POOL_HIST, POOL_WIDTH), lambda i: (layer, i, 0, 0)),
            pl.BlockSpec((None, SAMPLE_BB, RET_HEADS, RET_DK, RET_DV), lambda i: (layer, i, 0, 0, 0)),
        ],
        args=(z, cos, sin, jnp.asarray(mask, F32), jnp.asarray(qd, F32), jnp.asarray(kd, F32),
              w_pool, s_pool.reshape(N_EVEN, 1, POOL_WIDTH), state_pool, state_ret),
        out_specs=[
            pl.BlockSpec((SAMPLE_ROWS, D_MODEL), lambda i: (i, 0)),
            pl.BlockSpec((None, SAMPLE_BB, POOL_HIST, POOL_WIDTH), lambda i: (layer, i, 0, 0)),
            pl.BlockSpec((None, SAMPLE_BB, RET_HEADS, RET_DK, RET_DV), lambda i: (layer, i, 0, 0, 0)),
        ],
        out_shape=[
            jax.ShapeDtypeStruct((N_SAMPLE, D_MODEL), BF16),
            jax.ShapeDtypeStruct((N_EVEN, DEC_BATCH, POOL_HIST, POOL_WIDTH), F32),
            jax.ShapeDtypeStruct((N_EVEN, DEC_BATCH, RET_HEADS, RET_DK, RET_DV), F32),
        ],
        stacked={1: prev_hist, 2: prev_state},
        sem=("parallel",),
        scratch_shapes=[
            pltpu.VMEM((POOL_HIST + DEC_SEQ, POOL_WIDTH), F32),
            pltpu.VMEM((SAMPLE_ROWS, POOL_WIDTH), F32),
        ],
    )


CONV_LANES = 128


def _depthwise_conv(ext_ref, first_row, n_rows, dw_ref, dwb_ref, out_ref, out_row):
    for c in range(0, CONV_CH, CONV_LANES):
        lanes = slice(c, c + CONV_LANES)
        acc = jnp.broadcast_to(dwb_ref[:, lanes], (n_rows, CONV_LANES))
        for j in range(CONV_K):
            acc = acc + ext_ref[first_row + j:first_row + j + n_rows, lanes] * dw_ref[j:j + 1, lanes]
        out_ref[out_row:out_row + n_rows, lanes] = acc


SUBLANES = 8
SHIFTED_ROWS = CONV_PAD + CHUNK - SUBLANES


def _odd_prompt_kernel(a_ref, w_ref, lng_ref, lnb_ref, sgw_ref, sgb_ref, dw_ref, dwb_ref, cvg_ref, cvb_ref,
                       y_ref, cst_ref, z0_ref, z1_ref, vb_ref, ext_ref, cv_ref, xs_ref):
    step = pl.program_id(0)
    n = (step + N_CHUNKS - 1) % N_CHUNKS

    @pl.when(jnp.logical_or(n == 0, step == 0))
    def _():
        ext_ref[0:CONV_PAD, :] = jnp.zeros((CONV_PAD, CONV_CH), F32)

    _skewed(a_ref, w_ref, (z0_ref, z1_ref), ODD_IN,
            functools.partial(_odd_prompt_mix, lng_ref, lnb_ref, sgw_ref, sgb_ref, dw_ref, dwb_ref, cvg_ref, cvb_ref,
                              y_ref, vb_ref, ext_ref, cv_ref, xs_ref))

    @pl.when(jnp.logical_and(n == N_CHUNKS - 1, step > 0))
    def _():
        cst_ref[0] = ext_ref[CONV_PAD + CHUNK - CONV_HIST:CONV_PAD + CHUNK, :]

    ext_ref[0:CONV_PAD, :] = ext_ref[CHUNK:CHUNK + CONV_PAD, :]


def _odd_prompt_mix(lng_ref, lnb_ref, sgw_ref, sgb_ref, dw_ref, dwb_ref, cvg_ref, cvb_ref,
                    y_ref, vb_ref, ext_ref, cv_ref, xs_ref, step, n, zc, project):
    def gate_values():
        v = _layer_norm(jax.nn.gelu(zc[:, SG_WIDTH:2 * SG_WIDTH]), lng_ref[...], lnb_ref[...])
        vb_ref[...] = v.astype(BF16)

    def gating(g):
        lanes = slice(g * SG_GROUP_DIM, (g + 1) * SG_GROUP_DIM)
        row = lax.broadcasted_iota(jnp.int32, (CHUNK, CHUNK), 0)
        col = lax.broadcasted_iota(jnp.int32, (CHUNK, CHUNK), 1)
        ws = jnp.where(col <= row, sgw_ref[g], 0.0).astype(BF16)
        mixed = _dot(ws, vb_ref[:, lanes]) + sgb_ref[:, g:g + 1]
        y_ref[:, lanes] = (jax.nn.gelu(zc[:, lanes]) * mixed).astype(BF16)

    def glu():
        a = zc[:, 2 * SG_WIDTH:2 * SG_WIDTH + CONV_CH]
        gate = zc[:, 2 * SG_WIDTH + CONV_CH:2 * SG_WIDTH + 2 * CONV_CH]
        ext_ref[CONV_PAD:CONV_PAD + CHUNK, :] = a * jax.nn.sigmoid(gate)

    def shifted_copy(s):
        xs_ref[s - 1] = ext_ref[s:s + SHIFTED_ROWS, :]

    def conv(c):
        lanes = slice(c, c + CONV_LANES)
        acc = jnp.broadcast_to(dwb_ref[:, lanes], (CHUNK, CONV_LANES))
        for j in range(CONV_K):
            tile, s = divmod(CONV_PAD - CONV_HIST + j, SUBLANES)
            rows = slice(tile * SUBLANES, tile * SUBLANES + CHUNK)
            window = ext_ref[rows, lanes] if s == 0 else xs_ref[s - 1, rows, lanes]
            acc = acc + window * dw_ref[j:j + 1, lanes]
        cv_ref[:, lanes] = acc

    def conv_out():
        yd = _layer_norm(cv_ref[...], cvg_ref[...], cvb_ref[...])
        y_ref[:, SG_WIDTH:SG_WIDTH + CONV_CH] = (yd * jax.nn.sigmoid(yd)).astype(BF16)

    _interleave(project, [gate_values, glu] + [functools.partial(shifted_copy, s) for s in range(1, SUBLANES)]
                + [functools.partial(conv, c) for c in range(0, CONV_CH, CONV_LANES)] + [conv_out])
    for g in range(SG_GROUPS):
        gating(g)


def _odd_weight_specs(layer):
    per_layer = lambda *shape: pl.BlockSpec((None,) + shape, lambda *_: (layer,) + (0,) * len(shape))
    return dict(
        ln=per_layer(1, SG_WIDTH),
        sgw=per_layer(SG_GROUPS, SG_CHUNK, SG_CHUNK),
        sgb=per_layer(SG_CHUNK, SG_GROUPS),
        dw=per_layer(CONV_K, CONV_CH),
        ch=per_layer(1, CONV_CH),
    )


def _odd_prompt(a, w_in, sg_ln_g, sg_ln_b, sg_w, sg_b, dw_w, dw_b, cv_ln_g, cv_ln_b, layer, prev_cst):
    spec = _odd_weight_specs(layer)
    row = lambda x: x.reshape(N_ODD, 1, -1)
    return _call_stacked(
        _odd_prompt_kernel,
        name="odd_mixer_prompt",
        grid=(N_PROMPT_CHUNKS + 1,),
        in_specs=[
            pl.BlockSpec((CHUNK, D_MODEL), lambda t: (_projected_chunk(t), 0)),
            pl.BlockSpec((None, D_MODEL, ODD_IN), lambda t: (layer, 0, 0), pipeline_mode=pl.Buffered(1)),
            spec["ln"], spec["ln"], spec["sgw"], spec["sgb"], spec["dw"], spec["ch"], spec["ch"], spec["ch"],
        ],
        args=(a, w_in, row(sg_ln_g), row(sg_ln_b), sg_w, jnp.swapaxes(sg_b, 1, 2), dw_w, row(dw_b),
              row(cv_ln_g), row(cv_ln_b)),
        out_specs=[
            pl.BlockSpec((CHUNK, D_MODEL), lambda t: (_mixed_chunk(t), 0)),
            pl.BlockSpec((None, 1, CONV_HIST, CONV_CH), lambda t: (layer, _mixed_chunk(t) // N_CHUNKS, 0, 0)),
        ],
        out_shape=[
            jax.ShapeDtypeStruct((N_PROMPT, D_MODEL), BF16),
            jax.ShapeDtypeStruct((N_ODD, BATCH, CONV_HIST, CONV_CH), F32),
        ],
        stacked={1: prev_cst},
        sem=("arbitrary",),
        scratch_shapes=[
            pltpu.VMEM((CHUNK, ODD_IN), F32),
            pltpu.VMEM((CHUNK, ODD_IN), F32),
            pltpu.VMEM((CHUNK, SG_WIDTH), BF16),
            pltpu.VMEM((CONV_PAD + CHUNK, CONV_CH), F32),
            pltpu.VMEM((CHUNK, CONV_CH), F32),
            pltpu.VMEM((SUBLANES - 1, SHIFTED_ROWS, CONV_CH), F32),
        ],
    )


SG_SHIFT_PAD = 8


def _odd_sample_kernel(sgw_ref, sgb_ref, z_ref, lng_ref, lnb_ref, dw_ref, dwb_ref, cvg_ref, cvb_ref, cst_ref,
                       y_ref, sgv_ref, ncst_ref, vs_ref, ext_ref, cv_ref):
    u = jax.nn.gelu(z_ref[:, 0:SG_WIDTH])
    v = _layer_norm(jax.nn.gelu(z_ref[:, SG_WIDTH:2 * SG_WIDTH]), lng_ref[...], lnb_ref[...])
    sgv_ref[...] = v
    vs_ref[0:SG_SHIFT_PAD, :] = jnp.zeros((SG_SHIFT_PAD, SG_WIDTH), F32)
    vs_ref[SG_SHIFT_PAD:SG_SHIFT_PAD + SAMPLE_ROWS, :] = v
    t_of_row = lax.broadcasted_iota(jnp.int32, (SAMPLE_ROWS, 1), 0) % DEC_SEQ
    for g in range(SG_GROUPS):
        lanes = slice(g * SG_GROUP_DIM, (g + 1) * SG_GROUP_DIM)
        mixed = jnp.zeros((SAMPLE_ROWS, SG_GROUP_DIM), F32)
        for back in range(DEC_SEQ):
            coef = jnp.zeros((SAMPLE_ROWS, 1), F32)
            for t in range(back, DEC_SEQ):
                coef = jnp.where(t_of_row == t, sgw_ref[(g * DEC_SEQ + t) * DEC_SEQ + t - back], coef)
            shifted = vs_ref[SG_SHIFT_PAD - back:SG_SHIFT_PAD - back + SAMPLE_ROWS, lanes]
            mixed = mixed + coef * shifted
        bias = jnp.zeros((SAMPLE_ROWS, 1), F32)
        for t in range(DEC_SEQ):
            bias = jnp.where(t_of_row == t, sgb_ref[g * DEC_SEQ + t], bias)
        y_ref[:, lanes] = (u[:, lanes] * (mixed + bias)).astype(BF16)

    a = z_ref[:, 2 * SG_WIDTH:2 * SG_WIDTH + CONV_CH]
    gate = z_ref[:, 2 * SG_WIDTH + CONV_CH:2 * SG_WIDTH + 2 * CONV_CH]
    glu = a * jax.nn.sigmoid(gate)
    for b in range(SAMPLE_BB):
        rows = slice(b * DEC_SEQ, (b + 1) * DEC_SEQ)
        ext_ref[0:CONV_HIST, :] = cst_ref[b]
        ext_ref[CONV_HIST:CONV_HIST + DEC_SEQ, :] = glu[rows, :]
        _depthwise_conv(ext_ref, 0, DEC_SEQ, dw_ref, dwb_ref, cv_ref, b * DEC_SEQ)
        ncst_ref[b] = ext_ref[DEC_SEQ:DEC_SEQ + CONV_HIST, :]
    yd = _layer_norm(cv_ref[...], cvg_ref[...], cvb_ref[...])
    y_ref[:, SG_WIDTH:SG_WIDTH + CONV_CH] = (yd * jax.nn.sigmoid(yd)).astype(BF16)


def _odd_sample(z, sg_ln_g, sg_ln_b, sg_w, sg_b, dw_w, dw_b, cv_ln_g, cv_ln_b, state_conv, layer,
                prev_sgv, prev_cst):
    spec = _odd_weight_specs(layer)
    row = lambda a: a.reshape(N_ODD, 1, -1)
    smem = pl.BlockSpec(memory_space=pltpu.SMEM)
    return _call_stacked(
        _odd_sample_kernel,
        name="odd_mixer_sample",
        grid=(DEC_BATCH // SAMPLE_BB,),
        in_specs=[
            smem,
            smem,
            pl.BlockSpec((SAMPLE_ROWS, ODD_IN), lambda i: (i, 0)),
            spec["ln"], spec["ln"], spec["dw"], spec["ch"], spec["ch"], spec["ch"],
            pl.BlockSpec((None, SAMPLE_BB, CONV_HIST, CONV_CH), lambda i: (layer, i, 0, 0)),
        ],
        args=(sg_w[layer, :, :DEC_SEQ, :DEC_SEQ].reshape(-1), sg_b[layer, :, :DEC_SEQ].reshape(-1), z,
              row(sg_ln_g), row(sg_ln_b), dw_w, row(dw_b), row(cv_ln_g), row(cv_ln_b), state_conv),
        out_specs=[
            pl.BlockSpec((SAMPLE_ROWS, D_MODEL), lambda i: (i, 0)),
            pl.BlockSpec((None, SAMPLE_ROWS, SG_WIDTH), lambda i: (layer, i, 0)),
            pl.BlockSpec((None, SAMPLE_BB, CONV_HIST, CONV_CH), lambda i: (layer, i, 0, 0)),
        ],
        out_shape=[
            jax.ShapeDtypeStruct((N_SAMPLE, D_MODEL), BF16),
            jax.ShapeDtypeStruct((N_ODD, N_SAMPLE, SG_WIDTH), F32),
            jax.ShapeDtypeStruct((N_ODD, DEC_BATCH, CONV_HIST, CONV_CH), F32),
        ],
        stacked={1: prev_sgv, 2: prev_cst},
        sem=("parallel",),
        scratch_shapes=[
            pltpu.VMEM((SG_SHIFT_PAD + SAMPLE_ROWS, SG_WIDTH), F32),
            pltpu.VMEM((CONV_HIST + DEC_SEQ, CONV_CH), F32),
            pltpu.VMEM((SAMPLE_ROWS, CONV_CH), F32),
        ],
    )


def kernel(x_prompt, x_sample, state_pool, state_ret, state_conv, norm_mix_pre, norm_mix_post, norm_ffn_pre, norm_ffn_post, w_in_even, w_pool, s_pool, w_out_even, w_in_odd, sg_ln_g, sg_ln_b, sg_w, sg_b, dw_w, dw_b, cv_ln_g, cv_ln_b, w_out_odd, w_up, w_down):
    w_in_even, w_out_even, w_in_odd, w_out_odd, w_up, w_down = (
        w.astype(BF16) for w in (w_in_even, w_out_even, w_in_odd, w_out_odd, w_up, w_down))

    h, a = _prenorm(x_prompt, x_sample, norm_mix_pre[0])
    pool_p = pool_s = ret_p = ret_s = conv_p = conv_s = sgv_s = None
    for l in range(DEPTH):
        i = l // 2
        if l % 2 == 0:
            y_p, pool_p, ret_p = _even_prompt(a, w_in_even, w_pool, s_pool, i, pool_p, ret_p)
            z_s = _in_proj_sample(a, w_in_even, i)
            y_s, pool_s, ret_s = _even_sample(z_s, w_pool, s_pool, state_pool, state_ret, i, pool_s, ret_s)
            w_out = w_out_even
        else:
            y_p, conv_p = _odd_prompt(a, w_in_odd, sg_ln_g, sg_ln_b, sg_w, sg_b, dw_w, dw_b, cv_ln_g, cv_ln_b,
                                      i, conv_p)
            z_s = _in_proj_sample(a, w_in_odd, i)
            y_s, sgv_s, conv_s = _odd_sample(z_s, sg_ln_g, sg_ln_b, sg_w, sg_b, dw_w, dw_b, cv_ln_g, cv_ln_b,
                                             state_conv, i, sgv_s, conv_s)
            w_out = w_out_odd
        h, f = _out_proj(y_p, y_s, w_out, i, norm_mix_post[l], norm_ffn_pre[l], h)
        if l + 1 < DEPTH:
            h, a = _ffn(f, h, w_up, w_down, l, norm_ffn_post[l], norm_mix_pre[l + 1])
        else:
            y_prompt, y_sample = _ffn_final(f, h, w_up, w_down, l, norm_ffn_post[l])

    return (y_prompt.reshape(BATCH, SEQ, D_MODEL), y_sample.reshape(DEC_BATCH, DEC_SEQ, D_MODEL),
            pool_p, pool_s, ret_p, ret_s, conv_p, conv_s, sgv_s.reshape(N_ODD, DEC_BATCH, DEC_SEQ, SG_WIDTH))
```

```python
import functools

import jax
import jax.numpy as jnp
import numpy as np
from jax import lax
from jax.experimental import pallas as pl
from jax.experimental.pallas import tpu as pltpu

F32 = jnp.float32
BF16 = jnp.bfloat16

D_MODEL = 2048
BATCH = 4
SEQ = 2048
DEPTH = 4
DEC_BATCH = 128
DEC_SEQ = 4
PAST_LEN = 16384

N_EVEN = (DEPTH + 1) // 2
N_ODD = DEPTH // 2

POOL_WINDOWS = (2, 4, 8, 16)
POOL_GROUPS = len(POOL_WINDOWS)
POOL_WIDTH = D_MODEL // 4
POOL_GROUP_DIM = POOL_WIDTH // POOL_GROUPS
POOL_HIST = max(POOL_WINDOWS) - 1
RET_WIDTH = D_MODEL - POOL_WIDTH
RET_HEADS = 6
RET_DV = RET_WIDTH // RET_HEADS
RET_DK = RET_DV // 2
RET_QK = RET_HEADS * RET_DK
RET_CHUNK = 128
ROPE_BASE = 10000.0
SG_WIDTH = D_MODEL // 2
SG_CHUNK = 128
SG_GROUPS = 4
SG_GROUP_DIM = SG_WIDTH // SG_GROUPS
CONV_CH = D_MODEL // 2
CONV_K = 31
CONV_HIST = CONV_K - 1
D_FF = 4 * D_MODEL
EPS = 1e-6

EVEN_IN = POOL_WIDTH + 2 * RET_QK + 2 * RET_WIDTH
ODD_IN = 2 * SG_WIDTH + 2 * CONV_CH

Q_OFF = POOL_WIDTH
K_OFF = Q_OFF + RET_QK
V_OFF = K_OFF + RET_QK
G_OFF = V_OFF + RET_WIDTH

N_PROMPT = BATCH * SEQ
N_SAMPLE = DEC_BATCH * DEC_SEQ

CHUNK = 128
N_CHUNKS = SEQ // CHUNK
SAMPLE_BB = 4
SAMPLE_ROWS = SAMPLE_BB * DEC_SEQ
POOL_PAD = 16
CONV_PAD = 32

TN_IN = 1024
TM_OUT = 512
SLAB = 128
TF = 1024
VMEM_LIMIT = 56 * 1024 * 1024

assert N_PROMPT % TM_OUT == 0 and N_SAMPLE % TM_OUT == 0 and TM_OUT % SLAB == 0
assert POOL_PAD >= POOL_HIST and CONV_PAD >= CONV_HIST


def _params(*sem):
    return pltpu.CompilerParams(dimension_semantics=sem, vmem_limit_bytes=VMEM_LIMIT)


def _rms_scale(x, g):
    return x * lax.rsqrt(jnp.mean(x * x, axis=-1, keepdims=True) + EPS) * g


def _layer_norm(x, g, b):
    xc = x - jnp.mean(x, axis=-1, keepdims=True)
    return xc * lax.rsqrt(jnp.mean(xc * xc, axis=-1, keepdims=True) + EPS) * g + b


def _dot(a, b):
    return jnp.dot(a, b, preferred_element_type=F32)


def _dot_nt(a, b):
    return lax.dot_general(a, b, (((1,), (1,)), ((), ())), preferred_element_type=F32)


def _dot_tn(a, b):
    return lax.dot_general(a, b, (((0,), (0,)), ((), ())), preferred_element_type=F32)


def _skip_aliased(body, n_in, n_aliased):
    def wrapped(*refs):
        return body(*refs[:n_in], *refs[n_in + n_aliased:])
    return wrapped


def _call_stacked(body, *, name, grid, in_specs, args, out_specs, out_shape, stacked, sem, scratch_shapes):
    prev = [(o, p) for o, p in sorted(stacked.items()) if p is not None]
    n_in = len(args)
    return pl.pallas_call(
        _skip_aliased(body, n_in, len(prev)),
        grid=grid,
        in_specs=list(in_specs) + [pl.BlockSpec(memory_space=pl.ANY)] * len(prev),
        out_specs=out_specs,
        out_shape=out_shape,
        input_output_aliases={n_in + j: o for j, (o, _) in enumerate(prev)},
        scratch_shapes=scratch_shapes,
        compiler_params=_params(*sem),
        name=name,
    )(*args, *[p for _, p in prev])


def _prenorm_kernel(x_ref, g_ref, a_ref):
    a_ref[...] = _rms_scale(x_ref[...], g_ref[...]).astype(BF16)


def _prenorm(x, g):
    rows = x.shape[0]
    return pl.pallas_call(
        _prenorm_kernel,
        grid=(rows // TM_OUT,),
        in_specs=[
            pl.BlockSpec((TM_OUT, D_MODEL), lambda i: (i, 0)),
            pl.BlockSpec((1, D_MODEL), lambda i: (0, 0)),
        ],
        out_specs=pl.BlockSpec((TM_OUT, D_MODEL), lambda i: (i, 0)),
        out_shape=jax.ShapeDtypeStruct((rows, D_MODEL), BF16),
        compiler_params=_params("parallel"),
        name="prenorm",
    )(x, g.reshape(1, D_MODEL))


def _in_proj_kernel(a_ref, w_ref, o_ref):
    o_ref[...] = _dot(a_ref[...], w_ref[...])


def _in_proj_sample(a, w, layer):
    n_out = w.shape[2]
    return pl.pallas_call(
        _in_proj_kernel,
        grid=(n_out // TN_IN,),
        in_specs=[
            pl.BlockSpec((N_SAMPLE, D_MODEL), lambda j: (0, 0)),
            pl.BlockSpec((None, D_MODEL, TN_IN), lambda j: (layer, 0, j)),
        ],
        out_specs=pl.BlockSpec((N_SAMPLE, TN_IN), lambda j: (0, j)),
        out_shape=jax.ShapeDtypeStruct((N_SAMPLE, n_out), F32),
        compiler_params=_params("parallel"),
        name="in_proj_sample",
    )(a, w)


def _residual_norms(res, h_ref, g_ref, gn_ref, o_ref, n_ref):
    hn = h_ref[...] + _rms_scale(res, g_ref[...])
    o_ref[...] = hn
    n_ref[...] = _rms_scale(hn, gn_ref[...]).astype(BF16)


def _out_proj_kernel(y_ref, w_ref, g_ref, gf_ref, h_ref, o_ref, f_ref):
    for r in range(0, TM_OUT, SLAB):
        rows = pl.ds(r, SLAB)
        _residual_norms(_dot(y_ref[rows, :], w_ref[...]), h_ref.at[rows], g_ref, gf_ref, o_ref.at[rows],
                        f_ref.at[rows])


def _out_proj_sample(y, w, layer, g, g_ffn, h):
    tile = lambda: pl.BlockSpec((TM_OUT, D_MODEL), lambda i: (i, 0))
    gain = lambda: pl.BlockSpec((1, D_MODEL), lambda i: (0, 0))
    return pl.pallas_call(
        _out_proj_kernel,
        grid=(N_SAMPLE // TM_OUT,),
        in_specs=[tile(), pl.BlockSpec((None, D_MODEL, D_MODEL), lambda i: (layer, 0, 0)), gain(), gain(), tile()],
        out_specs=[tile(), tile()],
        out_shape=[
            jax.ShapeDtypeStruct((N_SAMPLE, D_MODEL), F32),
            jax.ShapeDtypeStruct((N_SAMPLE, D_MODEL), BF16),
        ],
        compiler_params=_params("parallel"),
        name="out_proj_sample",
    )(y, w, g.reshape(1, D_MODEL), g_ffn.reshape(1, D_MODEL), h)


def _ffn_accumulate(f_ref, wu_ref, wd_ref, acc_ref):
    k = pl.program_id(1)

    @pl.when(k == 0)
    def _():
        acc_ref[...] = jnp.zeros_like(acc_ref)

    u = jnp.square(jnp.maximum(_dot(f_ref[...], wu_ref[...]), 0.0)).astype(BF16)
    acc_ref[...] += _dot(u, wd_ref[...])
    return k == pl.num_programs(1) - 1


def _ffn_kernel(f_ref, h_ref, wu_ref, wd_ref, g2_ref, gn_ref, o_ref, a_ref, acc_ref):
    is_last = _ffn_accumulate(f_ref, wu_ref, wd_ref, acc_ref)

    @pl.when(is_last)
    def _():
        for r in range(0, TM_OUT, SLAB):
            rows = pl.ds(r, SLAB)
            _residual_norms(acc_ref[rows, :], h_ref.at[rows], g2_ref, gn_ref, o_ref.at[rows], a_ref.at[rows])


def _ffn_final_kernel(f_ref, h_ref, wu_ref, wd_ref, g2_ref, o_ref, acc_ref):
    is_last = _ffn_accumulate(f_ref, wu_ref, wd_ref, acc_ref)

    @pl.when(is_last)
    def _():
        for r in range(0, TM_OUT, SLAB):
            rows = pl.ds(r, SLAB)
            o_ref[rows, :] = h_ref[rows, :] + _rms_scale(acc_ref[rows, :], g2_ref[...])


def _ffn_in_specs(layer, n_gains):
    return [
        pl.BlockSpec((TM_OUT, D_MODEL), lambda i, k: (i, 0)),
        pl.BlockSpec((TM_OUT, D_MODEL), lambda i, k: (i, 0)),
        pl.BlockSpec((None, D_MODEL, TF), lambda i, k: (layer, 0, k)),
        pl.BlockSpec((None, TF, D_MODEL), lambda i, k: (layer, k, 0)),
    ] + [pl.BlockSpec((1, D_MODEL), lambda i, k: (0, 0))] * n_gains


def _ffn(f, h, w_up, w_down, layer, g2, g_next):
    rows = f.shape[0]
    return pl.pallas_call(
        _ffn_kernel,
        grid=(rows // TM_OUT, D_FF // TF),
        in_specs=_ffn_in_specs(layer, 2),
        out_specs=[
            pl.BlockSpec((TM_OUT, D_MODEL), lambda i, k: (i, 0)),
            pl.BlockSpec((TM_OUT, D_MODEL), lambda i, k: (i, 0)),
        ],
        out_shape=[
            jax.ShapeDtypeStruct((rows, D_MODEL), F32),
            jax.ShapeDtypeStruct((rows, D_MODEL), BF16),
        ],
        scratch_shapes=[pltpu.VMEM((TM_OUT, D_MODEL), F32)],
        compiler_params=_params("parallel", "arbitrary"),
        name="relu2_mlp",
    )(f, h, w_up, w_down, g2.reshape(1, D_MODEL), g_next.reshape(1, D_MODEL))


def _ffn_final(f, h, w_up, w_down, layer, g2):
    rows = f.shape[0]
    return pl.pallas_call(
        _ffn_final_kernel,
        grid=(rows // TM_OUT, D_FF // TF),
        in_specs=_ffn_in_specs(layer, 1),
        out_specs=pl.BlockSpec((TM_OUT, D_MODEL), lambda i, k: (i, 0)),
        out_shape=jax.ShapeDtypeStruct((rows, D_MODEL), F32),
        scratch_shapes=[pltpu.VMEM((TM_OUT, D_MODEL), F32)],
        compiler_params=_params("parallel", "arbitrary"),
        name="relu2_mlp_final",
    )(f, h, w_up, w_down, g2.reshape(1, D_MODEL))


def _rotary_tables(pos):
    half = RET_DK // 2
    inv = ROPE_BASE ** (-np.arange(half, dtype=np.float64) / half)
    ang = np.asarray(pos, np.float64)[:, None] * inv[None, :]
    cos = np.concatenate([np.cos(ang), np.cos(ang)], axis=-1)
    sin = np.concatenate([-np.sin(ang), np.sin(ang)], axis=-1)
    return jnp.asarray(cos, F32), jnp.asarray(sin, F32)


def _log_gamma():
    return np.log1p(-np.exp2(-5.0 - np.arange(RET_HEADS, dtype=np.float64)))


def _retention_tables(length):
    log_g = _log_gamma()
    idx = np.arange(length, dtype=np.float64)
    diff = idx[:, None] - idx[None, :]
    mask = np.where(diff[None] >= 0, np.exp(log_g[:, None, None] * np.maximum(diff, 0.0)[None]), 0.0)
    qd = np.exp(log_g[:, None] * (idx + 1.0))
    kd = np.exp(log_g[:, None] * (length - 1.0 - idx))
    cd = np.exp(log_g * length)
    return mask, qd, kd, cd


def _lane_bcast(a, width):
    return np.repeat(a[..., None], width, axis=-1)


N_PROMPT_CHUNKS = N_PROMPT // CHUNK
PIPE_STEPS = N_PROMPT_CHUNKS + 2
PROJ_COLS = 512


def _projected_chunk(step):
    return jnp.minimum(step, N_PROMPT_CHUNKS - 1)


def _mixed_chunk(step):
    return jnp.clip(step - 1, 0, N_PROMPT_CHUNKS - 1)


def _finished_chunk(step):
    return jnp.maximum(step - 2, 0)


def _interleave(first, second):
    i = j = 0
    while i < len(first) or j < len(second):
        if j >= len(second) or (i < len(first) and i * len(second) <= j * len(first)):
            first[i]()
            i += 1
        else:
            second[j]()
            j += 1


def _pipeline_step(a_ref, w_ref, z_refs, n_cols, y_refs, wo_ref, r_ref, h_ref, g_ref, gf_ref, o_ref, f_ref, mix):
    step = pl.program_id(0)
    n = (step + N_CHUNKS - 1) % N_CHUNKS

    @pl.when(step == 0)
    def _():
        z_refs[1][...] = jnp.zeros_like(z_refs[1])
        y_refs[0][...] = jnp.zeros_like(y_refs[0])

    for parity in range(2):
        def body(z_next=z_refs[parity], z_mixed=z_refs[1 - parity], y_mixed=y_refs[1 - parity],
                 y_done=y_refs[parity]):
            a = a_ref[...]
            y = y_done[...]

            def project_out(c):
                r_ref[:, c:c + PROJ_COLS] = _dot(y, wo_ref[:, c:c + PROJ_COLS])

            def project_in(c):
                z_next[:, c:c + PROJ_COLS] = _dot(a, w_ref[:, c:c + PROJ_COLS])

            mix(n, z_mixed, y_mixed, [functools.partial(project_out, c) for c in range(0, D_MODEL, PROJ_COLS)]
                + [functools.partial(project_in, c) for c in range(0, n_cols, PROJ_COLS)])
            _residual_norms(r_ref[...], h_ref, g_ref, gf_ref, o_ref, f_ref)

        pl.when(step % 2 == parity)(body)
    return step, n


def _pipeline_specs(layer, n_cols):
    gain = lambda: pl.BlockSpec((1, D_MODEL), lambda t: (0, 0))
    in_specs = [
        pl.BlockSpec((CHUNK, D_MODEL), lambda t: (_projected_chunk(t), 0)),
        pl.BlockSpec((None, D_MODEL, n_cols), lambda t: (layer, 0, 0), pipeline_mode=pl.Buffered(1)),
        pl.BlockSpec((CHUNK, D_MODEL), lambda t: (_finished_chunk(t), 0)),
        pl.BlockSpec((None, D_MODEL, D_MODEL), lambda t: (layer, 0, 0), pipeline_mode=pl.Buffered(1)),
        gain(), gain(),
    ]
    out_specs = [
        pl.BlockSpec((CHUNK, D_MODEL), lambda t: (_finished_chunk(t), 0)),
        pl.BlockSpec((CHUNK, D_MODEL), lambda t: (_finished_chunk(t), 0)),
    ]
    out_shapes = [
        jax.ShapeDtypeStruct((N_PROMPT, D_MODEL), F32),
        jax.ShapeDtypeStruct((N_PROMPT, D_MODEL), BF16),
    ]
    scratch = [
        pltpu.VMEM((CHUNK, n_cols), F32), pltpu.VMEM((CHUNK, n_cols), F32),
        pltpu.VMEM((CHUNK, D_MODEL), BF16), pltpu.VMEM((CHUNK, D_MODEL), BF16),
        pltpu.VMEM((CHUNK, D_MODEL), F32),
    ]
    return in_specs, out_specs, out_shapes, scratch


def _retention_head_out(o, gate):
    o = o * lax.rsqrt(jnp.mean(o * o, axis=-1, keepdims=True) + EPS)
    return (gate * jax.nn.sigmoid(gate) * o).astype(BF16)


def _rotate(x, cos, sin):
    return x * cos + pltpu.roll(x, RET_DK // 2, 1) * sin


def _even_prompt_kernel(a_ref, w_ref, h_ref, wo_ref, g_ref, gf_ref, cos_ref, sin_ref, mask_ref, qd_ref, kd_ref,
                        wp_ref, sp_ref, o_ref, f_ref, hist_ref, st_ref, z0_ref, z1_ref, y0_ref, y1_ref, r_ref,
                        pext_ref, s_ref, *, chunk_decay):
    step = pl.program_id(0)
    n = (step + N_CHUNKS - 1) % N_CHUNKS

    @pl.when(jnp.logical_or(n == 0, step == 0))
    def _():
        pext_ref[0:POOL_PAD, :] = jnp.zeros((POOL_PAD, POOL_WIDTH), F32)
        s_ref[...] = jnp.zeros_like(s_ref)

    _pipeline_step(a_ref, w_ref, (z0_ref, z1_ref), EVEN_IN, (y0_ref, y1_ref), wo_ref, r_ref, h_ref, g_ref, gf_ref,
                   o_ref, f_ref,
                   functools.partial(_even_prompt_mix, cos_ref, sin_ref, mask_ref, qd_ref, kd_ref, wp_ref, sp_ref,
                                     pext_ref, s_ref, chunk_decay))

    @pl.when(jnp.logical_and(n == N_CHUNKS - 1, jnp.logical_and(step > 0, step <= N_PROMPT_CHUNKS)))
    def _():
        hist_ref[0] = pext_ref[POOL_PAD + CHUNK - POOL_HIST:POOL_PAD + CHUNK, :]
        st_ref[0] = s_ref[...]


def _even_prompt_mix(cos_ref, sin_ref, mask_ref, qd_ref, kd_ref, wp_ref, sp_ref, pext_ref, s_ref,
                     chunk_decay, n, zc, y_ref, project):
    def pool():
        pext_ref[POOL_PAD:POOL_PAD + CHUNK, :] = zc[:, 0:POOL_WIDTH]
        pos = n * CHUNK + lax.broadcasted_iota(jnp.int32, (CHUNK, 1), 0)
        for g, w in enumerate(POOL_WINDOWS):
            lanes = slice(g * POOL_GROUP_DIM, (g + 1) * POOL_GROUP_DIM)
            p = pext_ref[POOL_PAD:POOL_PAD + CHUNK, lanes]
            acc = p
            for i in range(1, w):
                acc = acc + pext_ref[POOL_PAD - i:POOL_PAD - i + CHUNK, lanes]
            cnt = jnp.minimum(w, pos + 1).astype(F32)
            d = acc / cnt - p
            yg = _dot(d.astype(BF16), wp_ref[g].astype(BF16)) * sp_ref[:, lanes]
            y_ref[:, lanes] = yg.astype(BF16)
        pext_ref[0:POOL_PAD, :] = pext_ref[CHUNK:CHUNK + POOL_PAD, :]

    live = {}

    def scores_stage(h):
        cos = cos_ref[...]
        sin = sin_ref[...]
        q = _rotate(zc[:, Q_OFF + h * RET_DK:Q_OFF + (h + 1) * RET_DK], cos, sin)
        k = _rotate(zc[:, K_OFF + h * RET_DK:K_OFF + (h + 1) * RET_DK], cos, sin) * (RET_DK ** -0.5)
        v = zc[:, V_OFF + h * RET_DV:V_OFF + (h + 1) * RET_DV].astype(BF16)
        live[h] = (_dot_nt(q.astype(BF16), k.astype(BF16)), (q * qd_ref[h]).astype(BF16),
                   (k * kd_ref[h]).astype(BF16), v)

    def output_stage(h):
        scores, q_dec, k_dec, v = live[h]
        s = s_ref[h]
        o = _dot((scores * mask_ref[h]).astype(BF16), v) + _dot(q_dec, s.astype(BF16))
        s_ref[h] = s * chunk_decay[h] + _dot_tn(k_dec, v)
        live[h] = o

    def norm_stage(h):
        gate = zc[:, G_OFF + h * RET_DV:G_OFF + (h + 1) * RET_DV]
        y_ref[:, POOL_WIDTH + h * RET_DV:POOL_WIDTH + (h + 1) * RET_DV] = _retention_head_out(live.pop(h), gate)

    def heads_step(i):
        for stage, h in ((norm_stage, i - 2), (output_stage, i - 1), (scores_stage, i)):
            if 0 <= h < RET_HEADS:
                stage(h)

    _interleave(project, [pool] + [functools.partial(heads_step, i) for i in range(RET_HEADS + 2)])


def _even_prompt(a, w_in, h, w_out, g, g_ffn, w_pool, s_pool, layer, prev_hist, prev_state):
    cos, sin = _rotary_tables(np.arange(SEQ))
    mask, qd, kd, cd = _retention_tables(CHUNK)
    const = lambda *shape: pl.BlockSpec(shape, lambda t: (0,) * len(shape))
    pipe_in, pipe_out, pipe_shapes, pipe_scratch = _pipeline_specs(layer, EVEN_IN)
    return _call_stacked(
        functools.partial(_even_prompt_kernel, chunk_decay=tuple(float(c) for c in cd)),
        name="even_layer_prompt",
        grid=(PIPE_STEPS,),
        in_specs=pipe_in + [
            pl.BlockSpec((CHUNK, RET_DK), lambda t: (_mixed_chunk(t) % N_CHUNKS, 0)),
            pl.BlockSpec((CHUNK, RET_DK), lambda t: (_mixed_chunk(t) % N_CHUNKS, 0)),
            const(RET_HEADS, CHUNK, CHUNK),
            const(RET_HEADS, CHUNK, RET_DK),
            const(RET_HEADS, CHUNK, RET_DK),
            pl.BlockSpec((None, POOL_GROUPS, POOL_GROUP_DIM, POOL_GROUP_DIM), lambda t: (layer, 0, 0, 0)),
            pl.BlockSpec((None, 1, POOL_WIDTH), lambda t: (layer, 0, 0)),
        ],
        args=(a, w_in, h, w_out, g.reshape(1, D_MODEL), g_ffn.reshape(1, D_MODEL), cos, sin,
              jnp.asarray(mask, F32), jnp.asarray(_lane_bcast(qd, RET_DK), F32),
              jnp.asarray(_lane_bcast(kd, RET_DK), F32), w_pool, s_pool.reshape(N_EVEN, 1, POOL_WIDTH)),
        out_specs=pipe_out + [
            pl.BlockSpec((None, 1, POOL_HIST, POOL_WIDTH), lambda t: (layer, _mixed_chunk(t) // N_CHUNKS, 0, 0)),
            pl.BlockSpec((None, 1, RET_HEADS, RET_DK, RET_DV),
                         lambda t: (layer, _mixed_chunk(t) // N_CHUNKS, 0, 0, 0)),
        ],
        out_shape=pipe_shapes + [
            jax.ShapeDtypeStruct((N_EVEN, BATCH, POOL_HIST, POOL_WIDTH), F32),
            jax.ShapeDtypeStruct((N_EVEN, BATCH, RET_HEADS, RET_DK, RET_DV), F32),
        ],
        stacked={2: prev_hist, 3: prev_state},
        sem=("arbitrary",),
        scratch_shapes=pipe_scratch + [
            pltpu.VMEM((POOL_PAD + CHUNK, POOL_WIDTH), F32),
            pltpu.VMEM((RET_HEADS, RET_DK, RET_DV), F32),
        ],
    )


def _even_sample_kernel(z_ref, cos_ref, sin_ref, mask_ref, qd_ref, kd_ref, wp_ref, sp_ref, hist_ref, st_ref,
                        y_ref, nhist_ref, nst_ref, pext_ref, d_ref, *, chunk_decay):
    t_idx = lax.broadcasted_iota(jnp.int32, (DEC_SEQ, 1), 0)
    for b in range(SAMPLE_BB):
        rows = slice(b * DEC_SEQ, (b + 1) * DEC_SEQ)
        pext_ref[0:POOL_HIST, :] = hist_ref[b]
        pext_ref[POOL_HIST:POOL_HIST + DEC_SEQ, :] = z_ref[rows, 0:POOL_WIDTH]
        for g, w in enumerate(POOL_WINDOWS):
            lanes = slice(g * POOL_GROUP_DIM, (g + 1) * POOL_GROUP_DIM)
            p = pext_ref[POOL_HIST:POOL_HIST + DEC_SEQ, lanes]
            acc = p
            for i in range(1, w):
                acc = acc + pext_ref[POOL_HIST - i:POOL_HIST - i + DEC_SEQ, lanes]
            cnt = jnp.minimum(w, PAST_LEN + t_idx + 1).astype(F32)
            d_ref[rows, lanes] = acc / cnt - p
        nhist_ref[b] = pext_ref[DEC_SEQ:DEC_SEQ + POOL_HIST, :]
    for g in range(POOL_GROUPS):
        lanes = slice(g * POOL_GROUP_DIM, (g + 1) * POOL_GROUP_DIM)
        yg = _dot(d_ref[:, lanes].astype(BF16), wp_ref[g].astype(BF16)) * sp_ref[:, lanes]
        y_ref[:, lanes] = yg.astype(BF16)

    cos = cos_ref[...]
    sin = sin_ref[...]
    seq_of_row = lax.broadcasted_iota(jnp.int32, (SAMPLE_ROWS, 1), 0) // DEC_SEQ
    for h in range(RET_HEADS):
        q = _rotate(z_ref[:, Q_OFF + h * RET_DK:Q_OFF + (h + 1) * RET_DK], cos, sin)
        k = _rotate(z_ref[:, K_OFF + h * RET_DK:K_OFF + (h + 1) * RET_DK], cos, sin) * (RET_DK ** -0.5)
        v = z_ref[:, V_OFF + h * RET_DV:V_OFF + (h + 1) * RET_DV].astype(BF16)
        gate = z_ref[:, G_OFF + h * RET_DV:G_OFF + (h + 1) * RET_DV]
        scores = _dot_nt(q.astype(BF16), k.astype(BF16)) * mask_ref[h]
        o = _dot(scores.astype(BF16), v)
        q_dec = (q * qd_ref[h]).astype(BF16)
        k_dec = k * kd_ref[h]
        for b in range(SAMPLE_BB):
            own = seq_of_row == b
            s = st_ref[b, h]
            o = o + jnp.where(own, _dot(q_dec, s.astype(BF16)), 0.0)
            nst_ref[b, h] = s * chunk_decay[h] + _dot_tn(jnp.where(own, k_dec, 0.0).astype(BF16), v)
        y_ref[:, POOL_WIDTH + h * RET_DV:POOL_WIDTH + (h + 1) * RET_DV] = _retention_head_out(o, gate)


def _even_sample(z, w_pool, s_pool, state_pool, state_ret, layer, prev_hist, prev_state):
    pos = PAST_LEN + np.arange(DEC_SEQ)
    cos, sin = _rotary_tables(np.tile(pos, SAMPLE_BB))
    mask, qd, kd, cd = _retention_tables(DEC_SEQ)
    mask = np.stack([np.kron(np.eye(SAMPLE_BB), m) for m in mask])
    qd = _lane_bcast(np.tile(qd, (1, SAMPLE_BB)), RET_DK)
    kd = _lane_bcast(np.tile(kd, (1, SAMPLE_BB)), RET_DK)
    const = lambda *shape: pl.BlockSpec(shape, lambda i: (0,) * len(shape))
    return _call_stacked(
        functools.partial(_even_sample_kernel, chunk_decay=tuple(float(c) for c in cd)),
        name="even_mixer_sample",
        grid=(DEC_BATCH // SAMPLE_BB,),
        in_specs=[
            pl.BlockSpec((SAMPLE_ROWS, EVEN_IN), lambda i: (i, 0)),
            const(SAMPLE_ROWS, RET_DK),
            const(SAMPLE_ROWS, RET_DK),
            const(RET_HEADS, SAMPLE_ROWS, SAMPLE_ROWS),
            const(RET_HEADS, SAMPLE_ROWS, RET_DK),
            const(RET_HEADS, SAMPLE_ROWS, RET_DK),
            pl.BlockSpec((None, POOL_GROUPS, POOL_GROUP_DIM, POOL_GROUP_DIM), lambda i: (layer, 0, 0, 0)),
            pl.BlockSpec((None, 1, POOL_WIDTH), lambda i: (layer, 0, 0)),
            pl.BlockSpec((None, SAMPLE_BB, POOL_HIST, POOL_WIDTH), lambda i: (layer, i, 0, 0)),
            pl.BlockSpec((None, SAMPLE_BB, RET_HEADS, RET_DK, RET_DV), lambda i: (layer, i, 0, 0, 0)),
        ],
        args=(z, cos, sin, jnp.asarray(mask, F32), jnp.asarray(qd, F32), jnp.asarray(kd, F32),
              w_pool, s_pool.reshape(N_EVEN, 1, POOL_WIDTH), state_pool, state_ret),
        out_specs=[
            pl.BlockSpec((SAMPLE_ROWS, D_MODEL), lambda i: (i, 0)),
            pl.BlockSpec((None, SAMPLE_BB, POOL_HIST, POOL_WIDTH), lambda i: (layer, i, 0, 0)),
            pl.BlockSpec((None, SAMPLE_BB, RET_HEADS, RET_DK, RET_DV), lambda i: (layer, i, 0, 0, 0)),
        ],
        out_shape=[
            jax.ShapeDtypeStruct((N_SAMPLE, D_MODEL), BF16),
            jax.ShapeDtypeStruct((N_EVEN, DEC_BATCH, POOL_HIST, POOL_WIDTH), F32),
            jax.ShapeDtypeStruct((N_EVEN, DEC_BATCH, RET_HEADS, RET_DK, RET_DV), F32),
        ],
        stacked={1: prev_hist, 2: prev_state},
        sem=("parallel",),
        scratch_shapes=[
            pltpu.VMEM((POOL_HIST + DEC_SEQ, POOL_WIDTH), F32),
            pltpu.VMEM((SAMPLE_ROWS, POOL_WIDTH), F32),
        ],
    )


CONV_LANES = 128


def _depthwise_conv(ext_ref, first_row, n_rows, dw_ref, dwb_ref, out_ref, out_row):
    for c in range(0, CONV_CH, CONV_LANES):
        lanes = slice(c, c + CONV_LANES)
        acc = jnp.broadcast_to(dwb_ref[:, lanes], (n_rows, CONV_LANES))
        for j in range(CONV_K):
            acc = acc + ext_ref[first_row + j:first_row + j + n_rows, lanes] * dw_ref[j:j + 1, lanes]
        out_ref[out_row:out_row + n_rows, lanes] = acc


SUBLANES = 8
SHIFTED_ROWS = CONV_PAD + CHUNK - SUBLANES


def _odd_prompt_kernel(a_ref, w_ref, h_ref, wo_ref, g_ref, gf_ref, lng_ref, lnb_ref, sgw_ref, sgb_ref, dw_ref, dwb_ref,
                       cvg_ref, cvb_ref, o_ref, f_ref, cst_ref, z0_ref, z1_ref, y0_ref, y1_ref, r_ref,
                       vb_ref, ext_ref, cv_ref, xs_ref):
    step = pl.program_id(0)
    n = (step + N_CHUNKS - 1) % N_CHUNKS

    @pl.when(jnp.logical_or(n == 0, step == 0))
    def _():
        ext_ref[0:CONV_PAD, :] = jnp.zeros((CONV_PAD, CONV_CH), F32)

    _pipeline_step(a_ref, w_ref, (z0_ref, z1_ref), ODD_IN, (y0_ref, y1_ref), wo_ref, r_ref, h_ref, g_ref, gf_ref,
                   o_ref, f_ref,
                   functools.partial(_odd_prompt_mix, lng_ref, lnb_ref, sgw_ref, sgb_ref, dw_ref, dwb_ref, cvg_ref,
                                     cvb_ref, vb_ref, ext_ref, cv_ref, xs_ref))

    @pl.when(jnp.logical_and(n == N_CHUNKS - 1, jnp.logical_and(step > 0, step <= N_PROMPT_CHUNKS)))
    def _():
        cst_ref[0] = ext_ref[CONV_PAD + CHUNK - CONV_HIST:CONV_PAD + CHUNK, :]

    ext_ref[0:CONV_PAD, :] = ext_ref[CHUNK:CHUNK + CONV_PAD, :]


def _odd_prompt_mix(lng_ref, lnb_ref, sgw_ref, sgb_ref, dw_ref, dwb_ref, cvg_ref, cvb_ref,
                    vb_ref, ext_ref, cv_ref, xs_ref, n, zc, y_ref, project):
    def gate_values():
        v = _layer_norm(jax.nn.gelu(zc[:, SG_WIDTH:2 * SG_WIDTH]), lng_ref[...], lnb_ref[...])
        vb_ref[...] = v.astype(BF16)

    def gating(g):
        lanes = slice(g * SG_GROUP_DIM, (g + 1) * SG_GROUP_DIM)
        row = lax.broadcasted_iota(jnp.int32, (CHUNK, CHUNK), 0)
        col = lax.broadcasted_iota(jnp.int32, (CHUNK, CHUNK), 1)
        ws = jnp.where(col <= row, sgw_ref[g], 0.0).astype(BF16)
        mixed = _dot(ws, vb_ref[:, lanes]) + sgb_ref[:, g:g + 1]
        y_ref[:, lanes] = (jax.nn.gelu(zc[:, lanes]) * mixed).astype(BF16)

    def glu():
        a = zc[:, 2 * SG_WIDTH:2 * SG_WIDTH + CONV_CH]
        gate = zc[:, 2 * SG_WIDTH + CONV_CH:2 * SG_WIDTH + 2 * CONV_CH]
        ext_ref[CONV_PAD:CONV_PAD + CHUNK, :] = a * jax.nn.sigmoid(gate)

    def shifted_copy(s):
        xs_ref[s - 1] = ext_ref[s:s + SHIFTED_ROWS, :]

    def conv(c):
        lanes = slice(c, c + CONV_LANES)
        acc = jnp.broadcast_to(dwb_ref[:, lanes], (CHUNK, CONV_LANES))
        for j in range(CONV_K):
            tile, s = divmod(CONV_PAD - CONV_HIST + j, SUBLANES)
            rows = slice(tile * SUBLANES, tile * SUBLANES + CHUNK)
            window = ext_ref[rows, lanes] if s == 0 else xs_ref[s - 1, rows, lanes]
            acc = acc + window * dw_ref[j:j + 1, lanes]
        cv_ref[:, lanes] = acc

    def conv_out():
        yd = _layer_norm(cv_ref[...], cvg_ref[...], cvb_ref[...])
        y_ref[:, SG_WIDTH:SG_WIDTH + CONV_CH] = (yd * jax.nn.sigmoid(yd)).astype(BF16)

    _interleave(project, [gate_values, glu] + [functools.partial(shifted_copy, s) for s in range(1, SUBLANES)]
                + [functools.partial(conv, c) for c in range(0, CONV_CH, CONV_LANES)] + [conv_out])
    for g in range(SG_GROUPS):
        gating(g)


def _odd_weight_specs(layer):
    per_layer = lambda *shape: pl.BlockSpec((None,) + shape, lambda *_: (layer,) + (0,) * len(shape))
    return dict(
        ln=per_layer(1, SG_WIDTH),
        sgw=per_layer(SG_GROUPS, SG_CHUNK, SG_CHUNK),
        sgb=per_layer(SG_CHUNK, SG_GROUPS),
        dw=per_layer(CONV_K, CONV_CH),
        ch=per_layer(1, CONV_CH),
    )


def _odd_prompt(a, w_in, h, w_out, g, g_ffn, sg_ln_g, sg_ln_b, sg_w, sg_b, dw_w, dw_b, cv_ln_g, cv_ln_b, layer,
                prev_cst):
    spec = _odd_weight_specs(layer)
    row = lambda x: x.reshape(N_ODD, 1, -1)
    pipe_in, pipe_out, pipe_shapes, pipe_scratch = _pipeline_specs(layer, ODD_IN)
    return _call_stacked(
        _odd_prompt_kernel,
        name="odd_layer_prompt",
        grid=(PIPE_STEPS,),
        in_specs=pipe_in + [
            spec["ln"], spec["ln"], spec["sgw"], spec["sgb"], spec["dw"], spec["ch"], spec["ch"], spec["ch"],
        ],
        args=(a, w_in, h, w_out, g.reshape(1, D_MODEL), g_ffn.reshape(1, D_MODEL), row(sg_ln_g), row(sg_ln_b),
              sg_w, jnp.swapaxes(sg_b, 1, 2), dw_w, row(dw_b), row(cv_ln_g), row(cv_ln_b)),
        out_specs=pipe_out + [
            pl.BlockSpec((None, 1, CONV_HIST, CONV_CH), lambda t: (layer, _mixed_chunk(t) // N_CHUNKS, 0, 0)),
        ],
        out_shape=pipe_shapes + [
            jax.ShapeDtypeStruct((N_ODD, BATCH, CONV_HIST, CONV_CH), F32),
        ],
        stacked={2: prev_cst},
        sem=("arbitrary",),
        scratch_shapes=pipe_scratch + [
            pltpu.VMEM((CHUNK, SG_WIDTH), BF16),
            pltpu.VMEM((CONV_PAD + CHUNK, CONV_CH), F32),
            pltpu.VMEM((CHUNK, CONV_CH), F32),
            pltpu.VMEM((SUBLANES - 1, SHIFTED_ROWS, CONV_CH), F32),
        ],
    )


SG_SHIFT_PAD = 8


def _odd_sample_kernel(sgw_ref, sgb_ref, z_ref, lng_ref, lnb_ref, dw_ref, dwb_ref, cvg_ref, cvb_ref, cst_ref,
                       y_ref, sgv_ref, ncst_ref, vs_ref, ext_ref, cv_ref):
    u = jax.nn.gelu(z_ref[:, 0:SG_WIDTH])
    v = _layer_norm(jax.nn.gelu(z_ref[:, SG_WIDTH:2 * SG_WIDTH]), lng_ref[...], lnb_ref[...])
    sgv_ref[...] = v
    vs_ref[0:SG_SHIFT_PAD, :] = jnp.zeros((SG_SHIFT_PAD, SG_WIDTH), F32)
    vs_ref[SG_SHIFT_PAD:SG_SHIFT_PAD + SAMPLE_ROWS, :] = v
    t_of_row = lax.broadcasted_iota(jnp.int32, (SAMPLE_ROWS, 1), 0) % DEC_SEQ
    for g in range(SG_GROUPS):
        lanes = slice(g * SG_GROUP_DIM, (g + 1) * SG_GROUP_DIM)
        mixed = jnp.zeros((SAMPLE_ROWS, SG_GROUP_DIM), F32)
        for back in range(DEC_SEQ):
            coef = jnp.zeros((SAMPLE_ROWS, 1), F32)
            for t in range(back, DEC_SEQ):
                coef = jnp.where(t_of_row == t, sgw_ref[(g * DEC_SEQ + t) * DEC_SEQ + t - back], coef)
            shifted = vs_ref[SG_SHIFT_PAD - back:SG_SHIFT_PAD - back + SAMPLE_ROWS, lanes]
            mixed = mixed + coef * shifted
        bias = jnp.zeros((SAMPLE_ROWS, 1), F32)
        for t in range(DEC_SEQ):
            bias = jnp.where(t_of_row == t, sgb_ref[g * DEC_SEQ + t], bias)
        y_ref[:, lanes] = (u[:, lanes] * (mixed + bias)).astype(BF16)

    a = z_ref[:, 2 * SG_WIDTH:2 * SG_WIDTH + CONV_CH]
    gate = z_ref[:, 2 * SG_WIDTH + CONV_CH:2 * SG_WIDTH + 2 * CONV_CH]
    glu = a * jax.nn.sigmoid(gate)
    for b in range(SAMPLE_BB):
        rows = slice(b * DEC_SEQ, (b + 1) * DEC_SEQ)
        ext_ref[0:CONV_HIST, :] = cst_ref[b]
        ext_ref[CONV_HIST:CONV_HIST + DEC_SEQ, :] = glu[rows, :]
        _depthwise_conv(ext_ref, 0, DEC_SEQ, dw_ref, dwb_ref, cv_ref, b * DEC_SEQ)
        ncst_ref[b] = ext_ref[DEC_SEQ:DEC_SEQ + CONV_HIST, :]
    yd = _layer_norm(cv_ref[...], cvg_ref[...], cvb_ref[...])
    y_ref[:, SG_WIDTH:SG_WIDTH + CONV_CH] = (yd * jax.nn.sigmoid(yd)).astype(BF16)


def _odd_sample(z, sg_ln_g, sg_ln_b, sg_w, sg_b, dw_w, dw_b, cv_ln_g, cv_ln_b, state_conv, layer,
                prev_sgv, prev_cst):
    spec = _odd_weight_specs(layer)
    row = lambda a: a.reshape(N_ODD, 1, -1)
    smem = pl.BlockSpec(memory_space=pltpu.SMEM)
    return _call_stacked(
        _odd_sample_kernel,
        name="odd_mixer_sample",
        grid=(DEC_BATCH // SAMPLE_BB,),
        in_specs=[
            smem,
            smem,
            pl.BlockSpec((SAMPLE_ROWS, ODD_IN), lambda i: (i, 0)),
            spec["ln"], spec["ln"], spec["dw"], spec["ch"], spec["ch"], spec["ch"],
            pl.BlockSpec((None, SAMPLE_BB, CONV_HIST, CONV_CH), lambda i: (layer, i, 0, 0)),
        ],
        args=(sg_w[layer, :, :DEC_SEQ, :DEC_SEQ].reshape(-1), sg_b[layer, :, :DEC_SEQ].reshape(-1), z,
              row(sg_ln_g), row(sg_ln_b), dw_w, row(dw_b), row(cv_ln_g), row(cv_ln_b), state_conv),
        out_specs=[
            pl.BlockSpec((SAMPLE_ROWS, D_MODEL), lambda i: (i, 0)),
            pl.BlockSpec((None, SAMPLE_ROWS, SG_WIDTH), lambda i: (layer, i, 0)),
            pl.BlockSpec((None, SAMPLE_BB, CONV_HIST, CONV_CH), lambda i: (layer, i, 0, 0)),
        ],
        out_shape=[
            jax.ShapeDtypeStruct((N_SAMPLE, D_MODEL), BF16),
            jax.ShapeDtypeStruct((N_ODD, N_SAMPLE, SG_WIDTH), F32),
            jax.ShapeDtypeStruct((N_ODD, DEC_BATCH, CONV_HIST, CONV_CH), F32),
        ],
        stacked={1: prev_sgv, 2: prev_cst},
        sem=("parallel",),
        scratch_shapes=[
            pltpu.VMEM((SG_SHIFT_PAD + SAMPLE_ROWS, SG_WIDTH), F32),
            pltpu.VMEM((CONV_HIST + DEC_SEQ, CONV_CH), F32),
            pltpu.VMEM((SAMPLE_ROWS, CONV_CH), F32),
        ],
    )


def kernel(x_prompt, x_sample, state_pool, state_ret, state_conv, norm_mix_pre, norm_mix_post, norm_ffn_pre, norm_ffn_post, w_in_even, w_pool, s_pool, w_out_even, w_in_odd, sg_ln_g, sg_ln_b, sg_w, sg_b, dw_w, dw_b, cv_ln_g, cv_ln_b, w_out_odd, w_up, w_down):
    w_in_even, w_out_even, w_in_odd, w_out_odd, w_up, w_down = (
        w.astype(BF16) for w in (w_in_even, w_out_even, w_in_odd, w_out_odd, w_up, w_down))

    h_p = x_prompt.reshape(N_PROMPT, D_MODEL)
    h_s = x_sample.reshape(N_SAMPLE, D_MODEL)
    a_p = _prenorm(h_p, norm_mix_pre[0])
    a_s = _prenorm(h_s, norm_mix_pre[0])
    pool_p = pool_s = ret_p = ret_s = conv_p = conv_s = sgv_s = None
    for l in range(DEPTH):
        i = l // 2
        g, g_ffn = norm_mix_post[l], norm_ffn_pre[l]
        if l % 2 == 0:
            w_out = w_out_even
            h_p, f_p, pool_p, ret_p = _even_prompt(a_p, w_in_even, h_p, w_out, g, g_ffn, w_pool, s_pool, i,
                                                   pool_p, ret_p)
            z_s = _in_proj_sample(a_s, w_in_even, i)
            y_s, pool_s, ret_s = _even_sample(z_s, w_pool, s_pool, state_pool, state_ret, i, pool_s, ret_s)
        else:
            w_out = w_out_odd
            h_p, f_p, conv_p = _odd_prompt(a_p, w_in_odd, h_p, w_out, g, g_ffn, sg_ln_g, sg_ln_b, sg_w, sg_b,
                                           dw_w, dw_b, cv_ln_g, cv_ln_b, i, conv_p)
            z_s = _in_proj_sample(a_s, w_in_odd, i)
            y_s, sgv_s, conv_s = _odd_sample(z_s, sg_ln_g, sg_ln_b, sg_w, sg_b, dw_w, dw_b, cv_ln_g, cv_ln_b,
                                             state_conv, i, sgv_s, conv_s)
        h_s, f_s = _out_proj_sample(y_s, w_out, i, g, g_ffn, h_s)
        if l + 1 < DEPTH:
            h_p, a_p = _ffn(f_p, h_p, w_up, w_down, l, norm_ffn_post[l], norm_mix_pre[l + 1])
            h_s, a_s = _ffn(f_s, h_s, w_up, w_down, l, norm_ffn_post[l], norm_mix_pre[l + 1])
        else:
            y_prompt = _ffn_final(f_p, h_p, w_up, w_down, l, norm_ffn_post[l])
            y_sample = _ffn_final(f_s, h_s, w_up, w_down, l, norm_ffn_post[l])

    return (y_prompt.reshape(BATCH, SEQ, D_MODEL), y_sample.reshape(DEC_BATCH, DEC_SEQ, D_MODEL),
            pool_p, pool_s, ret_p, ret_s, conv_p, conv_s, sgv_s.reshape(N_ODD, DEC_BATCH, DEC_SEQ, SG_WIDTH))
```

```python
import functools

import jax
import jax.numpy as jnp
import numpy as np
from jax import lax
from jax.experimental import pallas as pl
from jax.experimental.pallas import tpu as pltpu

F32 = jnp.float32
BF16 = jnp.bfloat16

D_MODEL = 2048
BATCH = 4
SEQ = 2048
DEPTH = 4
DEC_BATCH = 128
DEC_SEQ = 4
PAST_LEN = 16384

N_EVEN = (DEPTH + 1) // 2
N_ODD = DEPTH // 2

POOL_WINDOWS = (2, 4, 8, 16)
POOL_GROUPS = len(POOL_WINDOWS)
POOL_WIDTH = D_MODEL // 4
POOL_GROUP_DIM = POOL_WIDTH // POOL_GROUPS
POOL_HIST = max(POOL_WINDOWS) - 1
RET_WIDTH = D_MODEL - POOL_WIDTH
RET_HEADS = 6
RET_DV = RET_WIDTH // RET_HEADS
RET_DK = RET_DV // 2
RET_QK = RET_HEADS * RET_DK
RET_CHUNK = 128
ROPE_BASE = 10000.0
SG_WIDTH = D_MODEL // 2
SG_CHUNK = 128
SG_GROUPS = 4
SG_GROUP_DIM = SG_WIDTH // SG_GROUPS
CONV_CH = D_MODEL // 2
CONV_K = 31
CONV_HIST = CONV_K - 1
D_FF = 4 * D_MODEL
EPS = 1e-6

EVEN_IN = POOL_WIDTH + 2 * RET_QK + 2 * RET_WIDTH
ODD_IN = 2 * SG_WIDTH + 2 * CONV_CH

Q_OFF = POOL_WIDTH
K_OFF = Q_OFF + RET_QK
V_OFF = K_OFF + RET_QK
G_OFF = V_OFF + RET_WIDTH

N_PROMPT = BATCH * SEQ
N_SAMPLE = DEC_BATCH * DEC_SEQ
N_TOK = N_PROMPT + N_SAMPLE

SUBLANES = 8
CHUNK = 128
N_CHUNKS = SEQ // CHUNK
SAMPLE_BB = SUBLANES
SAMPLE_ROWS = SAMPLE_BB * DEC_SEQ
POOL_PAD = 16
CONV_PAD = 32

TN_IN = 1024
TM_OUT = 512
SLAB = 128
TF = 1024
VMEM_LIMIT = 56 * 1024 * 1024

N_PROMPT_TILES = N_PROMPT // TM_OUT
assert N_PROMPT % TM_OUT == 0 and N_SAMPLE == TM_OUT and TM_OUT % SLAB == 0
assert POOL_PAD >= POOL_HIST and CONV_PAD >= CONV_HIST and DEC_BATCH % SAMPLE_BB == 0


def _params(*sem):
    return pltpu.CompilerParams(dimension_semantics=sem, vmem_limit_bytes=VMEM_LIMIT)


def _rms_scale(x, g):
    return x * lax.rsqrt(jnp.mean(x * x, axis=-1, keepdims=True) + EPS) * g


def _layer_norm(x, g, b):
    xc = x - jnp.mean(x, axis=-1, keepdims=True)
    return xc * lax.rsqrt(jnp.mean(xc * xc, axis=-1, keepdims=True) + EPS) * g + b


def _dot(a, b):
    return jnp.dot(a, b, preferred_element_type=F32)


def _dot_nt(a, b):
    return lax.dot_general(a, b, (((1,), (1,)), ((), ())), preferred_element_type=F32)


def _dot_tn(a, b):
    return lax.dot_general(a, b, (((0,), (0,)), ((), ())), preferred_element_type=F32)


def _skip_aliased(body, n_in, n_aliased):
    def wrapped(*refs):
        return body(*refs[:n_in], *refs[n_in + n_aliased:])
    return wrapped


def _call_stacked(body, *, name, grid, in_specs, args, out_specs, out_shape, stacked, sem, scratch_shapes):
    prev = [(o, p) for o, p in sorted(stacked.items()) if p is not None]
    n_in = len(args)
    return pl.pallas_call(
        _skip_aliased(body, n_in, len(prev)),
        grid=grid,
        in_specs=list(in_specs) + [pl.BlockSpec(memory_space=pl.ANY)] * len(prev),
        out_specs=out_specs,
        out_shape=out_shape,
        input_output_aliases={n_in + j: o for j, (o, _) in enumerate(prev)},
        scratch_shapes=scratch_shapes,
        compiler_params=_params(*sem),
        name=name,
    )(*args, *[p for _, p in prev])


def _is_prompt_tile():
    return pl.program_id(0) < N_PROMPT_TILES


def _prenorm_kernel(xp_ref, xs_ref, g_ref, h_ref, a_ref):
    def emit(x_ref):
        x = x_ref[...]
        h_ref[...] = x
        a_ref[...] = _rms_scale(x, g_ref[...]).astype(BF16)

    pl.when(_is_prompt_tile())(lambda: emit(xp_ref))
    pl.when(jnp.logical_not(_is_prompt_tile()))(lambda: emit(xs_ref))


def _prenorm(x_prompt, x_sample, g):
    return pl.pallas_call(
        _prenorm_kernel,
        grid=(N_TOK // TM_OUT,),
        in_specs=[
            pl.BlockSpec((TM_OUT, D_MODEL), lambda i: (jnp.minimum(i, N_PROMPT_TILES - 1), 0)),
            pl.BlockSpec((TM_OUT, D_MODEL), lambda i: (0, 0)),
            pl.BlockSpec((1, D_MODEL), lambda i: (0, 0)),
        ],
        out_specs=[
            pl.BlockSpec((TM_OUT, D_MODEL), lambda i: (i, 0)),
            pl.BlockSpec((TM_OUT, D_MODEL), lambda i: (i, 0)),
        ],
        out_shape=[
            jax.ShapeDtypeStruct((N_TOK, D_MODEL), F32),
            jax.ShapeDtypeStruct((N_TOK, D_MODEL), BF16),
        ],
        compiler_params=_params("parallel"),
        name="join_prenorm",
    )(x_prompt, x_sample, g.reshape(1, D_MODEL))


def _in_proj_kernel(a_ref, w_ref, o_ref):
    o_ref[...] = _dot(a_ref[...], w_ref[...])


def _in_proj_sample(a, w, layer):
    n_out = w.shape[2]
    return pl.pallas_call(
        _in_proj_kernel,
        grid=(n_out // TN_IN,),
        in_specs=[
            pl.BlockSpec((N_SAMPLE, D_MODEL), lambda j: (N_PROMPT // N_SAMPLE, 0)),
            pl.BlockSpec((None, D_MODEL, TN_IN), lambda j: (layer, 0, j)),
        ],
        out_specs=pl.BlockSpec((N_SAMPLE, TN_IN), lambda j: (0, j)),
        out_shape=jax.ShapeDtypeStruct((N_SAMPLE, n_out), F32),
        compiler_params=_params("parallel"),
        name="in_proj_sample",
    )(a, w)


def _out_proj_kernel(yp_ref, ys_ref, w_ref, g_ref, gf_ref, h_ref, o_ref, f_ref):
    def finish(y_ref):
        for r in range(0, TM_OUT, SLAB):
            rows = slice(r, r + SLAB)
            y = y_ref[rows, :].astype(BF16)
            hn = h_ref[rows, :] + _rms_scale(_dot(y, w_ref[...]), g_ref[...])
            o_ref[rows, :] = hn
            f_ref[rows, :] = _rms_scale(hn, gf_ref[...]).astype(BF16)

    pl.when(_is_prompt_tile())(lambda: finish(yp_ref))
    pl.when(jnp.logical_not(_is_prompt_tile()))(lambda: finish(ys_ref))


def _out_proj(y_prompt, y_sample, w, layer, g, g_ffn, h):
    return pl.pallas_call(
        _out_proj_kernel,
        grid=(N_TOK // TM_OUT,),
        in_specs=[
            pl.BlockSpec((TM_OUT, D_MODEL), lambda i: (jnp.minimum(i, N_PROMPT_TILES - 1), 0)),
            pl.BlockSpec((TM_OUT, D_MODEL), lambda i: (0, 0)),
            pl.BlockSpec((None, D_MODEL, D_MODEL), lambda i: (layer, 0, 0)),
            pl.BlockSpec((1, D_MODEL), lambda i: (0, 0)),
            pl.BlockSpec((1, D_MODEL), lambda i: (0, 0)),
            pl.BlockSpec((TM_OUT, D_MODEL), lambda i: (i, 0)),
        ],
        out_specs=[
            pl.BlockSpec((TM_OUT, D_MODEL), lambda i: (i, 0)),
            pl.BlockSpec((TM_OUT, D_MODEL), lambda i: (i, 0)),
        ],
        out_shape=[
            jax.ShapeDtypeStruct((N_TOK, D_MODEL), F32),
            jax.ShapeDtypeStruct((N_TOK, D_MODEL), BF16),
        ],
        compiler_params=_params("parallel"),
        name="out_proj_norm_residual",
    )(y_prompt, y_sample, w, g.reshape(1, D_MODEL), g_ffn.reshape(1, D_MODEL), h)


def _ffn_accumulate(f_ref, wu_ref, wd_ref, acc_ref):
    k = pl.program_id(1)

    @pl.when(k == 0)
    def _():
        acc_ref[...] = jnp.zeros_like(acc_ref)

    u = jnp.square(jnp.maximum(_dot(f_ref[...], wu_ref[...]), 0.0)).astype(BF16)
    acc_ref[...] += _dot(u, wd_ref[...])
    return k == pl.num_programs(1) - 1


def _ffn_kernel(f_ref, h_ref, wu_ref, wd_ref, g2_ref, gn_ref, o_ref, a_ref, acc_ref):
    is_last = _ffn_accumulate(f_ref, wu_ref, wd_ref, acc_ref)

    @pl.when(is_last)
    def _():
        for r in range(0, TM_OUT, SLAB):
            rows = slice(r, r + SLAB)
            hn = h_ref[rows, :] + _rms_scale(acc_ref[rows, :], g2_ref[...])
            o_ref[rows, :] = hn
            a_ref[rows, :] = _rms_scale(hn, gn_ref[...]).astype(BF16)


def _ffn_final_kernel(f_ref, h_ref, wu_ref, wd_ref, g2_ref, yp_ref, ys_ref, acc_ref):
    is_last = _ffn_accumulate(f_ref, wu_ref, wd_ref, acc_ref)

    def finish(o_ref):
        for r in range(0, TM_OUT, SLAB):
            rows = slice(r, r + SLAB)
            o_ref[rows, :] = h_ref[rows, :] + _rms_scale(acc_ref[rows, :], g2_ref[...])

    pl.when(jnp.logical_and(is_last, _is_prompt_tile()))(lambda: finish(yp_ref))
    pl.when(jnp.logical_and(is_last, jnp.logical_not(_is_prompt_tile())))(lambda: finish(ys_ref))


def _ffn_in_specs(layer, n_gains):
    return [
        pl.BlockSpec((TM_OUT, D_MODEL), lambda i, k: (i, 0)),
        pl.BlockSpec((TM_OUT, D_MODEL), lambda i, k: (i, 0)),
        pl.BlockSpec((None, D_MODEL, TF), lambda i, k: (layer, 0, k)),
        pl.BlockSpec((None, TF, D_MODEL), lambda i, k: (layer, k, 0)),
    ] + [pl.BlockSpec((1, D_MODEL), lambda i, k: (0, 0))] * n_gains


def _ffn(f, h, w_up, w_down, layer, g2, g_next):
    return pl.pallas_call(
        _ffn_kernel,
        grid=(N_TOK // TM_OUT, D_FF // TF),
        in_specs=_ffn_in_specs(layer, 2),
        out_specs=[
            pl.BlockSpec((TM_OUT, D_MODEL), lambda i, k: (i, 0)),
            pl.BlockSpec((TM_OUT, D_MODEL), lambda i, k: (i, 0)),
        ],
        out_shape=[
            jax.ShapeDtypeStruct((N_TOK, D_MODEL), F32),
            jax.ShapeDtypeStruct((N_TOK, D_MODEL), BF16),
        ],
        scratch_shapes=[pltpu.VMEM((TM_OUT, D_MODEL), F32)],
        compiler_params=_params("parallel", "arbitrary"),
        name="relu2_mlp",
    )(f, h, w_up, w_down, g2.reshape(1, D_MODEL), g_next.reshape(1, D_MODEL))


def _ffn_final(f, h, w_up, w_down, layer, g2):
    return pl.pallas_call(
        _ffn_final_kernel,
        grid=(N_TOK // TM_OUT, D_FF // TF),
        in_specs=_ffn_in_specs(layer, 1),
        out_specs=[
            pl.BlockSpec((TM_OUT, D_MODEL), lambda i, k: (jnp.minimum(i, N_PROMPT_TILES - 1), 0)),
            pl.BlockSpec((TM_OUT, D_MODEL), lambda i, k: (0, 0)),
        ],
        out_shape=[
            jax.ShapeDtypeStruct((N_PROMPT, D_MODEL), F32),
            jax.ShapeDtypeStruct((N_SAMPLE, D_MODEL), F32),
        ],
        scratch_shapes=[pltpu.VMEM((TM_OUT, D_MODEL), F32)],
        compiler_params=_params("arbitrary", "arbitrary"),
        name="relu2_mlp_final",
    )(f, h, w_up, w_down, g2.reshape(1, D_MODEL))


def _rotary_tables(pos):
    half = RET_DK // 2
    inv = ROPE_BASE ** (-np.arange(half, dtype=np.float64) / half)
    ang = np.asarray(pos, np.float64)[:, None] * inv[None, :]
    cos = np.concatenate([np.cos(ang), np.cos(ang)], axis=-1)
    sin = np.concatenate([-np.sin(ang), np.sin(ang)], axis=-1)
    return jnp.asarray(cos, F32), jnp.asarray(sin, F32)


def _log_gamma():
    return np.log1p(-np.exp2(-5.0 - np.arange(RET_HEADS, dtype=np.float64)))


def _retention_tables(length):
    log_g = _log_gamma()
    idx = np.arange(length, dtype=np.float64)
    diff = idx[:, None] - idx[None, :]
    mask = np.where(diff[None] >= 0, np.exp(log_g[:, None, None] * np.maximum(diff, 0.0)[None]), 0.0)
    qd = np.exp(log_g[:, None] * (idx + 1.0))
    kd = np.exp(log_g[:, None] * (length - 1.0 - idx))
    cd = np.exp(log_g * length)
    return mask, qd, kd, cd


def _lane_bcast(a, width):
    return np.repeat(a[..., None], width, axis=-1)


N_PROMPT_CHUNKS = N_PROMPT // CHUNK
PROJ_COLS = 512


def _projected_chunk(step):
    return jnp.minimum(step, N_PROMPT_CHUNKS - 1)


def _mixed_chunk(step):
    return jnp.maximum(step - 1, 0)


def _interleave(first, second):
    i = j = 0
    while i < len(first) or j < len(second):
        if j >= len(second) or (i < len(first) and i * len(second) <= j * len(first)):
            first[i]()
            i += 1
        else:
            second[j]()
            j += 1


def _skewed(a_ref, w_ref, z_refs, n_cols, mix):
    step = pl.program_id(0)
    n = (step + N_CHUNKS - 1) % N_CHUNKS

    @pl.when(step == 0)
    def _():
        z_refs[1][...] = jnp.zeros_like(z_refs[1])

    for parity in range(2):
        z_next, z_mixed = z_refs[parity], z_refs[1 - parity]

        def body(z_next=z_next, z_mixed=z_mixed):
            a = a_ref[...]

            def project(c):
                z_next[:, c:c + PROJ_COLS] = _dot(a, w_ref[:, c:c + PROJ_COLS])

            mix(step, n, z_mixed, [functools.partial(project, c) for c in range(0, n_cols, PROJ_COLS)])

        pl.when(step % 2 == parity)(body)


def _retention_head_out(o, gate):
    o = o * lax.rsqrt(jnp.mean(o * o, axis=-1, keepdims=True) + EPS)
    return gate * jax.nn.sigmoid(gate) * o


def _rotate(x, cos, sin):
    return x * cos + pltpu.roll(x, RET_DK // 2, 1) * sin


def _even_prompt_kernel(a_ref, w_ref, cos_ref, sin_ref, mask_ref, qd_ref, kd_ref, wp_ref, sp_ref,
                        y_ref, hist_ref, st_ref, z0_ref, z1_ref, pext_ref, s_ref, *, chunk_decay):
    step = pl.program_id(0)
    n = (step + N_CHUNKS - 1) % N_CHUNKS

    @pl.when(jnp.logical_or(n == 0, step == 0))
    def _():
        pext_ref[0:POOL_PAD, :] = jnp.zeros((POOL_PAD, POOL_WIDTH), F32)
        s_ref[...] = jnp.zeros_like(s_ref)

    _skewed(a_ref, w_ref, (z0_ref, z1_ref), EVEN_IN,
            functools.partial(_even_prompt_mix, cos_ref, sin_ref, mask_ref, qd_ref, kd_ref, wp_ref, sp_ref, y_ref,
                              pext_ref, s_ref, chunk_decay))

    @pl.when(jnp.logical_and(n == N_CHUNKS - 1, step > 0))
    def _():
        hist_ref[0] = pext_ref[POOL_PAD + CHUNK - POOL_HIST:POOL_PAD + CHUNK, :]
        st_ref[0] = s_ref[...]


def _even_prompt_mix(cos_ref, sin_ref, mask_ref, qd_ref, kd_ref, wp_ref, sp_ref, y_ref, pext_ref, s_ref,
                     chunk_decay, step, n, zc, project):
    def pool():
        pext_ref[POOL_PAD:POOL_PAD + CHUNK, :] = zc[:, 0:POOL_WIDTH]
        pos = n * CHUNK + lax.broadcasted_iota(jnp.int32, (CHUNK, 1), 0)
        for g, w in enumerate(POOL_WINDOWS):
            lanes = slice(g * POOL_GROUP_DIM, (g + 1) * POOL_GROUP_DIM)
            p = pext_ref[POOL_PAD:POOL_PAD + CHUNK, lanes]
            acc = p
            for i in range(1, w):
                acc = acc + pext_ref[POOL_PAD - i:POOL_PAD - i + CHUNK, lanes]
            cnt = jnp.minimum(w, pos + 1).astype(F32)
            d = acc / cnt - p
            yg = _dot(d.astype(BF16), wp_ref[g].astype(BF16)) * sp_ref[:, lanes]
            y_ref[:, lanes] = yg.astype(BF16)
        pext_ref[0:POOL_PAD, :] = pext_ref[CHUNK:CHUNK + POOL_PAD, :]

    live = {}

    def scores_stage(h):
        cos = cos_ref[...]
        sin = sin_ref[...]
        q = _rotate(zc[:, Q_OFF + h * RET_DK:Q_OFF + (h + 1) * RET_DK], cos, sin)
        k = _rotate(zc[:, K_OFF + h * RET_DK:K_OFF + (h + 1) * RET_DK], cos, sin) * (RET_DK ** -0.5)
        v = zc[:, V_OFF + h * RET_DV:V_OFF + (h + 1) * RET_DV].astype(BF16)
        live[h] = (_dot_nt(q.astype(BF16), k.astype(BF16)), (q * qd_ref[h]).astype(BF16),
                   (k * kd_ref[h]).astype(BF16), v)

    def output_stage(h):
        scores, q_dec, k_dec, v = live[h]
        s = s_ref[h]
        o = _dot((scores * mask_ref[h]).astype(BF16), v) + _dot(q_dec, s.astype(BF16))
        s_ref[h] = s * chunk_decay[h] + _dot_tn(k_dec, v)
        live[h] = o

    def norm_stage(h):
        gate = zc[:, G_OFF + h * RET_DV:G_OFF + (h + 1) * RET_DV]
        y_ref[:, POOL_WIDTH + h * RET_DV:POOL_WIDTH + (h + 1) * RET_DV] = (
            _retention_head_out(live.pop(h), gate).astype(BF16))

    def heads_step(i):
        for stage, h in ((norm_stage, i - 2), (output_stage, i - 1), (scores_stage, i)):
            if 0 <= h < RET_HEADS:
                stage(h)

    _interleave(project, [pool] + [functools.partial(heads_step, i) for i in range(RET_HEADS + 2)])


def _even_prompt(a, w_in, w_pool, s_pool, layer, prev_hist, prev_state):
    cos, sin = _rotary_tables(np.arange(SEQ))
    mask, qd, kd, cd = _retention_tables(CHUNK)
    const = lambda *shape: pl.BlockSpec(shape, lambda t: (0,) * len(shape))
    return _call_stacked(
        functools.partial(_even_prompt_kernel, chunk_decay=tuple(float(c) for c in cd)),
        name="even_mixer_prompt",
        grid=(N_PROMPT_CHUNKS + 1,),
        in_specs=[
            pl.BlockSpec((CHUNK, D_MODEL), lambda t: (_projected_chunk(t), 0)),
            pl.BlockSpec((None, D_MODEL, EVEN_IN), lambda t: (layer, 0, 0), pipeline_mode=pl.Buffered(1)),
            pl.BlockSpec((CHUNK, RET_DK), lambda t: (_mixed_chunk(t) % N_CHUNKS, 0)),
            pl.BlockSpec((CHUNK, RET_DK), lambda t: (_mixed_chunk(t) % N_CHUNKS, 0)),
            const(RET_HEADS, CHUNK, CHUNK),
            const(RET_HEADS, CHUNK, RET_DK),
            const(RET_HEADS, CHUNK, RET_DK),
            pl.BlockSpec((None, POOL_GROUPS, POOL_GROUP_DIM, POOL_GROUP_DIM), lambda t: (layer, 0, 0, 0)),
            pl.BlockSpec((None, 1, POOL_WIDTH), lambda t: (layer, 0, 0)),
        ],
        args=(a, w_in, cos, sin, jnp.asarray(mask, F32), jnp.asarray(_lane_bcast(qd, RET_DK), F32),
              jnp.asarray(_lane_bcast(kd, RET_DK), F32), w_pool, s_pool.reshape(N_EVEN, 1, POOL_WIDTH)),
        out_specs=[
            pl.BlockSpec((CHUNK, D_MODEL), lambda t: (_mixed_chunk(t), 0)),
            pl.BlockSpec((None, 1, POOL_HIST, POOL_WIDTH), lambda t: (layer, _mixed_chunk(t) // N_CHUNKS, 0, 0)),
            pl.BlockSpec((None, 1, RET_HEADS, RET_DK, RET_DV),
                         lambda t: (layer, _mixed_chunk(t) // N_CHUNKS, 0, 0, 0)),
        ],
        out_shape=[
            jax.ShapeDtypeStruct((N_PROMPT, D_MODEL), BF16),
            jax.ShapeDtypeStruct((N_EVEN, BATCH, POOL_HIST, POOL_WIDTH), F32),
            jax.ShapeDtypeStruct((N_EVEN, BATCH, RET_HEADS, RET_DK, RET_DV), F32),
        ],
        stacked={1: prev_hist, 2: prev_state},
        sem=("arbitrary",),
        scratch_shapes=[
            pltpu.VMEM((CHUNK, EVEN_IN), F32),
            pltpu.VMEM((CHUNK, EVEN_IN), F32),
            pltpu.VMEM((POOL_PAD + CHUNK, POOL_WIDTH), F32),
            pltpu.VMEM((RET_HEADS, RET_DK, RET_DV), F32),
        ],
    )


def _position_slabs(x):
    return x.reshape(SAMPLE_ROWS, x.shape[-1])


def _even_sample_kernel(z_ref, cos_ref, sin_ref, mask_ref, qd_ref, kd_ref, wp_ref, sp_ref, hist_ref, st_ref,
                        y_ref, nhist_ref, nst_ref, *, chunk_decay):
    def ext(r, lanes):
        return hist_ref[r, :, lanes] if r < POOL_HIST else z_ref[r - POOL_HIST, :, lanes]

    for g, w in enumerate(POOL_WINDOWS):
        lanes = slice(g * POOL_GROUP_DIM, (g + 1) * POOL_GROUP_DIM)
        d = []
        for t in range(DEC_SEQ):
            p = ext(POOL_HIST + t, lanes)
            acc = p
            for i in range(1, w):
                acc = acc + ext(POOL_HIST + t - i, lanes)
            d.append(acc / float(min(w, PAST_LEN + t + 1)) - p)
        d = jnp.concatenate(d, axis=0)
        yg = _dot(d.astype(BF16), wp_ref[g].astype(BF16)) * sp_ref[:, lanes]
        y_ref[:, :, lanes] = yg.reshape(DEC_SEQ, SAMPLE_BB, POOL_GROUP_DIM)
    for r in range(POOL_HIST):
        nhist_ref[r] = hist_ref[r + DEC_SEQ] if r + DEC_SEQ < POOL_HIST else z_ref[r + DEC_SEQ - POOL_HIST, :,
                                                                                 0:POOL_WIDTH]

    cos = cos_ref[...]
    sin = sin_ref[...]
    seq_of_row = lax.broadcasted_iota(jnp.int32, (SAMPLE_ROWS, 1), 0) % SAMPLE_BB
    for h in range(RET_HEADS):
        q = _rotate(_position_slabs(z_ref[:, :, Q_OFF + h * RET_DK:Q_OFF + (h + 1) * RET_DK]), cos, sin)
        k = _rotate(_position_slabs(z_ref[:, :, K_OFF + h * RET_DK:K_OFF + (h + 1) * RET_DK]), cos, sin)
        k = k * (RET_DK ** -0.5)
        v = _position_slabs(z_ref[:, :, V_OFF + h * RET_DV:V_OFF + (h + 1) * RET_DV]).astype(BF16)
        gate = _position_slabs(z_ref[:, :, G_OFF + h * RET_DV:G_OFF + (h + 1) * RET_DV])
        scores = _dot_nt(q.astype(BF16), k.astype(BF16)) * mask_ref[h]
        o = _dot(scores.astype(BF16), v)
        q_dec = (q * qd_ref[h]).astype(BF16)
        k_dec = k * kd_ref[h]
        for b in range(SAMPLE_BB):
            own = seq_of_row == b
            s = st_ref[b, h]
            o = o + jnp.where(own, _dot(q_dec, s.astype(BF16)), 0.0)
            nst_ref[b, h] = s * chunk_decay[h] + _dot_tn(jnp.where(own, k_dec, 0.0).astype(BF16), v)
        y_ref[:, :, POOL_WIDTH + h * RET_DV:POOL_WIDTH + (h + 1) * RET_DV] = (
            _retention_head_out(o, gate).reshape(DEC_SEQ, SAMPLE_BB, RET_DV))


def _even_sample(z, w_pool, s_pool, state_pool, state_ret, layer, prev_hist, prev_state):
    pos = PAST_LEN + np.repeat(np.arange(DEC_SEQ), SAMPLE_BB)
    cos, sin = _rotary_tables(pos)
    mask, qd, kd, cd = _retention_tables(DEC_SEQ)
    mask = np.stack([np.kron(m, np.eye(SAMPLE_BB)) for m in mask])
    qd = _lane_bcast(np.repeat(qd, SAMPLE_BB, axis=1), RET_DK)
    kd = _lane_bcast(np.repeat(kd, SAMPLE_BB, axis=1), RET_DK)
    const = lambda *shape: pl.BlockSpec(shape, lambda i: (0,) * len(shape))
    return _call_stacked(
        functools.partial(_even_sample_kernel, chunk_decay=tuple(float(c) for c in cd)),
        name="even_mixer_sample",
        grid=(DEC_BATCH // SAMPLE_BB,),
        in_specs=[
            pl.BlockSpec((DEC_SEQ, SAMPLE_BB, EVEN_IN), lambda i: (0, i, 0)),
            const(SAMPLE_ROWS, RET_DK),
            const(SAMPLE_ROWS, RET_DK),
            const(RET_HEADS, SAMPLE_ROWS, SAMPLE_ROWS),
            const(RET_HEADS, SAMPLE_ROWS, RET_DK),
            const(RET_HEADS, SAMPLE_ROWS, RET_DK),
            pl.BlockSpec((None, POOL_GROUPS, POOL_GROUP_DIM, POOL_GROUP_DIM), lambda i: (layer, 0, 0, 0)),
            pl.BlockSpec((None, 1, POOL_WIDTH), lambda i: (layer, 0, 0)),
            pl.BlockSpec((None, POOL_HIST, SAMPLE_BB, POOL_WIDTH), lambda i: (layer, 0, i, 0)),
            pl.BlockSpec((None, SAMPLE_BB, RET_HEADS, RET_DK, RET_DV), lambda i: (layer, i, 0, 0, 0)),
        ],
        args=(z, cos, sin, jnp.asarray(mask, F32), jnp.asarray(qd, F32), jnp.asarray(kd, F32),
              w_pool, s_pool.reshape(N_EVEN, 1, POOL_WIDTH), state_pool, state_ret),
        out_specs=[
            pl.BlockSpec((DEC_SEQ, SAMPLE_BB, D_MODEL), lambda i: (0, i, 0)),
            pl.BlockSpec((None, POOL_HIST, SAMPLE_BB, POOL_WIDTH), lambda i: (layer, 0, i, 0)),
            pl.BlockSpec((None, SAMPLE_BB, RET_HEADS, RET_DK, RET_DV), lambda i: (layer, i, 0, 0, 0)),
        ],
        out_shape=[
            jax.ShapeDtypeStruct((DEC_SEQ, DEC_BATCH, D_MODEL), F32),
            jax.ShapeDtypeStruct((N_EVEN, POOL_HIST, DEC_BATCH, POOL_WIDTH), F32),
            jax.ShapeDtypeStruct((N_EVEN, DEC_BATCH, RET_HEADS, RET_DK, RET_DV), F32),
        ],
        stacked={1: prev_hist, 2: prev_state},
        sem=("parallel",),
        scratch_shapes=[],
    )


CONV_LANES = 128
SHIFTED_ROWS = CONV_PAD + CHUNK - SUBLANES


def _odd_prompt_kernel(a_ref, w_ref, lng_ref, lnb_ref, sgw_ref, sgb_ref, dw_ref, dwb_ref, cvg_ref, cvb_ref,
                       y_ref, cst_ref, z0_ref, z1_ref, vb_ref, ext_ref, cv_ref, xs_ref):
    step = pl.program_id(0)
    n = (step + N_CHUNKS - 1) % N_CHUNKS

    @pl.when(jnp.logical_or(n == 0, step == 0))
    def _():
        ext_ref[0:CONV_PAD, :] = jnp.zeros((CONV_PAD, CONV_CH), F32)

    _skewed(a_ref, w_ref, (z0_ref, z1_ref), ODD_IN,
            functools.partial(_odd_prompt_mix, lng_ref, lnb_ref, sgw_ref, sgb_ref, dw_ref, dwb_ref, cvg_ref, cvb_ref,
                              y_ref, vb_ref, ext_ref, cv_ref, xs_ref))

    @pl.when(jnp.logical_and(n == N_CHUNKS - 1, step > 0))
    def _():
        cst_ref[0] = ext_ref[CONV_PAD + CHUNK - CONV_HIST:CONV_PAD + CHUNK, :]

    ext_ref[0:CONV_PAD, :] = ext_ref[CHUNK:CHUNK + CONV_PAD, :]


def _odd_prompt_mix(lng_ref, lnb_ref, sgw_ref, sgb_ref, dw_ref, dwb_ref, cvg_ref, cvb_ref,
                    y_ref, vb_ref, ext_ref, cv_ref, xs_ref, step, n, zc, project):
    def gate_values():
        v = _layer_norm(jax.nn.gelu(zc[:, SG_WIDTH:2 * SG_WIDTH]), lng_ref[...], lnb_ref[...])
        vb_ref[...] = v.astype(BF16)

    def gating(g):
        lanes = slice(g * SG_GROUP_DIM, (g + 1) * SG_GROUP_DIM)
        row = lax.broadcasted_iota(jnp.int32, (CHUNK, CHUNK), 0)
        col = lax.broadcasted_iota(jnp.int32, (CHUNK, CHUNK), 1)
        ws = jnp.where(col <= row, sgw_ref[g], 0.0).astype(BF16)
        mixed = _dot(ws, vb_ref[:, lanes]) + sgb_ref[:, g:g + 1]
        y_ref[:, lanes] = (jax.nn.gelu(zc[:, lanes]) * mixed).astype(BF16)

    def glu():
        a = zc[:, 2 * SG_WIDTH:2 * SG_WIDTH + CONV_CH]
        gate = zc[:, 2 * SG_WIDTH + CONV_CH:2 * SG_WIDTH + 2 * CONV_CH]
        ext_ref[CONV_PAD:CONV_PAD + CHUNK, :] = a * jax.nn.sigmoid(gate)

    def shifted_copy(s):
        xs_ref[s - 1] = ext_ref[s:s + SHIFTED_ROWS, :]

    def conv(c):
        lanes = slice(c, c + CONV_LANES)
        acc = jnp.broadcast_to(dwb_ref[:, lanes], (CHUNK, CONV_LANES))
        for j in range(CONV_K):
            tile, s = divmod(CONV_PAD - CONV_HIST + j, SUBLANES)
            rows = slice(tile * SUBLANES, tile * SUBLANES + CHUNK)
            window = ext_ref[rows, lanes] if s == 0 else xs_ref[s - 1, rows, lanes]
            acc = acc + window * dw_ref[j:j + 1, lanes]
        cv_ref[:, lanes] = acc

    def conv_out():
        yd = _layer_norm(cv_ref[...], cvg_ref[...], cvb_ref[...])
        y_ref[:, SG_WIDTH:SG_WIDTH + CONV_CH] = (yd * jax.nn.sigmoid(yd)).astype(BF16)

    _interleave(project, [gate_values, glu] + [functools.partial(shifted_copy, s) for s in range(1, SUBLANES)]
                + [functools.partial(conv, c) for c in range(0, CONV_CH, CONV_LANES)] + [conv_out])
    for g in range(SG_GROUPS):
        gating(g)


def _odd_weight_specs(layer):
    per_layer = lambda *shape: pl.BlockSpec((None,) + shape, lambda *_: (layer,) + (0,) * len(shape))
    return dict(
        ln=per_layer(1, SG_WIDTH),
        sgw=per_layer(SG_GROUPS, SG_CHUNK, SG_CHUNK),
        sgb=per_layer(SG_CHUNK, SG_GROUPS),
        dw=per_layer(CONV_K, CONV_CH),
        ch=per_layer(1, CONV_CH),
    )


def _odd_prompt(a, w_in, sg_ln_g, sg_ln_b, sg_w, sg_b, dw_w, dw_b, cv_ln_g, cv_ln_b, layer, prev_cst):
    spec = _odd_weight_specs(layer)
    row = lambda x: x.reshape(N_ODD, 1, -1)
    return _call_stacked(
        _odd_prompt_kernel,
        name="odd_mixer_prompt",
        grid=(N_PROMPT_CHUNKS + 1,),
        in_specs=[
            pl.BlockSpec((CHUNK, D_MODEL), lambda t: (_projected_chunk(t), 0)),
            pl.BlockSpec((None, D_MODEL, ODD_IN), lambda t: (layer, 0, 0), pipeline_mode=pl.Buffered(1)),
            spec["ln"], spec["ln"], spec["sgw"], spec["sgb"], spec["dw"], spec["ch"], spec["ch"], spec["ch"],
        ],
        args=(a, w_in, row(sg_ln_g), row(sg_ln_b), sg_w, jnp.swapaxes(sg_b, 1, 2), dw_w, row(dw_b),
              row(cv_ln_g), row(cv_ln_b)),
        out_specs=[
            pl.BlockSpec((CHUNK, D_MODEL), lambda t: (_mixed_chunk(t), 0)),
            pl.BlockSpec((None, 1, CONV_HIST, CONV_CH), lambda t: (layer, _mixed_chunk(t) // N_CHUNKS, 0, 0)),
        ],
        out_shape=[
            jax.ShapeDtypeStruct((N_PROMPT, D_MODEL), BF16),
            jax.ShapeDtypeStruct((N_ODD, BATCH, CONV_HIST, CONV_CH), F32),
        ],
        stacked={1: prev_cst},
        sem=("arbitrary",),
        scratch_shapes=[
            pltpu.VMEM((CHUNK, ODD_IN), F32),
            pltpu.VMEM((CHUNK, ODD_IN), F32),
            pltpu.VMEM((CHUNK, SG_WIDTH), BF16),
            pltpu.VMEM((CONV_PAD + CHUNK, CONV_CH), F32),
            pltpu.VMEM((CHUNK, CONV_CH), F32),
            pltpu.VMEM((SUBLANES - 1, SHIFTED_ROWS, CONV_CH), F32),
        ],
    )


def _odd_sample_kernel(sgw_ref, sgb_ref, z_ref, lng_ref, lnb_ref, dw_ref, dwb_ref, cvg_ref, cvb_ref, cst_ref,
                       y_ref, sgv_ref, ncst_ref, cv_ref):
    v = _layer_norm(jax.nn.gelu(_position_slabs(z_ref[:, :, SG_WIDTH:2 * SG_WIDTH])), lng_ref[...], lnb_ref[...])
    sgv_ref[...] = v.reshape(DEC_SEQ, SAMPLE_BB, SG_WIDTH)
    for g in range(SG_GROUPS):
        lanes = slice(g * SG_GROUP_DIM, (g + 1) * SG_GROUP_DIM)
        for t in range(DEC_SEQ):
            mixed = jnp.full((SAMPLE_BB, SG_GROUP_DIM), sgb_ref[g * DEC_SEQ + t], F32)
            for j in range(t + 1):
                mixed = mixed + sgw_ref[(g * DEC_SEQ + t) * DEC_SEQ + j] * sgv_ref[j, :, lanes]
            y_ref[t, :, lanes] = jax.nn.gelu(z_ref[t, :, lanes]) * mixed

    def ext(r, lanes):
        if r < CONV_HIST:
            return cst_ref[r, :, lanes]
        a = z_ref[r - CONV_HIST, :, pl.ds(2 * SG_WIDTH + lanes.start, CONV_LANES)]
        gate = z_ref[r - CONV_HIST, :, pl.ds(2 * SG_WIDTH + CONV_CH + lanes.start, CONV_LANES)]
        return a * jax.nn.sigmoid(gate)

    for c in range(0, CONV_CH, CONV_LANES):
        lanes = slice(c, c + CONV_LANES)
        rows = [ext(r, lanes) for r in range(CONV_HIST + DEC_SEQ)]
        for t in range(DEC_SEQ):
            acc = jnp.broadcast_to(dwb_ref[:, lanes], (SAMPLE_BB, CONV_LANES))
            for j in range(CONV_K):
                acc = acc + rows[t + j] * dw_ref[j:j + 1, lanes]
            cv_ref[t, :, lanes] = acc
        for r in range(CONV_HIST):
            ncst_ref[r, :, lanes] = rows[r + DEC_SEQ]
    yd = _layer_norm(_position_slabs(cv_ref[...]), cvg_ref[...], cvb_ref[...])
    y_ref[:, :, SG_WIDTH:SG_WIDTH + CONV_CH] = (yd * jax.nn.sigmoid(yd)).reshape(DEC_SEQ, SAMPLE_BB, CONV_CH)


def _odd_sample(z, sg_ln_g, sg_ln_b, sg_w, sg_b, dw_w, dw_b, cv_ln_g, cv_ln_b, state_conv, layer,
                prev_sgv, prev_cst):
    spec = _odd_weight_specs(layer)
    row = lambda x: x.reshape(N_ODD, 1, -1)
    smem = pl.BlockSpec(memory_space=pltpu.SMEM)
    return _call_stacked(
        _odd_sample_kernel,
        name="odd_mixer_sample",
        grid=(DEC_BATCH // SAMPLE_BB,),
        in_specs=[
            smem,
            smem,
            pl.BlockSpec((DEC_SEQ, SAMPLE_BB, ODD_IN), lambda i: (0, i, 0)),
            spec["ln"], spec["ln"], spec["dw"], spec["ch"], spec["ch"], spec["ch"],
            pl.BlockSpec((None, CONV_HIST, SAMPLE_BB, CONV_CH), lambda i: (layer, 0, i, 0)),
        ],
        args=(sg_w[layer, :, :DEC_SEQ, :DEC_SEQ].reshape(-1), sg_b[layer, :, :DEC_SEQ].reshape(-1), z,
              row(sg_ln_g), row(sg_ln_b), dw_w, row(dw_b), row(cv_ln_g), row(cv_ln_b), state_conv),
        out_specs=[
            pl.BlockSpec((DEC_SEQ, SAMPLE_BB, D_MODEL), lambda i: (0, i, 0)),
            pl.BlockSpec((None, DEC_SEQ, SAMPLE_BB, SG_WIDTH), lambda i: (layer, 0, i, 0)),
            pl.BlockSpec((None, CONV_HIST, SAMPLE_BB, CONV_CH), lambda i: (layer, 0, i, 0)),
        ],
        out_shape=[
            jax.ShapeDtypeStruct((DEC_SEQ, DEC_BATCH, D_MODEL), F32),
            jax.ShapeDtypeStruct((N_ODD, DEC_SEQ, DEC_BATCH, SG_WIDTH), F32),
            jax.ShapeDtypeStruct((N_ODD, CONV_HIST, DEC_BATCH, CONV_CH), F32),
        ],
        stacked={1: prev_sgv, 2: prev_cst},
        sem=("parallel",),
        scratch_shapes=[pltpu.VMEM((DEC_SEQ, SAMPLE_BB, CONV_CH), F32)],
    )


def _position_major(x):
    return jnp.swapaxes(x, -3, -2)


def kernel(x_prompt, x_sample, state_pool, state_ret, state_conv, norm_mix_pre, norm_mix_post, norm_ffn_pre, norm_ffn_post, w_in_even, w_pool, s_pool, w_out_even, w_in_odd, sg_ln_g, sg_ln_b, sg_w, sg_b, dw_w, dw_b, cv_ln_g, cv_ln_b, w_out_odd, w_up, w_down):
    w_in_even, w_out_even, w_in_odd, w_out_odd, w_up, w_down = (
        w.astype(BF16) for w in (w_in_even, w_out_even, w_in_odd, w_out_odd, w_up, w_down))

    state_pool = _position_major(state_pool)
    state_conv = _position_major(state_conv)
    h, a = _prenorm(x_prompt.reshape(N_PROMPT, D_MODEL), _position_major(x_sample).reshape(N_SAMPLE, D_MODEL),
                    norm_mix_pre[0])
    pool_p = pool_s = ret_p = ret_s = conv_p = conv_s = sgv_s = None
    for l in range(DEPTH):
        i = l // 2
        if l % 2 == 0:
            y_p, pool_p, ret_p = _even_prompt(a, w_in_even, w_pool, s_pool, i, pool_p, ret_p)
            z_s = _in_proj_sample(a, w_in_even, i).reshape(DEC_SEQ, DEC_BATCH, EVEN_IN)
            y_s, pool_s, ret_s = _even_sample(z_s, w_pool, s_pool, state_pool, state_ret, i, pool_s, ret_s)
            w_out = w_out_even
        else:
            y_p, conv_p = _odd_prompt(a, w_in_odd, sg_ln_g, sg_ln_b, sg_w, sg_b, dw_w, dw_b, cv_ln_g, cv_ln_b,
                                      i, conv_p)
            z_s = _in_proj_sample(a, w_in_odd, i).reshape(DEC_SEQ, DEC_BATCH, ODD_IN)
            y_s, sgv_s, conv_s = _odd_sample(z_s, sg_ln_g, sg_ln_b, sg_w, sg_b, dw_w, dw_b, cv_ln_g, cv_ln_b,
                                             state_conv, i, sgv_s, conv_s)
            w_out = w_out_odd
        h, f = _out_proj(y_p, y_s.reshape(N_SAMPLE, D_MODEL), w_out, i, norm_mix_post[l], norm_ffn_pre[l], h)
        if l + 1 < DEPTH:
            h, a = _ffn(f, h, w_up, w_down, l, norm_ffn_post[l], norm_mix_pre[l + 1])
        else:
            y_prompt, y_sample = _ffn_final(f, h, w_up, w_down, l, norm_ffn_post[l])

    y_sample = _position_major(y_sample.reshape(DEC_SEQ, DEC_BATCH, D_MODEL))
    return (y_prompt.reshape(BATCH, SEQ, D_MODEL), y_sample, pool_p, _position_major(pool_s), ret_p, ret_s,
            conv_p, _position_major(conv_s), _position_major(sgv_s))
```

```python
import functools

import jax
import jax.numpy as jnp
import numpy as np
from jax import lax
from jax.experimental import pallas as pl
from jax.experimental.pallas import tpu as pltpu

F32 = jnp.float32
BF16 = jnp.bfloat16

D_MODEL = 2048
BATCH = 4
SEQ = 2048
DEPTH = 4
DEC_BATCH = 128
DEC_SEQ = 4
PAST_LEN = 16384

N_EVEN = (DEPTH + 1) // 2
N_ODD = DEPTH // 2

POOL_WINDOWS = (2, 4, 8, 16)
POOL_GROUPS = len(POOL_WINDOWS)
POOL_WIDTH = D_MODEL // 4
POOL_GROUP_DIM = POOL_WIDTH // POOL_GROUPS
POOL_HIST = max(POOL_WINDOWS) - 1
RET_WIDTH = D_MODEL - POOL_WIDTH
RET_HEADS = 6
RET_DV = RET_WIDTH // RET_HEADS
RET_DK = RET_DV // 2
RET_QK = RET_HEADS * RET_DK
RET_CHUNK = 128
ROPE_BASE = 10000.0
SG_WIDTH = D_MODEL // 2
SG_CHUNK = 128
SG_GROUPS = 4
SG_GROUP_DIM = SG_WIDTH // SG_GROUPS
CONV_CH = D_MODEL // 2
CONV_K = 31
CONV_HIST = CONV_K - 1
D_FF = 4 * D_MODEL
EPS = 1e-6

EVEN_IN = POOL_WIDTH + 2 * RET_QK + 2 * RET_WIDTH
ODD_IN = 2 * SG_WIDTH + 2 * CONV_CH

Q_OFF = POOL_WIDTH
K_OFF = Q_OFF + RET_QK
V_OFF = K_OFF + RET_QK
G_OFF = V_OFF + RET_WIDTH

N_PROMPT = BATCH * SEQ
N_SAMPLE = DEC_BATCH * DEC_SEQ
N_TOK = N_PROMPT + N_SAMPLE

SUBLANES = 8
CHUNK = 128
N_CHUNKS = SEQ // CHUNK
SAMPLE_BB = SUBLANES
SAMPLE_ROWS = SAMPLE_BB * DEC_SEQ
POOL_PAD = 16
CONV_PAD = 32

TN_IN = 1024
TM_OUT = 512
SLAB = 128
TF = 1024
VMEM_LIMIT = 56 * 1024 * 1024

N_PROMPT_TILES = N_PROMPT // TM_OUT
assert N_PROMPT % TM_OUT == 0 and N_SAMPLE == TM_OUT and TM_OUT % SLAB == 0
assert POOL_PAD >= POOL_HIST and CONV_PAD >= CONV_HIST and DEC_BATCH % SAMPLE_BB == 0


def _params(*sem):
    return pltpu.CompilerParams(dimension_semantics=sem, vmem_limit_bytes=VMEM_LIMIT)


def _rms_scale(x, g):
    return x * lax.rsqrt(jnp.mean(x * x, axis=-1, keepdims=True) + EPS) * g


def _layer_norm(x, g, b):
    xc = x - jnp.mean(x, axis=-1, keepdims=True)
    return xc * lax.rsqrt(jnp.mean(xc * xc, axis=-1, keepdims=True) + EPS) * g + b


def _dot(a, b):
    return jnp.dot(a, b, preferred_element_type=F32)


def _dot_nt(a, b):
    return lax.dot_general(a, b, (((1,), (1,)), ((), ())), preferred_element_type=F32)


def _dot_tn(a, b):
    return lax.dot_general(a, b, (((0,), (0,)), ((), ())), preferred_element_type=F32)


def _skip_aliased(body, n_in, n_aliased):
    def wrapped(*refs):
        return body(*refs[:n_in], *refs[n_in + n_aliased:])
    return wrapped


def _call_stacked(body, *, name, grid, in_specs, args, out_specs, out_shape, stacked, sem, scratch_shapes):
    prev = [(o, p) for o, p in sorted(stacked.items()) if p is not None]
    n_in = len(args)
    return pl.pallas_call(
        _skip_aliased(body, n_in, len(prev)),
        grid=grid,
        in_specs=list(in_specs) + [pl.BlockSpec(memory_space=pl.ANY)] * len(prev),
        out_specs=out_specs,
        out_shape=out_shape,
        input_output_aliases={n_in + j: o for j, (o, _) in enumerate(prev)},
        scratch_shapes=scratch_shapes,
        compiler_params=_params(*sem),
        name=name,
    )(*args, *[p for _, p in prev])


def _is_prompt_tile():
    return pl.program_id(0) < N_PROMPT_TILES


def _prenorm_kernel(xp_ref, xs_ref, g_ref, h_ref, a_ref):
    def emit(x_ref):
        x = x_ref[...]
        h_ref[...] = x
        a_ref[...] = _rms_scale(x, g_ref[...]).astype(BF16)

    pl.when(_is_prompt_tile())(lambda: emit(xp_ref))
    pl.when(jnp.logical_not(_is_prompt_tile()))(lambda: emit(xs_ref))


def _prenorm(x_prompt, x_sample, g):
    return pl.pallas_call(
        _prenorm_kernel,
        grid=(N_TOK // TM_OUT,),
        in_specs=[
            pl.BlockSpec((TM_OUT, D_MODEL), lambda i: (jnp.minimum(i, N_PROMPT_TILES - 1), 0)),
            pl.BlockSpec((TM_OUT, D_MODEL), lambda i: (0, 0)),
            pl.BlockSpec((1, D_MODEL), lambda i: (0, 0)),
        ],
        out_specs=[
            pl.BlockSpec((TM_OUT, D_MODEL), lambda i: (i, 0)),
            pl.BlockSpec((TM_OUT, D_MODEL), lambda i: (i, 0)),
        ],
        out_shape=[
            jax.ShapeDtypeStruct((N_TOK, D_MODEL), F32),
            jax.ShapeDtypeStruct((N_TOK, D_MODEL), BF16),
        ],
        compiler_params=_params("parallel"),
        name="join_prenorm",
    )(x_prompt, x_sample, g.reshape(1, D_MODEL))


def _in_proj_kernel(a_ref, w_ref, o_ref):
    o_ref[...] = _dot(a_ref[...], w_ref[...])


def _in_proj_sample(a, w, layer):
    n_out = w.shape[2]
    return pl.pallas_call(
        _in_proj_kernel,
        grid=(n_out // TN_IN,),
        in_specs=[
            pl.BlockSpec((N_SAMPLE, D_MODEL), lambda j: (N_PROMPT // N_SAMPLE, 0)),
            pl.BlockSpec((None, D_MODEL, TN_IN), lambda j: (layer, 0, j)),
        ],
        out_specs=pl.BlockSpec((N_SAMPLE, TN_IN), lambda j: (0, j)),
        out_shape=jax.ShapeDtypeStruct((N_SAMPLE, n_out), F32),
        compiler_params=_params("parallel"),
        name="in_proj_sample",
    )(a, w)


def _out_proj_kernel(yp_ref, ys_ref, w_ref, g_ref, gf_ref, h_ref, o_ref, f_ref):
    def finish(y_ref):
        for r in range(0, TM_OUT, SLAB):
            rows = slice(r, r + SLAB)
            y = y_ref[rows, :].astype(BF16)
            hn = h_ref[rows, :] + _rms_scale(_dot(y, w_ref[...]), g_ref[...])
            o_ref[rows, :] = hn
            f_ref[rows, :] = _rms_scale(hn, gf_ref[...]).astype(BF16)

    pl.when(_is_prompt_tile())(lambda: finish(yp_ref))
    pl.when(jnp.logical_not(_is_prompt_tile()))(lambda: finish(ys_ref))


def _out_proj(y_prompt, y_sample, w, layer, g, g_ffn, h):
    return pl.pallas_call(
        _out_proj_kernel,
        grid=(N_TOK // TM_OUT,),
        in_specs=[
            pl.BlockSpec((TM_OUT, D_MODEL), lambda i: (jnp.minimum(i, N_PROMPT_TILES - 1), 0)),
            pl.BlockSpec((TM_OUT, D_MODEL), lambda i: (0, 0)),
            pl.BlockSpec((None, D_MODEL, D_MODEL), lambda i: (layer, 0, 0)),
            pl.BlockSpec((1, D_MODEL), lambda i: (0, 0)),
            pl.BlockSpec((1, D_MODEL), lambda i: (0, 0)),
            pl.BlockSpec((TM_OUT, D_MODEL), lambda i: (i, 0)),
        ],
        out_specs=[
            pl.BlockSpec((TM_OUT, D_MODEL), lambda i: (i, 0)),
            pl.BlockSpec((TM_OUT, D_MODEL), lambda i: (i, 0)),
        ],
        out_shape=[
            jax.ShapeDtypeStruct((N_TOK, D_MODEL), F32),
            jax.ShapeDtypeStruct((N_TOK, D_MODEL), BF16),
        ],
        compiler_params=_params("parallel"),
        name="out_proj_norm_residual",
    )(y_prompt, y_sample, w, g.reshape(1, D_MODEL), g_ffn.reshape(1, D_MODEL), h)


N_FF_CHUNKS = D_FF // TF
HEAD_SPLIT = 2
TF_HEAD = TF // HEAD_SPLIT


def _ffn_accumulate(k, f_ref, wu, wd, acc_ref):
    @pl.when(k == 0)
    def _():
        acc_ref[...] = jnp.zeros_like(acc_ref)

    u = jnp.square(jnp.maximum(_dot(f_ref[...], wu), 0.0)).astype(BF16)
    acc_ref[...] += _dot(u, wd)


def _ffn_epilogue(h_ref, acc_ref, g2_ref, o_ref, gn_ref=None, a_ref=None):
    for r in range(0, TM_OUT, SLAB):
        rows = slice(r, r + SLAB)
        hn = h_ref[rows, :] + _rms_scale(acc_ref[rows, :], g2_ref[...])
        o_ref[rows, :] = hn
        if a_ref is not None:
            a_ref[rows, :] = _rms_scale(hn, gn_ref[...]).astype(BF16)


def _ffn_head_kernel(f_ref, h_ref, wu_ref, wd_ref, g2_ref, *refs, last_layer):
    if last_layer:
        o_ref, wub_ref, wdb_ref, acc_ref = refs
        gn_ref = a_ref = None
    else:
        gn_ref, o_ref, a_ref, wub_ref, wdb_ref, acc_ref = refs
    k = pl.program_id(0)
    wu = wu_ref[...].astype(BF16)
    wd = wd_ref[...].astype(BF16)
    wub_ref[...] = wu
    wdb_ref[...] = wd
    _ffn_accumulate(k, f_ref, wu, wd, acc_ref)
    pl.when(k == pl.num_programs(0) - 1)(
        functools.partial(_ffn_epilogue, h_ref, acc_ref, g2_ref, o_ref, gn_ref, a_ref))


def _ffn_body_kernel(f_ref, h_ref, wu_ref, wd_ref, g2_ref, *refs, last_layer):
    k = pl.program_id(1)
    is_last = k == pl.num_programs(1) - 1
    if last_layer:
        yp_ref, ys_ref, acc_ref = refs
        _ffn_accumulate(k, f_ref, wu_ref[...], wd_ref[...], acc_ref)
        is_prompt = pl.program_id(0) + 1 < N_PROMPT_TILES
        pl.when(jnp.logical_and(is_last, is_prompt))(
            functools.partial(_ffn_epilogue, h_ref, acc_ref, g2_ref, yp_ref))
        pl.when(jnp.logical_and(is_last, jnp.logical_not(is_prompt)))(
            functools.partial(_ffn_epilogue, h_ref, acc_ref, g2_ref, ys_ref))
    else:
        gn_ref, o_ref, a_ref, acc_ref = refs
        _ffn_accumulate(k, f_ref, wu_ref[...], wd_ref[...], acc_ref)
        pl.when(is_last)(functools.partial(_ffn_epilogue, h_ref, acc_ref, g2_ref, o_ref, gn_ref, a_ref))


def _ffn(f, h, w_up, w_down, layer, g2, g_next=None):
    last_layer = g_next is None
    gains = [g2.reshape(1, D_MODEL)] + ([] if last_layer else [g_next.reshape(1, D_MODEL)])
    acc = pltpu.VMEM((TM_OUT, D_MODEL), F32)
    rounded_shapes = [
        jax.ShapeDtypeStruct((N_FF_CHUNKS, D_MODEL, TF), BF16),
        jax.ShapeDtypeStruct((D_FF, D_MODEL), BF16),
    ]
    if last_layer:
        act_shapes = [jax.ShapeDtypeStruct((N_PROMPT, D_MODEL), F32)]
    else:
        act_shapes = [jax.ShapeDtypeStruct((N_TOK, D_MODEL), F32), jax.ShapeDtypeStruct((N_TOK, D_MODEL), BF16)]

    tile0 = lambda **kw: pl.BlockSpec((TM_OUT, D_MODEL), lambda k: (0, 0), **kw)
    head = pl.pallas_call(
        functools.partial(_ffn_head_kernel, last_layer=last_layer),
        grid=(N_FF_CHUNKS * HEAD_SPLIT,),
        in_specs=[
            tile0(pipeline_mode=pl.Buffered(1)), tile0(pipeline_mode=pl.Buffered(1)),
            pl.BlockSpec((None, D_MODEL, TF_HEAD), lambda k: (layer, 0, k)),
            pl.BlockSpec((None, TF_HEAD, D_MODEL), lambda k: (layer, k, 0)),
        ] + [pl.BlockSpec((1, D_MODEL), lambda k: (0, 0))] * len(gains),
        out_specs=[tile0() for _ in act_shapes] + [
            pl.BlockSpec((None, D_MODEL, TF_HEAD), lambda k: (k // HEAD_SPLIT, 0, k % HEAD_SPLIT)),
            pl.BlockSpec((TF_HEAD, D_MODEL), lambda k: (k, 0)),
        ],
        out_shape=act_shapes + rounded_shapes,
        scratch_shapes=[acc],
        compiler_params=_params("arbitrary"),
        name="relu2_mlp_head",
    )(f, h, w_up, w_down, *gains)
    *acts, wu_b, wd_b = head

    tile = lambda: pl.BlockSpec((TM_OUT, D_MODEL), lambda i, k: (i + 1, 0))
    in_specs = [
        tile(), tile(),
        pl.BlockSpec((None, D_MODEL, TF), lambda i, k: (k, 0, 0)),
        pl.BlockSpec((TF, D_MODEL), lambda i, k: (k, 0)),
    ] + [pl.BlockSpec((1, D_MODEL), lambda i, k: (0, 0))] * len(gains)
    if last_layer:
        out_specs = [
            pl.BlockSpec((TM_OUT, D_MODEL), lambda i, k: (jnp.minimum(i + 1, N_PROMPT_TILES - 1), 0)),
            pl.BlockSpec((TM_OUT, D_MODEL), lambda i, k: (0, 0)),
        ]
        out_shape = act_shapes + [jax.ShapeDtypeStruct((N_SAMPLE, D_MODEL), F32)]
    else:
        out_specs = [tile(), tile()]
        out_shape = act_shapes
    n_in = len(in_specs)
    return pl.pallas_call(
        _skip_aliased(functools.partial(_ffn_body_kernel, last_layer=last_layer), n_in, len(acts)),
        grid=(N_TOK // TM_OUT - 1, N_FF_CHUNKS),
        in_specs=in_specs + [pl.BlockSpec(memory_space=pl.ANY)] * len(acts),
        out_specs=out_specs,
        out_shape=out_shape,
        input_output_aliases={n_in + j: j for j in range(len(acts))},
        scratch_shapes=[acc],
        compiler_params=_params("arbitrary", "arbitrary"),
        name="relu2_mlp_body",
    )(f, h, wu_b, wd_b, *gains, *acts)


def _rotary_tables(pos):
    half = RET_DK // 2
    inv = ROPE_BASE ** (-np.arange(half, dtype=np.float64) / half)
    ang = np.asarray(pos, np.float64)[:, None] * inv[None, :]
    cos = np.concatenate([np.cos(ang), np.cos(ang)], axis=-1)
    sin = np.concatenate([-np.sin(ang), np.sin(ang)], axis=-1)
    return jnp.asarray(cos, F32), jnp.asarray(sin, F32)


def _log_gamma():
    return np.log1p(-np.exp2(-5.0 - np.arange(RET_HEADS, dtype=np.float64)))


def _retention_tables(length):
    log_g = _log_gamma()
    idx = np.arange(length, dtype=np.float64)
    diff = idx[:, None] - idx[None, :]
    mask = np.where(diff[None] >= 0, np.exp(log_g[:, None, None] * np.maximum(diff, 0.0)[None]), 0.0)
    qd = np.exp(log_g[:, None] * (idx + 1.0))
    kd = np.exp(log_g[:, None] * (length - 1.0 - idx))
    cd = np.exp(log_g * length)
    return mask, qd, kd, cd


def _lane_bcast(a, width):
    return np.repeat(a[..., None], width, axis=-1)


N_PROMPT_CHUNKS = N_PROMPT // CHUNK
PROJ_COLS = 512


def _projected_chunk(step):
    return jnp.minimum(step, N_PROMPT_CHUNKS - 1)


def _mixed_chunk(step):
    return jnp.maximum(step - 1, 0)


def _interleave(first, second):
    i = j = 0
    while i < len(first) or j < len(second):
        if j >= len(second) or (i < len(first) and i * len(second) <= j * len(first)):
            first[i]()
            i += 1
        else:
            second[j]()
            j += 1


def _skewed(a_ref, w_ref, z_refs, n_cols, mix):
    step = pl.program_id(0)
    n = (step + N_CHUNKS - 1) % N_CHUNKS

    @pl.when(step == 0)
    def _():
        z_refs[1][...] = jnp.zeros_like(z_refs[1])

    for parity in range(2):
        z_next, z_mixed = z_refs[parity], z_refs[1 - parity]

        def body(z_next=z_next, z_mixed=z_mixed):
            a = a_ref[...]

            def project(c):
                z_next[:, c:c + PROJ_COLS] = _dot(a, w_ref[:, c:c + PROJ_COLS])

            mix(step, n, z_mixed, [functools.partial(project, c) for c in range(0, n_cols, PROJ_COLS)])

        pl.when(step % 2 == parity)(body)


def _retention_head_out(o, gate):
    o = o * lax.rsqrt(jnp.mean(o * o, axis=-1, keepdims=True) + EPS)
    return gate * jax.nn.sigmoid(gate) * o


def _rotate(x, cos, sin):
    return x * cos + pltpu.roll(x, RET_DK // 2, 1) * sin


def _even_prompt_kernel(a_ref, w_ref, cos_ref, sin_ref, mask_ref, qd_ref, kd_ref, wp_ref, sp_ref,
                        y_ref, hist_ref, st_ref, z0_ref, z1_ref, pext_ref, s_ref, *, chunk_decay):
    step = pl.program_id(0)
    n = (step + N_CHUNKS - 1) % N_CHUNKS

    @pl.when(jnp.logical_or(n == 0, step == 0))
    def _():
        pext_ref[0:POOL_PAD, :] = jnp.zeros((POOL_PAD, POOL_WIDTH), F32)
        s_ref[...] = jnp.zeros_like(s_ref)

    _skewed(a_ref, w_ref, (z0_ref, z1_ref), EVEN_IN,
            functools.partial(_even_prompt_mix, cos_ref, sin_ref, mask_ref, qd_ref, kd_ref, wp_ref, sp_ref, y_ref,
                              pext_ref, s_ref, chunk_decay))

    @pl.when(jnp.logical_and(n == N_CHUNKS - 1, step > 0))
    def _():
        hist_ref[0] = pext_ref[POOL_PAD + CHUNK - POOL_HIST:POOL_PAD + CHUNK, :]
        st_ref[0] = s_ref[...]


def _even_prompt_mix(cos_ref, sin_ref, mask_ref, qd_ref, kd_ref, wp_ref, sp_ref, y_ref, pext_ref, s_ref,
                     chunk_decay, step, n, zc, project):
    def pool():
        pext_ref[POOL_PAD:POOL_PAD + CHUNK, :] = zc[:, 0:POOL_WIDTH]
        pos = n * CHUNK + lax.broadcasted_iota(jnp.int32, (CHUNK, 1), 0)
        for g, w in enumerate(POOL_WINDOWS):
            lanes = slice(g * POOL_GROUP_DIM, (g + 1) * POOL_GROUP_DIM)
            p = pext_ref[POOL_PAD:POOL_PAD + CHUNK, lanes]
            acc = p
            for i in range(1, w):
                acc = acc + pext_ref[POOL_PAD - i:POOL_PAD - i + CHUNK, lanes]
            cnt = jnp.minimum(w, pos + 1).astype(F32)
            d = acc / cnt - p
            yg = _dot(d.astype(BF16), wp_ref[g].astype(BF16)) * sp_ref[:, lanes]
            y_ref[:, lanes] = yg.astype(BF16)
        pext_ref[0:POOL_PAD, :] = pext_ref[CHUNK:CHUNK + POOL_PAD, :]

    live = {}

    def scores_stage(h):
        cos = cos_ref[...]
        sin = sin_ref[...]
        q = _rotate(zc[:, Q_OFF + h * RET_DK:Q_OFF + (h + 1) * RET_DK], cos, sin)
        k = _rotate(zc[:, K_OFF + h * RET_DK:K_OFF + (h + 1) * RET_DK], cos, sin) * (RET_DK ** -0.5)
        v = zc[:, V_OFF + h * RET_DV:V_OFF + (h + 1) * RET_DV].astype(BF16)
        live[h] = (_dot_nt(q.astype(BF16), k.astype(BF16)), (q * qd_ref[h]).astype(BF16),
                   (k * kd_ref[h]).astype(BF16), v)

    def output_stage(h):
        scores, q_dec, k_dec, v = live[h]
        s = s_ref[h]
        o = _dot((scores * mask_ref[h]).astype(BF16), v) + _dot(q_dec, s.astype(BF16))
        s_ref[h] = s * chunk_decay[h] + _dot_tn(k_dec, v)
        live[h] = o

    def norm_stage(h):
        gate = zc[:, G_OFF + h * RET_DV:G_OFF + (h + 1) * RET_DV]
        y_ref[:, POOL_WIDTH + h * RET_DV:POOL_WIDTH + (h + 1) * RET_DV] = (
            _retention_head_out(live.pop(h), gate).astype(BF16))

    def heads_step(i):
        for stage, h in ((norm_stage, i - 2), (output_stage, i - 1), (scores_stage, i)):
            if 0 <= h < RET_HEADS:
                stage(h)

    _interleave(project, [pool] + [functools.partial(heads_step, i) for i in range(RET_HEADS + 2)])


def _even_prompt(a, w_in, w_pool, s_pool, layer, prev_hist, prev_state):
    cos, sin = _rotary_tables(np.arange(SEQ))
    mask, qd, kd, cd = _retention_tables(CHUNK)
    const = lambda *shape: pl.BlockSpec(shape, lambda t: (0,) * len(shape))
    return _call_stacked(
        functools.partial(_even_prompt_kernel, chunk_decay=tuple(float(c) for c in cd)),
        name="even_mixer_prompt",
        grid=(N_PROMPT_CHUNKS + 1,),
        in_specs=[
            pl.BlockSpec((CHUNK, D_MODEL), lambda t: (_projected_chunk(t), 0)),
            pl.BlockSpec((None, D_MODEL, EVEN_IN), lambda t: (layer, 0, 0), pipeline_mode=pl.Buffered(1)),
            pl.BlockSpec((CHUNK, RET_DK), lambda t: (_mixed_chunk(t) % N_CHUNKS, 0)),
            pl.BlockSpec((CHUNK, RET_DK), lambda t: (_mixed_chunk(t) % N_CHUNKS, 0)),
            const(RET_HEADS, CHUNK, CHUNK),
            const(RET_HEADS, CHUNK, RET_DK),
            const(RET_HEADS, CHUNK, RET_DK),
            pl.BlockSpec((None, POOL_GROUPS, POOL_GROUP_DIM, POOL_GROUP_DIM), lambda t: (layer, 0, 0, 0)),
            pl.BlockSpec((None, 1, POOL_WIDTH), lambda t: (layer, 0, 0)),
        ],
        args=(a, w_in, cos, sin, jnp.asarray(mask, F32), jnp.asarray(_lane_bcast(qd, RET_DK), F32),
              jnp.asarray(_lane_bcast(kd, RET_DK), F32), w_pool, s_pool.reshape(N_EVEN, 1, POOL_WIDTH)),
        out_specs=[
            pl.BlockSpec((CHUNK, D_MODEL), lambda t: (_mixed_chunk(t), 0)),
            pl.BlockSpec((None, 1, POOL_HIST, POOL_WIDTH), lambda t: (layer, _mixed_chunk(t) // N_CHUNKS, 0, 0)),
            pl.BlockSpec((None, 1, RET_HEADS, RET_DK, RET_DV),
                         lambda t: (layer, _mixed_chunk(t) // N_CHUNKS, 0, 0, 0)),
        ],
        out_shape=[
            jax.ShapeDtypeStruct((N_PROMPT, D_MODEL), BF16),
            jax.ShapeDtypeStruct((N_EVEN, BATCH, POOL_HIST, POOL_WIDTH), F32),
            jax.ShapeDtypeStruct((N_EVEN, BATCH, RET_HEADS, RET_DK, RET_DV), F32),
        ],
        stacked={1: prev_hist, 2: prev_state},
        sem=("arbitrary",),
        scratch_shapes=[
            pltpu.VMEM((CHUNK, EVEN_IN), F32),
            pltpu.VMEM((CHUNK, EVEN_IN), F32),
            pltpu.VMEM((POOL_PAD + CHUNK, POOL_WIDTH), F32),
            pltpu.VMEM((RET_HEADS, RET_DK, RET_DV), F32),
        ],
    )


def _position_slabs(x):
    return x.reshape(SAMPLE_ROWS, x.shape[-1])


def _even_sample_kernel(z_ref, cos_ref, sin_ref, mask_ref, qd_ref, kd_ref, wp_ref, sp_ref, hist_ref, st_ref,
                        y_ref, nhist_ref, nst_ref, *, chunk_decay):
    def ext(r, lanes):
        return hist_ref[r, :, lanes] if r < POOL_HIST else z_ref[r - POOL_HIST, :, lanes]

    for g, w in enumerate(POOL_WINDOWS):
        lanes = slice(g * POOL_GROUP_DIM, (g + 1) * POOL_GROUP_DIM)
        d = []
        for t in range(DEC_SEQ):
            p = ext(POOL_HIST + t, lanes)
            acc = p
            for i in range(1, w):
                acc = acc + ext(POOL_HIST + t - i, lanes)
            d.append(acc / float(min(w, PAST_LEN + t + 1)) - p)
        d = jnp.concatenate(d, axis=0)
        yg = _dot(d.astype(BF16), wp_ref[g].astype(BF16)) * sp_ref[:, lanes]
        y_ref[:, :, lanes] = yg.reshape(DEC_SEQ, SAMPLE_BB, POOL_GROUP_DIM)
    for r in range(POOL_HIST):
        nhist_ref[r] = hist_ref[r + DEC_SEQ] if r + DEC_SEQ < POOL_HIST else z_ref[r + DEC_SEQ - POOL_HIST, :,
                                                                                 0:POOL_WIDTH]

    cos = cos_ref[...]
    sin = sin_ref[...]
    seq_of_row = lax.broadcasted_iota(jnp.int32, (SAMPLE_ROWS, 1), 0) % SAMPLE_BB
    for h in range(RET_HEADS):
        q = _rotate(_position_slabs(z_ref[:, :, Q_OFF + h * RET_DK:Q_OFF + (h + 1) * RET_DK]), cos, sin)
        k = _rotate(_position_slabs(z_ref[:, :, K_OFF + h * RET_DK:K_OFF + (h + 1) * RET_DK]), cos, sin)
        k = k * (RET_DK ** -0.5)
        v = _position_slabs(z_ref[:, :, V_OFF + h * RET_DV:V_OFF + (h + 1) * RET_DV]).astype(BF16)
        gate = _position_slabs(z_ref[:, :, G_OFF + h * RET_DV:G_OFF + (h + 1) * RET_DV])
        scores = _dot_nt(q.astype(BF16), k.astype(BF16)) * mask_ref[h]
        o = _dot(scores.astype(BF16), v)
        q_dec = (q * qd_ref[h]).astype(BF16)
        k_dec = k * kd_ref[h]
        for b in range(SAMPLE_BB):
            own = seq_of_row == b
            s = st_ref[b, h]
            o = o + jnp.where(own, _dot(q_dec, s.astype(BF16)), 0.0)
            nst_ref[b, h] = s * chunk_decay[h] + _dot_tn(jnp.where(own, k_dec, 0.0).astype(BF16), v)
        y_ref[:, :, POOL_WIDTH + h * RET_DV:POOL_WIDTH + (h + 1) * RET_DV] = (
            _retention_head_out(o, gate).reshape(DEC_SEQ, SAMPLE_BB, RET_DV))


def _even_sample(z, w_pool, s_pool, state_pool, state_ret, layer, prev_hist, prev_state):
    pos = PAST_LEN + np.repeat(np.arange(DEC_SEQ), SAMPLE_BB)
    cos, sin = _rotary_tables(pos)
    mask, qd, kd, cd = _retention_tables(DEC_SEQ)
    mask = np.stack([np.kron(m, np.eye(SAMPLE_BB)) for m in mask])
    qd = _lane_bcast(np.repeat(qd, SAMPLE_BB, axis=1), RET_DK)
    kd = _lane_bcast(np.repeat(kd, SAMPLE_BB, axis=1), RET_DK)
    const = lambda *shape: pl.BlockSpec(shape, lambda i: (0,) * len(shape))
    return _call_stacked(
        functools.partial(_even_sample_kernel, chunk_decay=tuple(float(c) for c in cd)),
        name="even_mixer_sample",
        grid=(DEC_BATCH // SAMPLE_BB,),
        in_specs=[
            pl.BlockSpec((DEC_SEQ, SAMPLE_BB, EVEN_IN), lambda i: (0, i, 0)),
            const(SAMPLE_ROWS, RET_DK),
            const(SAMPLE_ROWS, RET_DK),
            const(RET_HEADS, SAMPLE_ROWS, SAMPLE_ROWS),
            const(RET_HEADS, SAMPLE_ROWS, RET_DK),
            const(RET_HEADS, SAMPLE_ROWS, RET_DK),
            pl.BlockSpec((None, POOL_GROUPS, POOL_GROUP_DIM, POOL_GROUP_DIM), lambda i: (layer, 0, 0, 0)),
            pl.BlockSpec((None, 1, POOL_WIDTH), lambda i: (layer, 0, 0)),
            pl.BlockSpec((None, POOL_HIST, SAMPLE_BB, POOL_WIDTH), lambda i: (layer, 0, i, 0)),
            pl.BlockSpec((None, SAMPLE_BB, RET_HEADS, RET_DK, RET_DV), lambda i: (layer, i, 0, 0, 0)),
        ],
        args=(z, cos, sin, jnp.asarray(mask, F32), jnp.asarray(qd, F32), jnp.asarray(kd, F32),
              w_pool, s_pool.reshape(N_EVEN, 1, POOL_WIDTH), state_pool, state_ret),
        out_specs=[
            pl.BlockSpec((DEC_SEQ, SAMPLE_BB, D_MODEL), lambda i: (0, i, 0)),
            pl.BlockSpec((None, POOL_HIST, SAMPLE_BB, POOL_WIDTH), lambda i: (layer, 0, i, 0)),
            pl.BlockSpec((None, SAMPLE_BB, RET_HEADS, RET_DK, RET_DV), lambda i: (layer, i, 0, 0, 0)),
        ],
        out_shape=[
            jax.ShapeDtypeStruct((DEC_SEQ, DEC_BATCH, D_MODEL), F32),
            jax.ShapeDtypeStruct((N_EVEN, POOL_HIST, DEC_BATCH, POOL_WIDTH), F32),
            jax.ShapeDtypeStruct((N_EVEN, DEC_BATCH, RET_HEADS, RET_DK, RET_DV), F32),
        ],
        stacked={1: prev_hist, 2: prev_state},
        sem=("parallel",),
        scratch_shapes=[],
    )


CONV_LANES = 128
SHIFTED_ROWS = CONV_PAD + CHUNK - SUBLANES


def _odd_prompt_kernel(a_ref, w_ref, lng_ref, lnb_ref, sgw_ref, sgb_ref, dw_ref, dwb_ref, cvg_ref, cvb_ref,
                       y_ref, cst_ref, z0_ref, z1_ref, vb_ref, ext_ref, cv_ref, xs_ref):
    step = pl.program_id(0)
    n = (step + N_CHUNKS - 1) % N_CHUNKS

    @pl.when(jnp.logical_or(n == 0, step == 0))
    def _():
        ext_ref[0:CONV_PAD, :] = jnp.zeros((CONV_PAD, CONV_CH), F32)

    _skewed(a_ref, w_ref, (z0_ref, z1_ref), ODD_IN,
            functools.partial(_odd_prompt_mix, lng_ref, lnb_ref, sgw_ref, sgb_ref, dw_ref, dwb_ref, cvg_ref, cvb_ref,
                              y_ref, vb_ref, ext_ref, cv_ref, xs_ref))

    @pl.when(jnp.logical_and(n == N_CHUNKS - 1, step > 0))
    def _():
        cst_ref[0] = ext_ref[CONV_PAD + CHUNK - CONV_HIST:CONV_PAD + CHUNK, :]

    ext_ref[0:CONV_PAD, :] = ext_ref[CHUNK:CHUNK + CONV_PAD, :]


def _odd_prompt_mix(lng_ref, lnb_ref, sgw_ref, sgb_ref, dw_ref, dwb_ref, cvg_ref, cvb_ref,
                    y_ref, vb_ref, ext_ref, cv_ref, xs_ref, step, n, zc, project):
    def gate_values():
        v = _layer_norm(jax.nn.gelu(zc[:, SG_WIDTH:2 * SG_WIDTH]), lng_ref[...], lnb_ref[...])
        vb_ref[...] = v.astype(BF16)

    def gating(g):
        lanes = slice(g * SG_GROUP_DIM, (g + 1) * SG_GROUP_DIM)
        row = lax.broadcasted_iota(jnp.int32, (CHUNK, CHUNK), 0)
        col = lax.broadcasted_iota(jnp.int32, (CHUNK, CHUNK), 1)
        ws = jnp.where(col <= row, sgw_ref[g], 0.0).astype(BF16)
        mixed = _dot(ws, vb_ref[:, lanes]) + sgb_ref[:, g:g + 1]
        y_ref[:, lanes] = (jax.nn.gelu(zc[:, lanes]) * mixed).astype(BF16)

    def glu():
        a = zc[:, 2 * SG_WIDTH:2 * SG_WIDTH + CONV_CH]
        gate = zc[:, 2 * SG_WIDTH + CONV_CH:2 * SG_WIDTH + 2 * CONV_CH]
        ext_ref[CONV_PAD:CONV_PAD + CHUNK, :] = a * jax.nn.sigmoid(gate)

    def shifted_copy(s):
        xs_ref[s - 1] = ext_ref[s:s + SHIFTED_ROWS, :]

    def conv(c):
        lanes = slice(c, c + CONV_LANES)
        acc = jnp.broadcast_to(dwb_ref[:, lanes], (CHUNK, CONV_LANES))
        for j in range(CONV_K):
            tile, s = divmod(CONV_PAD - CONV_HIST + j, SUBLANES)
            rows = slice(tile * SUBLANES, tile * SUBLANES + CHUNK)
            window = ext_ref[rows, lanes] if s == 0 else xs_ref[s - 1, rows, lanes]
            acc = acc + window * dw_ref[j:j + 1, lanes]
        cv_ref[:, lanes] = acc

    def conv_out():
        yd = _layer_norm(cv_ref[...], cvg_ref[...], cvb_ref[...])
        y_ref[:, SG_WIDTH:SG_WIDTH + CONV_CH] = (yd * jax.nn.sigmoid(yd)).astype(BF16)

    _interleave(project, [gate_values, glu] + [functools.partial(shifted_copy, s) for s in range(1, SUBLANES)]
                + [functools.partial(conv, c) for c in range(0, CONV_CH, CONV_LANES)] + [conv_out])
    for g in range(SG_GROUPS):
        gating(g)


def _odd_weight_specs(layer):
    per_layer = lambda *shape: pl.BlockSpec((None,) + shape, lambda *_: (layer,) + (0,) * len(shape))
    return dict(
        ln=per_layer(1, SG_WIDTH),
        sgw=per_layer(SG_GROUPS, SG_CHUNK, SG_CHUNK),
        sgb=per_layer(SG_CHUNK, SG_GROUPS),
        dw=per_layer(CONV_K, CONV_CH),
        ch=per_layer(1, CONV_CH),
    )


def _odd_prompt(a, w_in, sg_ln_g, sg_ln_b, sg_w, sg_b, dw_w, dw_b, cv_ln_g, cv_ln_b, layer, prev_cst):
    spec = _odd_weight_specs(layer)
    row = lambda x: x.reshape(N_ODD, 1, -1)
    return _call_stacked(
        _odd_prompt_kernel,
        name="odd_mixer_prompt",
        grid=(N_PROMPT_CHUNKS + 1,),
        in_specs=[
            pl.BlockSpec((CHUNK, D_MODEL), lambda t: (_projected_chunk(t), 0)),
            pl.BlockSpec((None, D_MODEL, ODD_IN), lambda t: (layer, 0, 0), pipeline_mode=pl.Buffered(1)),
            spec["ln"], spec["ln"], spec["sgw"], spec["sgb"], spec["dw"], spec["ch"], spec["ch"], spec["ch"],
        ],
        args=(a, w_in, row(sg_ln_g), row(sg_ln_b), sg_w, jnp.swapaxes(sg_b, 1, 2), dw_w, row(dw_b),
              row(cv_ln_g), row(cv_ln_b)),
        out_specs=[
            pl.BlockSpec((CHUNK, D_MODEL), lambda t: (_mixed_chunk(t), 0)),
            pl.BlockSpec((None, 1, CONV_HIST, CONV_CH), lambda t: (layer, _mixed_chunk(t) // N_CHUNKS, 0, 0)),
        ],
        out_shape=[
            jax.ShapeDtypeStruct((N_PROMPT, D_MODEL), BF16),
            jax.ShapeDtypeStruct((N_ODD, BATCH, CONV_HIST, CONV_CH), F32),
        ],
        stacked={1: prev_cst},
        sem=("arbitrary",),
        scratch_shapes=[
            pltpu.VMEM((CHUNK, ODD_IN), F32),
            pltpu.VMEM((CHUNK, ODD_IN), F32),
            pltpu.VMEM((CHUNK, SG_WIDTH), BF16),
            pltpu.VMEM((CONV_PAD + CHUNK, CONV_CH), F32),
            pltpu.VMEM((CHUNK, CONV_CH), F32),
            pltpu.VMEM((SUBLANES - 1, SHIFTED_ROWS, CONV_CH), F32),
        ],
    )


def _odd_sample_kernel(sgw_ref, sgb_ref, z_ref, lng_ref, lnb_ref, dw_ref, dwb_ref, cvg_ref, cvb_ref, cst_ref,
                       y_ref, sgv_ref, ncst_ref, cv_ref):
    v = _layer_norm(jax.nn.gelu(_position_slabs(z_ref[:, :, SG_WIDTH:2 * SG_WIDTH])), lng_ref[...], lnb_ref[...])
    sgv_ref[...] = v.reshape(DEC_SEQ, SAMPLE_BB, SG_WIDTH)
    for g in range(SG_GROUPS):
        lanes = slice(g * SG_GROUP_DIM, (g + 1) * SG_GROUP_DIM)
        for t in range(DEC_SEQ):
            mixed = jnp.full((SAMPLE_BB, SG_GROUP_DIM), sgb_ref[g * DEC_SEQ + t], F32)
            for j in range(t + 1):
                mixed = mixed + sgw_ref[(g * DEC_SEQ + t) * DEC_SEQ + j] * sgv_ref[j, :, lanes]
            y_ref[t, :, lanes] = jax.nn.gelu(z_ref[t, :, lanes]) * mixed

    def ext(r, lanes):
        if r < CONV_HIST:
            return cst_ref[r, :, lanes]
        a = z_ref[r - CONV_HIST, :, pl.ds(2 * SG_WIDTH + lanes.start, CONV_LANES)]
        gate = z_ref[r - CONV_HIST, :, pl.ds(2 * SG_WIDTH + CONV_CH + lanes.start, CONV_LANES)]
        return a * jax.nn.sigmoid(gate)

    for c in range(0, CONV_CH, CONV_LANES):
        lanes = slice(c, c + CONV_LANES)
        rows = [ext(r, lanes) for r in range(CONV_HIST + DEC_SEQ)]
        for t in range(DEC_SEQ):
            acc = jnp.broadcast_to(dwb_ref[:, lanes], (SAMPLE_BB, CONV_LANES))
            for j in range(CONV_K):
                acc = acc + rows[t + j] * dw_ref[j:j + 1, lanes]
            cv_ref[t, :, lanes] = acc
        for r in range(CONV_HIST):
            ncst_ref[r, :, lanes] = rows[r + DEC_SEQ]
    yd = _layer_norm(_position_slabs(cv_ref[...]), cvg_ref[...], cvb_ref[...])
    y_ref[:, :, SG_WIDTH:SG_WIDTH + CONV_CH] = (yd * jax.nn.sigmoid(yd)).reshape(DEC_SEQ, SAMPLE_BB, CONV_CH)


def _odd_sample(z, sg_ln_g, sg_ln_b, sg_w, sg_b, dw_w, dw_b, cv_ln_g, cv_ln_b, state_conv, layer,
                prev_sgv, prev_cst):
    spec = _odd_weight_specs(layer)
    row = lambda x: x.reshape(N_ODD, 1, -1)
    smem = pl.BlockSpec(memory_space=pltpu.SMEM)
    return _call_stacked(
        _odd_sample_kernel,
        name="odd_mixer_sample",
        grid=(DEC_BATCH // SAMPLE_BB,),
        in_specs=[
            smem,
            smem,
            pl.BlockSpec((DEC_SEQ, SAMPLE_BB, ODD_IN), lambda i: (0, i, 0)),
            spec["ln"], spec["ln"], spec["dw"], spec["ch"], spec["ch"], spec["ch"],
            pl.BlockSpec((None, CONV_HIST, SAMPLE_BB, CONV_CH), lambda i: (layer, 0, i, 0)),
        ],
        args=(sg_w[layer, :, :DEC_SEQ, :DEC_SEQ].reshape(-1), sg_b[layer, :, :DEC_SEQ].reshape(-1), z,
              row(sg_ln_g), row(sg_ln_b), dw_w, row(dw_b), row(cv_ln_g), row(cv_ln_b), state_conv),
        out_specs=[
            pl.BlockSpec((DEC_SEQ, SAMPLE_BB, D_MODEL), lambda i: (0, i, 0)),
            pl.BlockSpec((None, DEC_SEQ, SAMPLE_BB, SG_WIDTH), lambda i: (layer, 0, i, 0)),
            pl.BlockSpec((None, CONV_HIST, SAMPLE_BB, CONV_CH), lambda i: (layer, 0, i, 0)),
        ],
        out_shape=[
            jax.ShapeDtypeStruct((DEC_SEQ, DEC_BATCH, D_MODEL), F32),
            jax.ShapeDtypeStruct((N_ODD, DEC_SEQ, DEC_BATCH, SG_WIDTH), F32),
            jax.ShapeDtypeStruct((N_ODD, CONV_HIST, DEC_BATCH, CONV_CH), F32),
        ],
        stacked={1: prev_sgv, 2: prev_cst},
        sem=("parallel",),
        scratch_shapes=[pltpu.VMEM((DEC_SEQ, SAMPLE_BB, CONV_CH), F32)],
    )


def _position_major(x):
    return jnp.swapaxes(x, -3, -2)


def kernel(x_prompt, x_sample, state_pool, state_ret, state_conv, norm_mix_pre, norm_mix_post, norm_ffn_pre, norm_ffn_post, w_in_even, w_pool, s_pool, w_out_even, w_in_odd, sg_ln_g, sg_ln_b, sg_w, sg_b, dw_w, dw_b, cv_ln_g, cv_ln_b, w_out_odd, w_up, w_down):
    w_in_even, w_out_even, w_in_odd, w_out_odd = (
        w.astype(BF16) for w in (w_in_even, w_out_even, w_in_odd, w_out_odd))

    state_pool = _position_major(state_pool)
    state_conv = _position_major(state_conv)
    h, a = _prenorm(x_prompt.reshape(N_PROMPT, D_MODEL), _position_major(x_sample).reshape(N_SAMPLE, D_MODEL),
                    norm_mix_pre[0])
    pool_p = pool_s = ret_p = ret_s = conv_p = conv_s = sgv_s = None
    for l in range(DEPTH):
        i = l // 2
        if l % 2 == 0:
            y_p, pool_p, ret_p = _even_prompt(a, w_in_even, w_pool, s_pool, i, pool_p, ret_p)
            z_s = _in_proj_sample(a, w_in_even, i).reshape(DEC_SEQ, DEC_BATCH, EVEN_IN)
            y_s, pool_s, ret_s = _even_sample(z_s, w_pool, s_pool, state_pool, state_ret, i, pool_s, ret_s)
            w_out = w_out_even
        else:
            y_p, conv_p = _odd_prompt(a, w_in_odd, sg_ln_g, sg_ln_b, sg_w, sg_b, dw_w, dw_b, cv_ln_g, cv_ln_b,
                                      i, conv_p)
            z_s = _in_proj_sample(a, w_in_odd, i).reshape(DEC_SEQ, DEC_BATCH, ODD_IN)
            y_s, sgv_s, conv_s = _odd_sample(z_s, sg_ln_g, sg_ln_b, sg_w, sg_b, dw_w, dw_b, cv_ln_g, cv_ln_b,
                                             state_conv, i, sgv_s, conv_s)
            w_out = w_out_odd
        h, f = _out_proj(y_p, y_s.reshape(N_SAMPLE, D_MODEL), w_out, i, norm_mix_post[l], norm_ffn_pre[l], h)
        if l + 1 < DEPTH:
            h, a = _ffn(f, h, w_up, w_down, l, norm_ffn_post[l], norm_mix_pre[l + 1])
        else:
            y_prompt, y_sample = _ffn(f, h, w_up, w_down, l, norm_ffn_post[l])

    y_sample = _position_major(y_sample.reshape(DEC_SEQ, DEC_BATCH, D_MODEL))
    return (y_prompt.reshape(BATCH, SEQ, D_MODEL), y_sample, pool_p, _position_major(pool_s), ret_p, ret_s,
            conv_p, _position_major(conv_s), _position_major(sgv_s))
```

```python
import functools

import jax
import jax.numpy as jnp
import numpy as np
from jax import lax
from jax.experimental import pallas as pl
from jax.experimental.pallas import tpu as pltpu

F32 = jnp.float32
BF16 = jnp.bfloat16

D_MODEL = 2048
BATCH = 4
SEQ = 2048
DEPTH = 4
DEC_BATCH = 128
DEC_SEQ = 4
PAST_LEN = 16384

N_EVEN = (DEPTH + 1) // 2
N_ODD = DEPTH // 2

POOL_WINDOWS = (2, 4, 8, 16)
POOL_GROUPS = len(POOL_WINDOWS)
POOL_WIDTH = D_MODEL // 4
POOL_GROUP_DIM = POOL_WIDTH // POOL_GROUPS
POOL_HIST = max(POOL_WINDOWS) - 1
RET_WIDTH = D_MODEL - POOL_WIDTH
RET_HEADS = 6
RET_DV = RET_WIDTH // RET_HEADS
RET_DK = RET_DV // 2
RET_QK = RET_HEADS * RET_DK
RET_CHUNK = 128
ROPE_BASE = 10000.0
SG_WIDTH = D_MODEL // 2
SG_CHUNK = 128
SG_GROUPS = 4
SG_GROUP_DIM = SG_WIDTH // SG_GROUPS
CONV_CH = D_MODEL // 2
CONV_K = 31
CONV_HIST = CONV_K - 1
D_FF = 4 * D_MODEL
EPS = 1e-6

EVEN_IN = POOL_WIDTH + 2 * RET_QK + 2 * RET_WIDTH
ODD_IN = 2 * SG_WIDTH + 2 * CONV_CH

Q_OFF = POOL_WIDTH
K_OFF = Q_OFF + RET_QK
V_OFF = K_OFF + RET_QK
G_OFF = V_OFF + RET_WIDTH

N_PROMPT = BATCH * SEQ
N_SAMPLE = DEC_BATCH * DEC_SEQ
N_TOK = N_PROMPT + N_SAMPLE

SUBLANES = 8
CHUNK = 128
N_CHUNKS = SEQ // CHUNK
SAMPLE_BB = SUBLANES
SAMPLE_ROWS = SAMPLE_BB * DEC_SEQ
POOL_PAD = 16
CONV_PAD = 32

TN_IN = 1024
TM_OUT = 512
SLAB = 128
TF = 1024
VMEM_LIMIT = 56 * 1024 * 1024

N_PROMPT_TILES = N_PROMPT // TM_OUT
assert N_PROMPT % TM_OUT == 0 and N_SAMPLE == TM_OUT and TM_OUT % SLAB == 0
assert POOL_PAD >= POOL_HIST and CONV_PAD >= CONV_HIST and DEC_BATCH % SAMPLE_BB == 0


def _params(*sem):
    return pltpu.CompilerParams(dimension_semantics=sem, vmem_limit_bytes=VMEM_LIMIT)


def _rms_scale(x, g):
    return x * lax.rsqrt(jnp.mean(x * x, axis=-1, keepdims=True) + EPS) * g


def _layer_norm(x, g, b):
    xc = x - jnp.mean(x, axis=-1, keepdims=True)
    return xc * lax.rsqrt(jnp.mean(xc * xc, axis=-1, keepdims=True) + EPS) * g + b


def _dot(a, b):
    return jnp.dot(a, b, preferred_element_type=F32)


def _dot_nt(a, b):
    return lax.dot_general(a, b, (((1,), (1,)), ((), ())), preferred_element_type=F32)


def _dot_tn(a, b):
    return lax.dot_general(a, b, (((0,), (0,)), ((), ())), preferred_element_type=F32)


def _skip_aliased(body, n_in, n_aliased):
    def wrapped(*refs):
        return body(*refs[:n_in], *refs[n_in + n_aliased:])
    return wrapped


def _call_stacked(body, *, name, grid, in_specs, args, out_specs, out_shape, stacked, sem, scratch_shapes):
    prev = [(o, p) for o, p in sorted(stacked.items()) if p is not None]
    n_in = len(args)
    return pl.pallas_call(
        _skip_aliased(body, n_in, len(prev)),
        grid=grid,
        in_specs=list(in_specs) + [pl.BlockSpec(memory_space=pl.ANY)] * len(prev),
        out_specs=out_specs,
        out_shape=out_shape,
        input_output_aliases={n_in + j: o for j, (o, _) in enumerate(prev)},
        scratch_shapes=scratch_shapes,
        compiler_params=_params(*sem),
        name=name,
    )(*args, *[p for _, p in prev])


def _is_prompt_tile():
    return pl.program_id(0) < N_PROMPT_TILES


def _prenorm_kernel(xp_ref, xs_ref, g_ref, h_ref, a_ref):
    def emit(x_ref):
        x = x_ref[...]
        h_ref[...] = x
        a_ref[...] = _rms_scale(x, g_ref[...]).astype(BF16)

    pl.when(_is_prompt_tile())(lambda: emit(xp_ref))
    pl.when(jnp.logical_not(_is_prompt_tile()))(lambda: emit(xs_ref))


def _prenorm(x_prompt, x_sample, g):
    return pl.pallas_call(
        _prenorm_kernel,
        grid=(N_TOK // TM_OUT,),
        in_specs=[
            pl.BlockSpec((TM_OUT, D_MODEL), lambda i: (jnp.minimum(i, N_PROMPT_TILES - 1), 0)),
            pl.BlockSpec((TM_OUT, D_MODEL), lambda i: (0, 0)),
            pl.BlockSpec((1, D_MODEL), lambda i: (0, 0)),
        ],
        out_specs=[
            pl.BlockSpec((TM_OUT, D_MODEL), lambda i: (i, 0)),
            pl.BlockSpec((TM_OUT, D_MODEL), lambda i: (i, 0)),
        ],
        out_shape=[
            jax.ShapeDtypeStruct((N_TOK, D_MODEL), F32),
            jax.ShapeDtypeStruct((N_TOK, D_MODEL), BF16),
        ],
        compiler_params=_params("parallel"),
        name="join_prenorm",
    )(x_prompt, x_sample, g.reshape(1, D_MODEL))


def _in_proj_kernel(a_ref, w_ref, o_ref):
    o_ref[...] = _dot(a_ref[...], w_ref[...])


def _in_proj_sample(a, w, layer):
    n_out = w.shape[2]
    return pl.pallas_call(
        _in_proj_kernel,
        grid=(n_out // TN_IN,),
        in_specs=[
            pl.BlockSpec((N_SAMPLE, D_MODEL), lambda j: (N_PROMPT // N_SAMPLE, 0)),
            pl.BlockSpec((None, D_MODEL, TN_IN), lambda j: (layer, 0, j)),
        ],
        out_specs=pl.BlockSpec((N_SAMPLE, TN_IN), lambda j: (0, j)),
        out_shape=jax.ShapeDtypeStruct((N_SAMPLE, n_out), F32),
        compiler_params=_params("parallel"),
        name="in_proj_sample",
    )(a, w)


def _out_proj_kernel(yp_ref, ys_ref, w_ref, g_ref, gf_ref, h_ref, o_ref, f_ref):
    def finish(y_ref):
        for r in range(0, TM_OUT, SLAB):
            rows = slice(r, r + SLAB)
            y = y_ref[rows, :].astype(BF16)
            hn = h_ref[rows, :] + _rms_scale(_dot(y, w_ref[...]), g_ref[...])
            o_ref[rows, :] = hn
            f_ref[rows, :] = _rms_scale(hn, gf_ref[...]).astype(BF16)

    pl.when(_is_prompt_tile())(lambda: finish(yp_ref))
    pl.when(jnp.logical_not(_is_prompt_tile()))(lambda: finish(ys_ref))


def _out_proj(y_prompt, y_sample, w, layer, g, g_ffn, h):
    return pl.pallas_call(
        _out_proj_kernel,
        grid=(N_TOK // TM_OUT,),
        in_specs=[
            pl.BlockSpec((TM_OUT, D_MODEL), lambda i: (jnp.minimum(i, N_PROMPT_TILES - 1), 0)),
            pl.BlockSpec((TM_OUT, D_MODEL), lambda i: (0, 0)),
            pl.BlockSpec((None, D_MODEL, D_MODEL), lambda i: (layer, 0, 0)),
            pl.BlockSpec((1, D_MODEL), lambda i: (0, 0)),
            pl.BlockSpec((1, D_MODEL), lambda i: (0, 0)),
            pl.BlockSpec((TM_OUT, D_MODEL), lambda i: (i, 0)),
        ],
        out_specs=[
            pl.BlockSpec((TM_OUT, D_MODEL), lambda i: (i, 0)),
            pl.BlockSpec((TM_OUT, D_MODEL), lambda i: (i, 0)),
        ],
        out_shape=[
            jax.ShapeDtypeStruct((N_TOK, D_MODEL), F32),
            jax.ShapeDtypeStruct((N_TOK, D_MODEL), BF16),
        ],
        compiler_params=_params("parallel"),
        name="out_proj_norm_residual",
    )(y_prompt, y_sample, w, g.reshape(1, D_MODEL), g_ffn.reshape(1, D_MODEL), h)


N_FF_CHUNKS = D_FF // TF
HEAD_SPLIT = 2
TF_HEAD = TF // HEAD_SPLIT


def _ffn_accumulate(k, f_ref, wu_ref, wd_ref, acc_ref, wub_ref=None, wdb_ref=None):
    @pl.when(k == 0)
    def _():
        acc_ref[...] = jnp.zeros_like(acc_ref)

    wu = wu_ref[...].astype(BF16)
    wd = wd_ref[...].astype(BF16)
    if wub_ref is not None:
        wub_ref[...] = wu
        wdb_ref[...] = wd
    u = jnp.square(jnp.maximum(_dot(f_ref[...], wu), 0.0)).astype(BF16)
    acc_ref[...] += _dot(u, wd)


def _ffn_epilogue(h_ref, acc_ref, g2_ref, o_ref, gn_ref=None, a_ref=None):
    for r in range(0, TM_OUT, SLAB):
        rows = slice(r, r + SLAB)
        hn = h_ref[rows, :] + _rms_scale(acc_ref[rows, :], g2_ref[...])
        o_ref[rows, :] = hn
        if a_ref is not None:
            a_ref[rows, :] = _rms_scale(hn, gn_ref[...]).astype(BF16)


def _ffn_head_kernel(f_ref, h_ref, wu_ref, wd_ref, g2_ref, *refs, last_layer):
    if last_layer:
        o_ref, wub_ref, wdb_ref, acc_ref = refs
        gn_ref = a_ref = None
    else:
        gn_ref, o_ref, a_ref, wub_ref, wdb_ref, acc_ref = refs
    k = pl.program_id(0)
    _ffn_accumulate(k, f_ref, wu_ref, wd_ref, acc_ref, wub_ref, wdb_ref)
    pl.when(k == pl.num_programs(0) - 1)(
        functools.partial(_ffn_epilogue, h_ref, acc_ref, g2_ref, o_ref, gn_ref, a_ref))


def _ffn_body_kernel(f_ref, h_ref, wu_ref, wd_ref, g2_ref, *refs, last_layer):
    k = pl.program_id(1)
    is_last = k == pl.num_programs(1) - 1
    if last_layer:
        yp_ref, ys_ref, acc_ref = refs
        _ffn_accumulate(k, f_ref, wu_ref, wd_ref, acc_ref)
        is_prompt = pl.program_id(0) + 1 < N_PROMPT_TILES
        pl.when(jnp.logical_and(is_last, is_prompt))(
            functools.partial(_ffn_epilogue, h_ref, acc_ref, g2_ref, yp_ref))
        pl.when(jnp.logical_and(is_last, jnp.logical_not(is_prompt)))(
            functools.partial(_ffn_epilogue, h_ref, acc_ref, g2_ref, ys_ref))
    else:
        gn_ref, o_ref, a_ref, acc_ref = refs
        _ffn_accumulate(k, f_ref, wu_ref, wd_ref, acc_ref)
        pl.when(is_last)(functools.partial(_ffn_epilogue, h_ref, acc_ref, g2_ref, o_ref, gn_ref, a_ref))


def _ffn(f, h, w_up, w_down, layer, g2, g_next=None):
    last_layer = g_next is None
    gains = [g2.reshape(1, D_MODEL)] + ([] if last_layer else [g_next.reshape(1, D_MODEL)])
    acc = pltpu.VMEM((TM_OUT, D_MODEL), F32)
    rounded_shapes = [
        jax.ShapeDtypeStruct((N_FF_CHUNKS, D_MODEL, TF), BF16),
        jax.ShapeDtypeStruct((D_FF, D_MODEL), BF16),
    ]
    if last_layer:
        act_shapes = [jax.ShapeDtypeStruct((N_PROMPT, D_MODEL), F32)]
    else:
        act_shapes = [jax.ShapeDtypeStruct((N_TOK, D_MODEL), F32), jax.ShapeDtypeStruct((N_TOK, D_MODEL), BF16)]

    tile0 = lambda **kw: pl.BlockSpec((TM_OUT, D_MODEL), lambda k: (0, 0), **kw)
    head = pl.pallas_call(
        functools.partial(_ffn_head_kernel, last_layer=last_layer),
        grid=(N_FF_CHUNKS * HEAD_SPLIT,),
        in_specs=[
            tile0(pipeline_mode=pl.Buffered(1)), tile0(pipeline_mode=pl.Buffered(1)),
            pl.BlockSpec((None, D_MODEL, TF_HEAD), lambda k: (layer, 0, k)),
            pl.BlockSpec((None, TF_HEAD, D_MODEL), lambda k: (layer, k, 0)),
        ] + [pl.BlockSpec((1, D_MODEL), lambda k: (0, 0))] * len(gains),
        out_specs=[tile0() for _ in act_shapes] + [
            pl.BlockSpec((None, D_MODEL, TF_HEAD), lambda k: (k // HEAD_SPLIT, 0, k % HEAD_SPLIT)),
            pl.BlockSpec((TF_HEAD, D_MODEL), lambda k: (k, 0)),
        ],
        out_shape=act_shapes + rounded_shapes,
        scratch_shapes=[acc],
        compiler_params=_params("arbitrary"),
        name="relu2_mlp_head",
    )(f, h, w_up, w_down, *gains)
    *acts, wu_b, wd_b = head

    tile = lambda: pl.BlockSpec((TM_OUT, D_MODEL), lambda i, k: (i + 1, 0))
    in_specs = [
        tile(), tile(),
        pl.BlockSpec((None, D_MODEL, TF), lambda i, k: (k, 0, 0)),
        pl.BlockSpec((TF, D_MODEL), lambda i, k: (k, 0)),
    ] + [pl.BlockSpec((1, D_MODEL), lambda i, k: (0, 0))] * len(gains)
    if last_layer:
        out_specs = [
            pl.BlockSpec((TM_OUT, D_MODEL), lambda i, k: (jnp.minimum(i + 1, N_PROMPT_TILES - 1), 0)),
            pl.BlockSpec((TM_OUT, D_MODEL), lambda i, k: (0, 0)),
        ]
        out_shape = act_shapes + [jax.ShapeDtypeStruct((N_SAMPLE, D_MODEL), F32)]
    else:
        out_specs = [tile(), tile()]
        out_shape = act_shapes
    n_in = len(in_specs)
    return pl.pallas_call(
        _skip_aliased(functools.partial(_ffn_body_kernel, last_layer=last_layer), n_in, len(acts)),
        grid=(N_TOK // TM_OUT - 1, N_FF_CHUNKS),
        in_specs=in_specs + [pl.BlockSpec(memory_space=pl.ANY)] * len(acts),
        out_specs=out_specs,
        out_shape=out_shape,
        input_output_aliases={n_in + j: j for j in range(len(acts))},
        scratch_shapes=[acc],
        compiler_params=_params("arbitrary", "arbitrary"),
        name="relu2_mlp_body",
    )(f, h, wu_b, wd_b, *gains, *acts)


def _rotary_tables(pos):
    half = RET_DK // 2
    inv = ROPE_BASE ** (-np.arange(half, dtype=np.float64) / half)
    ang = np.asarray(pos, np.float64)[:, None] * inv[None, :]
    cos = np.concatenate([np.cos(ang), np.cos(ang)], axis=-1)
    sin = np.concatenate([-np.sin(ang), np.sin(ang)], axis=-1)
    return jnp.asarray(cos, F32), jnp.asarray(sin, F32)


def _log_gamma():
    return np.log1p(-np.exp2(-5.0 - np.arange(RET_HEADS, dtype=np.float64)))


def _retention_tables(length):
    log_g = _log_gamma()
    idx = np.arange(length, dtype=np.float64)
    diff = idx[:, None] - idx[None, :]
    mask = np.where(diff[None] >= 0, np.exp(log_g[:, None, None] * np.maximum(diff, 0.0)[None]), 0.0)
    qd = np.exp(log_g[:, None] * (idx + 1.0))
    kd = np.exp(log_g[:, None] * (length - 1.0 - idx))
    cd = np.exp(log_g * length)
    return mask, qd, kd, cd


def _lane_bcast(a, width):
    return np.repeat(a[..., None], width, axis=-1)


N_PROMPT_CHUNKS = N_PROMPT // CHUNK
PROJ_COLS = 512


def _projected_chunk(step):
    return jnp.minimum(step, N_PROMPT_CHUNKS - 1)


def _mixed_chunk(step):
    return jnp.maximum(step - 1, 0)


def _interleave(first, second):
    i = j = 0
    while i < len(first) or j < len(second):
        if j >= len(second) or (i < len(first) and i * len(second) <= j * len(first)):
            first[i]()
            i += 1
        else:
            second[j]()
            j += 1


def _skewed(a_ref, w_ref, z_refs, n_cols, mix):
    step = pl.program_id(0)
    n = (step + N_CHUNKS - 1) % N_CHUNKS

    @pl.when(step == 0)
    def _():
        z_refs[1][...] = jnp.zeros_like(z_refs[1])

    for parity in range(2):
        z_next, z_mixed = z_refs[parity], z_refs[1 - parity]

        def body(z_next=z_next, z_mixed=z_mixed):
            a = a_ref[...]

            def project(c):
                z_next[:, c:c + PROJ_COLS] = _dot(a, w_ref[:, c:c + PROJ_COLS])

            mix(step, n, z_mixed, [functools.partial(project, c) for c in range(0, n_cols, PROJ_COLS)])

        pl.when(step % 2 == parity)(body)


def _retention_head_out(o, gate):
    o = o * lax.rsqrt(jnp.mean(o * o, axis=-1, keepdims=True) + EPS)
    return gate * jax.nn.sigmoid(gate) * o


def _rotate(x, cos, sin):
    return x * cos + pltpu.roll(x, RET_DK // 2, 1) * sin


def _even_prompt_kernel(a_ref, w_ref, cos_ref, sin_ref, mask_ref, qd_ref, kd_ref, wp_ref, sp_ref,
                        y_ref, hist_ref, st_ref, z0_ref, z1_ref, pext_ref, s_ref, *, chunk_decay):
    step = pl.program_id(0)
    n = (step + N_CHUNKS - 1) % N_CHUNKS

    @pl.when(jnp.logical_or(n == 0, step == 0))
    def _():
        pext_ref[0:POOL_PAD, :] = jnp.zeros((POOL_PAD, POOL_WIDTH), F32)
        s_ref[...] = jnp.zeros_like(s_ref)

    _skewed(a_ref, w_ref, (z0_ref, z1_ref), EVEN_IN,
            functools.partial(_even_prompt_mix, cos_ref, sin_ref, mask_ref, qd_ref, kd_ref, wp_ref, sp_ref, y_ref,
                              pext_ref, s_ref, chunk_decay))

    @pl.when(jnp.logical_and(n == N_CHUNKS - 1, step > 0))
    def _():
        hist_ref[0] = pext_ref[POOL_PAD + CHUNK - POOL_HIST:POOL_PAD + CHUNK, :]
        st_ref[0] = s_ref[...]


def _even_prompt_mix(cos_ref, sin_ref, mask_ref, qd_ref, kd_ref, wp_ref, sp_ref, y_ref, pext_ref, s_ref,
                     chunk_decay, step, n, zc, project):
    def pool():
        pext_ref[POOL_PAD:POOL_PAD + CHUNK, :] = zc[:, 0:POOL_WIDTH]
        pos = n * CHUNK + lax.broadcasted_iota(jnp.int32, (CHUNK, 1), 0)
        for g, w in enumerate(POOL_WINDOWS):
            lanes = slice(g * POOL_GROUP_DIM, (g + 1) * POOL_GROUP_DIM)
            p = pext_ref[POOL_PAD:POOL_PAD + CHUNK, lanes]
            acc = p
            for i in range(1, w):
                acc = acc + pext_ref[POOL_PAD - i:POOL_PAD - i + CHUNK, lanes]
            cnt = jnp.minimum(w, pos + 1).astype(F32)
            d = acc / cnt - p
            yg = _dot(d.astype(BF16), wp_ref[g].astype(BF16)) * sp_ref[:, lanes]
            y_ref[:, lanes] = yg.astype(BF16)
        pext_ref[0:POOL_PAD, :] = pext_ref[CHUNK:CHUNK + POOL_PAD, :]

    live = {}

    def scores_stage(h):
        cos = cos_ref[...]
        sin = sin_ref[...]
        q = _rotate(zc[:, Q_OFF + h * RET_DK:Q_OFF + (h + 1) * RET_DK], cos, sin)
        k = _rotate(zc[:, K_OFF + h * RET_DK:K_OFF + (h + 1) * RET_DK], cos, sin) * (RET_DK ** -0.5)
        v = zc[:, V_OFF + h * RET_DV:V_OFF + (h + 1) * RET_DV].astype(BF16)
        live[h] = (_dot_nt(q.astype(BF16), k.astype(BF16)), (q * qd_ref[h]).astype(BF16),
                   (k * kd_ref[h]).astype(BF16), v)

    def output_stage(h):
        scores, q_dec, k_dec, v = live[h]
        s = s_ref[h]
        o = _dot((scores * mask_ref[h]).astype(BF16), v) + _dot(q_dec, s.astype(BF16))
        s_ref[h] = s * chunk_decay[h] + _dot_tn(k_dec, v)
        live[h] = o

    def norm_stage(h):
        gate = zc[:, G_OFF + h * RET_DV:G_OFF + (h + 1) * RET_DV]
        y_ref[:, POOL_WIDTH + h * RET_DV:POOL_WIDTH + (h + 1) * RET_DV] = (
            _retention_head_out(live.pop(h), gate).astype(BF16))

    def heads_step(i):
        for stage, h in ((norm_stage, i - 2), (output_stage, i - 1), (scores_stage, i)):
            if 0 <= h < RET_HEADS:
                stage(h)

    _interleave(project, [pool] + [functools.partial(heads_step, i) for i in range(RET_HEADS + 2)])


def _even_prompt(a, w_in, w_pool, s_pool, layer, prev_hist, prev_state):
    cos, sin = _rotary_tables(np.arange(SEQ))
    mask, qd, kd, cd = _retention_tables(CHUNK)
    const = lambda *shape: pl.BlockSpec(shape, lambda t: (0,) * len(shape))
    return _call_stacked(
        functools.partial(_even_prompt_kernel, chunk_decay=tuple(float(c) for c in cd)),
        name="even_mixer_prompt",
        grid=(N_PROMPT_CHUNKS + 1,),
        in_specs=[
            pl.BlockSpec((CHUNK, D_MODEL), lambda t: (_projected_chunk(t), 0)),
            pl.BlockSpec((None, D_MODEL, EVEN_IN), lambda t: (layer, 0, 0), pipeline_mode=pl.Buffered(1)),
            pl.BlockSpec((CHUNK, RET_DK), lambda t: (_mixed_chunk(t) % N_CHUNKS, 0)),
            pl.BlockSpec((CHUNK, RET_DK), lambda t: (_mixed_chunk(t) % N_CHUNKS, 0)),
            const(RET_HEADS, CHUNK, CHUNK),
            const(RET_HEADS, CHUNK, RET_DK),
            const(RET_HEADS, CHUNK, RET_DK),
            pl.BlockSpec((None, POOL_GROUPS, POOL_GROUP_DIM, POOL_GROUP_DIM), lambda t: (layer, 0, 0, 0)),
            pl.BlockSpec((None, 1, POOL_WIDTH), lambda t: (layer, 0, 0)),
        ],
        args=(a, w_in, cos, sin, jnp.asarray(mask, F32), jnp.asarray(_lane_bcast(qd, RET_DK), F32),
              jnp.asarray(_lane_bcast(kd, RET_DK), F32), w_pool, s_pool.reshape(N_EVEN, 1, POOL_WIDTH)),
        out_specs=[
            pl.BlockSpec((CHUNK, D_MODEL), lambda t: (_mixed_chunk(t), 0)),
            pl.BlockSpec((None, 1, POOL_HIST, POOL_WIDTH), lambda t: (layer, _mixed_chunk(t) // N_CHUNKS, 0, 0)),
            pl.BlockSpec((None, 1, RET_HEADS, RET_DK, RET_DV),
                         lambda t: (layer, _mixed_chunk(t) // N_CHUNKS, 0, 0, 0)),
        ],
        out_shape=[
            jax.ShapeDtypeStruct((N_PROMPT, D_MODEL), BF16),
            jax.ShapeDtypeStruct((N_EVEN, BATCH, POOL_HIST, POOL_WIDTH), F32),
            jax.ShapeDtypeStruct((N_EVEN, BATCH, RET_HEADS, RET_DK, RET_DV), F32),
        ],
        stacked={1: prev_hist, 2: prev_state},
        sem=("arbitrary",),
        scratch_shapes=[
            pltpu.VMEM((CHUNK, EVEN_IN), F32),
            pltpu.VMEM((CHUNK, EVEN_IN), F32),
            pltpu.VMEM((POOL_PAD + CHUNK, POOL_WIDTH), F32),
            pltpu.VMEM((RET_HEADS, RET_DK, RET_DV), F32),
        ],
    )


def _position_slabs(x):
    return x.reshape(SAMPLE_ROWS, x.shape[-1])


def _even_sample_kernel(z_ref, cos_ref, sin_ref, mask_ref, qd_ref, kd_ref, wp_ref, sp_ref, hist_ref, st_ref,
                        y_ref, nhist_ref, nst_ref, *, chunk_decay):
    def ext(r, lanes):
        return hist_ref[r, :, lanes] if r < POOL_HIST else z_ref[r - POOL_HIST, :, lanes]

    for g, w in enumerate(POOL_WINDOWS):
        lanes = slice(g * POOL_GROUP_DIM, (g + 1) * POOL_GROUP_DIM)
        d = []
        for t in range(DEC_SEQ):
            p = ext(POOL_HIST + t, lanes)
            acc = p
            for i in range(1, w):
                acc = acc + ext(POOL_HIST + t - i, lanes)
            d.append(acc / float(min(w, PAST_LEN + t + 1)) - p)
        d = jnp.concatenate(d, axis=0)
        yg = _dot(d.astype(BF16), wp_ref[g].astype(BF16)) * sp_ref[:, lanes]
        y_ref[:, :, lanes] = yg.reshape(DEC_SEQ, SAMPLE_BB, POOL_GROUP_DIM)
    for r in range(POOL_HIST):
        nhist_ref[r] = hist_ref[r + DEC_SEQ] if r + DEC_SEQ < POOL_HIST else z_ref[r + DEC_SEQ - POOL_HIST, :,
                                                                                 0:POOL_WIDTH]

    cos = cos_ref[...]
    sin = sin_ref[...]
    seq_of_row = lax.broadcasted_iota(jnp.int32, (SAMPLE_ROWS, 1), 0) % SAMPLE_BB
    for h in range(RET_HEADS):
        q = _rotate(_position_slabs(z_ref[:, :, Q_OFF + h * RET_DK:Q_OFF + (h + 1) * RET_DK]), cos, sin)
        k = _rotate(_position_slabs(z_ref[:, :, K_OFF + h * RET_DK:K_OFF + (h + 1) * RET_DK]), cos, sin)
        k = k * (RET_DK ** -0.5)
        v = _position_slabs(z_ref[:, :, V_OFF + h * RET_DV:V_OFF + (h + 1) * RET_DV]).astype(BF16)
        gate = _position_slabs(z_ref[:, :, G_OFF + h * RET_DV:G_OFF + (h + 1) * RET_DV])
        scores = _dot_nt(q.astype(BF16), k.astype(BF16)) * mask_ref[h]
        o = _dot(scores.astype(BF16), v)
        q_dec = (q * qd_ref[h]).astype(BF16)
        k_dec = k * kd_ref[h]
        for b in range(SAMPLE_BB):
            own = seq_of_row == b
            s = st_ref[b, h]
            o = o + jnp.where(own, _dot(q_dec, s.astype(BF16)), 0.0)
            nst_ref[b, h] = s * chunk_decay[h] + _dot_tn(jnp.where(own, k_dec, 0.0).astype(BF16), v)
        y_ref[:, :, POOL_WIDTH + h * RET_DV:POOL_WIDTH + (h + 1) * RET_DV] = (
            _retention_head_out(o, gate).reshape(DEC_SEQ, SAMPLE_BB, RET_DV))


def _even_sample(z, w_pool, s_pool, state_pool, state_ret, layer, prev_hist, prev_state):
    pos = PAST_LEN + np.repeat(np.arange(DEC_SEQ), SAMPLE_BB)
    cos, sin = _rotary_tables(pos)
    mask, qd, kd, cd = _retention_tables(DEC_SEQ)
    mask = np.stack([np.kron(m, np.eye(SAMPLE_BB)) for m in mask])
    qd = _lane_bcast(np.repeat(qd, SAMPLE_BB, axis=1), RET_DK)
    kd = _lane_bcast(np.repeat(kd, SAMPLE_BB, axis=1), RET_DK)
    const = lambda *shape: pl.BlockSpec(shape, lambda i: (0,) * len(shape))
    return _call_stacked(
        functools.partial(_even_sample_kernel, chunk_decay=tuple(float(c) for c in cd)),
        name="even_mixer_sample",
        grid=(DEC_BATCH // SAMPLE_BB,),
        in_specs=[
            pl.BlockSpec((DEC_SEQ, SAMPLE_BB, EVEN_IN), lambda i: (0, i, 0)),
            const(SAMPLE_ROWS, RET_DK),
            const(SAMPLE_ROWS, RET_DK),
            const(RET_HEADS, SAMPLE_ROWS, SAMPLE_ROWS),
            const(RET_HEADS, SAMPLE_ROWS, RET_DK),
            const(RET_HEADS, SAMPLE_ROWS, RET_DK),
            pl.BlockSpec((None, POOL_GROUPS, POOL_GROUP_DIM, POOL_GROUP_DIM), lambda i: (layer, 0, 0, 0)),
            pl.BlockSpec((None, 1, POOL_WIDTH), lambda i: (layer, 0, 0)),
            pl.BlockSpec((None, POOL_HIST, SAMPLE_BB, POOL_WIDTH), lambda i: (layer, 0, i, 0)),
            pl.BlockSpec((None, SAMPLE_BB, RET_HEADS, RET_DK, RET_DV), lambda i: (layer, i, 0, 0, 0)),
        ],
        args=(z, cos, sin, jnp.asarray(mask, F32), jnp.asarray(qd, F32), jnp.asarray(kd, F32),
              w_pool, s_pool.reshape(N_EVEN, 1, POOL_WIDTH), state_pool, state_ret),
        out_specs=[
            pl.BlockSpec((DEC_SEQ, SAMPLE_BB, D_MODEL), lambda i: (0, i, 0)),
            pl.BlockSpec((None, POOL_HIST, SAMPLE_BB, POOL_WIDTH), lambda i: (layer, 0, i, 0)),
            pl.BlockSpec((None, SAMPLE_BB, RET_HEADS, RET_DK, RET_DV), lambda i: (layer, i, 0, 0, 0)),
        ],
        out_shape=[
            jax.ShapeDtypeStruct((DEC_SEQ, DEC_BATCH, D_MODEL), F32),
            jax.ShapeDtypeStruct((N_EVEN, POOL_HIST, DEC_BATCH, POOL_WIDTH), F32),
            jax.ShapeDtypeStruct((N_EVEN, DEC_BATCH, RET_HEADS, RET_DK, RET_DV), F32),
        ],
        stacked={1: prev_hist, 2: prev_state},
        sem=("parallel",),
        scratch_shapes=[],
    )


CONV_LANES = 128
SHIFTED_ROWS = CONV_PAD + CHUNK - SUBLANES


def _odd_prompt_kernel(a_ref, w_ref, lng_ref, lnb_ref, sgw_ref, sgb_ref, dw_ref, dwb_ref, cvg_ref, cvb_ref,
                       y_ref, cst_ref, z0_ref, z1_ref, vb_ref, ext_ref, cv_ref, xs_ref):
    step = pl.program_id(0)
    n = (step + N_CHUNKS - 1) % N_CHUNKS

    @pl.when(jnp.logical_or(n == 0, step == 0))
    def _():
        ext_ref[0:CONV_PAD, :] = jnp.zeros((CONV_PAD, CONV_CH), F32)

    _skewed(a_ref, w_ref, (z0_ref, z1_ref), ODD_IN,
            functools.partial(_odd_prompt_mix, lng_ref, lnb_ref, sgw_ref, sgb_ref, dw_ref, dwb_ref, cvg_ref, cvb_ref,
                              y_ref, vb_ref, ext_ref, cv_ref, xs_ref))

    @pl.when(jnp.logical_and(n == N_CHUNKS - 1, step > 0))
    def _():
        cst_ref[0] = ext_ref[CONV_PAD + CHUNK - CONV_HIST:CONV_PAD + CHUNK, :]

    ext_ref[0:CONV_PAD, :] = ext_ref[CHUNK:CHUNK + CONV_PAD, :]


def _odd_prompt_mix(lng_ref, lnb_ref, sgw_ref, sgb_ref, dw_ref, dwb_ref, cvg_ref, cvb_ref,
                    y_ref, vb_ref, ext_ref, cv_ref, xs_ref, step, n, zc, project):
    def gate_values():
        v = _layer_norm(jax.nn.gelu(zc[:, SG_WIDTH:2 * SG_WIDTH]), lng_ref[...], lnb_ref[...])
        vb_ref[...] = v.astype(BF16)

    def gating(g):
        lanes = slice(g * SG_GROUP_DIM, (g + 1) * SG_GROUP_DIM)
        row = lax.broadcasted_iota(jnp.int32, (CHUNK, CHUNK), 0)
        col = lax.broadcasted_iota(jnp.int32, (CHUNK, CHUNK), 1)
        ws = jnp.where(col <= row, sgw_ref[g], 0.0).astype(BF16)
        mixed = _dot(ws, vb_ref[:, lanes]) + sgb_ref[:, g:g + 1]
        y_ref[:, lanes] = (jax.nn.gelu(zc[:, lanes]) * mixed).astype(BF16)

    def glu():
        a = zc[:, 2 * SG_WIDTH:2 * SG_WIDTH + CONV_CH]
        gate = zc[:, 2 * SG_WIDTH + CONV_CH:2 * SG_WIDTH + 2 * CONV_CH]
        ext_ref[CONV_PAD:CONV_PAD + CHUNK, :] = a * jax.nn.sigmoid(gate)

    def shifted_copy(s):
        xs_ref[s - 1] = ext_ref[s:s + SHIFTED_ROWS, :]

    def conv(c):
        lanes = slice(c, c + CONV_LANES)
        acc = jnp.broadcast_to(dwb_ref[:, lanes], (CHUNK, CONV_LANES))
        for j in range(CONV_K):
            tile, s = divmod(CONV_PAD - CONV_HIST + j, SUBLANES)
            rows = slice(tile * SUBLANES, tile * SUBLANES + CHUNK)
            window = ext_ref[rows, lanes] if s == 0 else xs_ref[s - 1, rows, lanes]
            acc = acc + window * dw_ref[j:j + 1, lanes]
        cv_ref[:, lanes] = acc

    def conv_out():
        yd = _layer_norm(cv_ref[...], cvg_ref[...], cvb_ref[...])
        y_ref[:, SG_WIDTH:SG_WIDTH + CONV_CH] = (yd * jax.nn.sigmoid(yd)).astype(BF16)

    _interleave(project, [gate_values, glu] + [functools.partial(shifted_copy, s) for s in range(1, SUBLANES)]
                + [functools.partial(conv, c) for c in range(0, CONV_CH, CONV_LANES)] + [conv_out])
    for g in range(SG_GROUPS):
        gating(g)


def _odd_weight_specs(layer):
    per_layer = lambda *shape: pl.BlockSpec((None,) + shape, lambda *_: (layer,) + (0,) * len(shape))
    return dict(
        ln=per_layer(1, SG_WIDTH),
        sgw=per_layer(SG_GROUPS, SG_CHUNK, SG_CHUNK),
        sgb=per_layer(SG_CHUNK, SG_GROUPS),
        dw=per_layer(CONV_K, CONV_CH),
        ch=per_layer(1, CONV_CH),
    )


def _odd_prompt(a, w_in, sg_ln_g, sg_ln_b, sg_w, sg_b, dw_w, dw_b, cv_ln_g, cv_ln_b, layer, prev_cst):
    spec = _odd_weight_specs(layer)
    row = lambda x: x.reshape(N_ODD, 1, -1)
    return _call_stacked(
        _odd_prompt_kernel,
        name="odd_mixer_prompt",
        grid=(N_PROMPT_CHUNKS + 1,),
        in_specs=[
            pl.BlockSpec((CHUNK, D_MODEL), lambda t: (_projected_chunk(t), 0)),
            pl.BlockSpec((None, D_MODEL, ODD_IN), lambda t: (layer, 0, 0), pipeline_mode=pl.Buffered(1)),
            spec["ln"], spec["ln"], spec["sgw"], spec["sgb"], spec["dw"], spec["ch"], spec["ch"], spec["ch"],
        ],
        args=(a, w_in, row(sg_ln_g), row(sg_ln_b), sg_w, jnp.swapaxes(sg_b, 1, 2), dw_w, row(dw_b),
              row(cv_ln_g), row(cv_ln_b)),
        out_specs=[
            pl.BlockSpec((CHUNK, D_MODEL), lambda t: (_mixed_chunk(t), 0)),
            pl.BlockSpec((None, 1, CONV_HIST, CONV_CH), lambda t: (layer, _mixed_chunk(t) // N_CHUNKS, 0, 0)),
        ],
        out_shape=[
            jax.ShapeDtypeStruct((N_PROMPT, D_MODEL), BF16),
            jax.ShapeDtypeStruct((N_ODD, BATCH, CONV_HIST, CONV_CH), F32),
        ],
        stacked={1: prev_cst},
        sem=("arbitrary",),
        scratch_shapes=[
            pltpu.VMEM((CHUNK, ODD_IN), F32),
            pltpu.VMEM((CHUNK, ODD_IN), F32),
            pltpu.VMEM((CHUNK, SG_WIDTH), BF16),
            pltpu.VMEM((CONV_PAD + CHUNK, CONV_CH), F32),
            pltpu.VMEM((CHUNK, CONV_CH), F32),
            pltpu.VMEM((SUBLANES - 1, SHIFTED_ROWS, CONV_CH), F32),
        ],
    )


def _odd_sample_kernel(sgw_ref, sgb_ref, z_ref, lng_ref, lnb_ref, dw_ref, dwb_ref, cvg_ref, cvb_ref, cst_ref,
                       y_ref, sgv_ref, ncst_ref, cv_ref):
    v = _layer_norm(jax.nn.gelu(_position_slabs(z_ref[:, :, SG_WIDTH:2 * SG_WIDTH])), lng_ref[...], lnb_ref[...])
    sgv_ref[...] = v.reshape(DEC_SEQ, SAMPLE_BB, SG_WIDTH)
    for g in range(SG_GROUPS):
        lanes = slice(g * SG_GROUP_DIM, (g + 1) * SG_GROUP_DIM)
        for t in range(DEC_SEQ):
            mixed = jnp.full((SAMPLE_BB, SG_GROUP_DIM), sgb_ref[g * DEC_SEQ + t], F32)
            for j in range(t + 1):
                mixed = mixed + sgw_ref[(g * DEC_SEQ + t) * DEC_SEQ + j] * sgv_ref[j, :, lanes]
            y_ref[t, :, lanes] = jax.nn.gelu(z_ref[t, :, lanes]) * mixed

    def ext(r, lanes):
        if r < CONV_HIST:
            return cst_ref[r, :, lanes]
        a = z_ref[r - CONV_HIST, :, pl.ds(2 * SG_WIDTH + lanes.start, CONV_LANES)]
        gate = z_ref[r - CONV_HIST, :, pl.ds(2 * SG_WIDTH + CONV_CH + lanes.start, CONV_LANES)]
        return a * jax.nn.sigmoid(gate)

    for c in range(0, CONV_CH, CONV_LANES):
        lanes = slice(c, c + CONV_LANES)
        rows = [ext(r, lanes) for r in range(CONV_HIST + DEC_SEQ)]
        for t in range(DEC_SEQ):
            acc = jnp.broadcast_to(dwb_ref[:, lanes], (SAMPLE_BB, CONV_LANES))
            for j in range(CONV_K):
                acc = acc + rows[t + j] * dw_ref[j:j + 1, lanes]
            cv_ref[t, :, lanes] = acc
        for r in range(CONV_HIST):
            ncst_ref[r, :, lanes] = rows[r + DEC_SEQ]
    yd = _layer_norm(_position_slabs(cv_ref[...]), cvg_ref[...], cvb_ref[...])
    y_ref[:, :, SG_WIDTH:SG_WIDTH + CONV_CH] = (yd * jax.nn.sigmoid(yd)).reshape(DEC_SEQ, SAMPLE_BB, CONV_CH)


def _odd_sample(z, sg_ln_g, sg_ln_b, sg_w, sg_b, dw_w, dw_b, cv_ln_g, cv_ln_b, state_conv, layer,
                prev_sgv, prev_cst):
    spec = _odd_weight_specs(layer)
    row = lambda x: x.reshape(N_ODD, 1, -1)
    smem = pl.BlockSpec(memory_space=pltpu.SMEM)
    return _call_stacked(
        _odd_sample_kernel,
        name="odd_mixer_sample",
        grid=(DEC_BATCH // SAMPLE_BB,),
        in_specs=[
            smem,
            smem,
            pl.BlockSpec((DEC_SEQ, SAMPLE_BB, ODD_IN), lambda i: (0, i, 0)),
            spec["ln"], spec["ln"], spec["dw"], spec["ch"], spec["ch"], spec["ch"],
            pl.BlockSpec((None, CONV_HIST, SAMPLE_BB, CONV_CH), lambda i: (layer, 0, i, 0)),
        ],
        args=(sg_w[layer, :, :DEC_SEQ, :DEC_SEQ].reshape(-1), sg_b[layer, :, :DEC_SEQ].reshape(-1), z,
              row(sg_ln_g), row(sg_ln_b), dw_w, row(dw_b), row(cv_ln_g), row(cv_ln_b), state_conv),
        out_specs=[
            pl.BlockSpec((DEC_SEQ, SAMPLE_BB, D_MODEL), lambda i: (0, i, 0)),
            pl.BlockSpec((None, DEC_SEQ, SAMPLE_BB, SG_WIDTH), lambda i: (layer, 0, i, 0)),
            pl.BlockSpec((None, CONV_HIST, SAMPLE_BB, CONV_CH), lambda i: (layer, 0, i, 0)),
        ],
        out_shape=[
            jax.ShapeDtypeStruct((DEC_SEQ, DEC_BATCH, D_MODEL), F32),
            jax.ShapeDtypeStruct((N_ODD, DEC_SEQ, DEC_BATCH, SG_WIDTH), F32),
            jax.ShapeDtypeStruct((N_ODD, CONV_HIST, DEC_BATCH, CONV_CH), F32),
        ],
        stacked={1: prev_sgv, 2: prev_cst},
        sem=("parallel",),
        scratch_shapes=[pltpu.VMEM((DEC_SEQ, SAMPLE_BB, CONV_CH), F32)],
    )


def _position_major(x):
    return jnp.swapaxes(x, -3, -2)


def kernel(x_prompt, x_sample, state_pool, state_ret, state_conv, norm_mix_pre, norm_mix_post, norm_ffn_pre, norm_ffn_post, w_in_even, w_pool, s_pool, w_out_even, w_in_odd, sg_ln_g, sg_ln_b, sg_w, sg_b, dw_w, dw_b, cv_ln_g, cv_ln_b, w_out_odd, w_up, w_down):
    w_in_even, w_out_even, w_in_odd, w_out_odd = (
        w.astype(BF16) for w in (w_in_even, w_out_even, w_in_odd, w_out_odd))

    state_pool = _position_major(state_pool)
    state_conv = _position_major(state_conv)
    h, a = _prenorm(x_prompt.reshape(N_PROMPT, D_MODEL), _position_major(x_sample).reshape(N_SAMPLE, D_MODEL),
                    norm_mix_pre[0])
    pool_p = pool_s = ret_p = ret_s = conv_p = conv_s = sgv_s = None
    for l in range(DEPTH):
        i = l // 2
        if l % 2 == 0:
            y_p, pool_p, ret_p = _even_prompt(a, w_in_even, w_pool, s_pool, i, pool_p, ret_p)
            z_s = _in_proj_sample(a, w_in_even, i).reshape(DEC_SEQ, DEC_BATCH, EVEN_IN)
            y_s, pool_s, ret_s = _even_sample(z_s, w_pool, s_pool, state_pool, state_ret, i, pool_s, ret_s)
            w_out = w_out_even
        else:
            y_p, conv_p = _odd_prompt(a, w_in_odd, sg_ln_g, sg_ln_b, sg_w, sg_b, dw_w, dw_b, cv_ln_g, cv_ln_b,
                                      i, conv_p)
            z_s = _in_proj_sample(a, w_in_odd, i).reshape(DEC_SEQ, DEC_BATCH, ODD_IN)
            y_s, sgv_s, conv_s = _odd_sample(z_s, sg_ln_g, sg_ln_b, sg_w, sg_b, dw_w, dw_b, cv_ln_g, cv_ln_b,
                                             state_conv, i, sgv_s, conv_s)
            w_out = w_out_odd
        h, f = _out_proj(y_p, y_s.reshape(N_SAMPLE, D_MODEL), w_out, i, norm_mix_post[l], norm_ffn_pre[l], h)
        if l + 1 < DEPTH:
            h, a = _ffn(f, h, w_up, w_down, l, norm_ffn_post[l], norm_mix_pre[l + 1])
        else:
            y_prompt, y_sample = _ffn(f, h, w_up, w_down, l, norm_ffn_post[l])

    y_sample = _position_major(y_sample.reshape(DEC_SEQ, DEC_BATCH, D_MODEL))
    return (y_prompt.reshape(BATCH, SEQ, D_MODEL), y_sample, pool_p, _position_major(pool_s), ret_p, ret_s,
            conv_p, _position_major(conv_s), _position_major(sgv_s))
```

```python
import functools

import jax
import jax.numpy as jnp
import numpy as np
from jax import lax
from jax.experimental import pallas as pl
from jax.experimental.pallas import tpu as pltpu

F32 = jnp.float32
BF16 = jnp.bfloat16

D_MODEL = 2048
BATCH = 4
SEQ = 2048
DEPTH = 4
DEC_BATCH = 128
DEC_SEQ = 4
PAST_LEN = 16384

N_EVEN = (DEPTH + 1) // 2
N_ODD = DEPTH // 2

POOL_WINDOWS = (2, 4, 8, 16)
POOL_GROUPS = len(POOL_WINDOWS)
POOL_WIDTH = D_MODEL // 4
POOL_GROUP_DIM = POOL_WIDTH // POOL_GROUPS
POOL_HIST = max(POOL_WINDOWS) - 1
RET_WIDTH = D_MODEL - POOL_WIDTH
RET_HEADS = 6
RET_DV = RET_WIDTH // RET_HEADS
RET_DK = RET_DV // 2
RET_QK = RET_HEADS * RET_DK
RET_CHUNK = 128
ROPE_BASE = 10000.0
SG_WIDTH = D_MODEL // 2
SG_CHUNK = 128
SG_GROUPS = 4
SG_GROUP_DIM = SG_WIDTH // SG_GROUPS
CONV_CH = D_MODEL // 2
CONV_K = 31
CONV_HIST = CONV_K - 1
D_FF = 4 * D_MODEL
EPS = 1e-6

EVEN_IN = POOL_WIDTH + 2 * RET_QK + 2 * RET_WIDTH
ODD_IN = 2 * SG_WIDTH + 2 * CONV_CH

Q_OFF = POOL_WIDTH
K_OFF = Q_OFF + RET_QK
V_OFF = K_OFF + RET_QK
G_OFF = V_OFF + RET_WIDTH

N_PROMPT = BATCH * SEQ
N_SAMPLE = DEC_BATCH * DEC_SEQ
N_TOK = N_PROMPT + N_SAMPLE

SUBLANES = 8
CHUNK = 128
N_CHUNKS = SEQ // CHUNK
SAMPLE_BB = SUBLANES
SAMPLE_ROWS = SAMPLE_BB * DEC_SEQ
POOL_PAD = 16
CONV_PAD = 32

TN_IN = 1024
TM_OUT = 512
SLAB = 128
TF = 1024
VMEM_LIMIT = 56 * 1024 * 1024

N_PROMPT_TILES = N_PROMPT // TM_OUT
assert N_PROMPT % TM_OUT == 0 and N_SAMPLE == TM_OUT and TM_OUT % SLAB == 0
assert POOL_PAD >= POOL_HIST and CONV_PAD >= CONV_HIST and DEC_BATCH % SAMPLE_BB == 0


def _params(*sem):
    return pltpu.CompilerParams(dimension_semantics=sem, vmem_limit_bytes=VMEM_LIMIT)


def _rms_scale(x, g):
    return x * lax.rsqrt(jnp.mean(x * x, axis=-1, keepdims=True) + EPS) * g


def _layer_norm(x, g, b):
    xc = x - jnp.mean(x, axis=-1, keepdims=True)
    return xc * lax.rsqrt(jnp.mean(xc * xc, axis=-1, keepdims=True) + EPS) * g + b


def _dot(a, b):
    return jnp.dot(a, b, preferred_element_type=F32)


def _dot_nt(a, b):
    return lax.dot_general(a, b, (((1,), (1,)), ((), ())), preferred_element_type=F32)


def _dot_tn(a, b):
    return lax.dot_general(a, b, (((0,), (0,)), ((), ())), preferred_element_type=F32)


def _skip_aliased(body, n_in, n_aliased):
    def wrapped(*refs):
        return body(*refs[:n_in], *refs[n_in + n_aliased:])
    return wrapped


def _call_stacked(body, *, name, grid, in_specs, args, out_specs, out_shape, stacked, sem, scratch_shapes):
    prev = [(o, p) for o, p in sorted(stacked.items()) if p is not None]
    n_in = len(args)
    return pl.pallas_call(
        _skip_aliased(body, n_in, len(prev)),
        grid=grid,
        in_specs=list(in_specs) + [pl.BlockSpec(memory_space=pl.ANY)] * len(prev),
        out_specs=out_specs,
        out_shape=out_shape,
        input_output_aliases={n_in + j: o for j, (o, _) in enumerate(prev)},
        scratch_shapes=scratch_shapes,
        compiler_params=_params(*sem),
        name=name,
    )(*args, *[p for _, p in prev])


def _is_prompt_tile():
    return pl.program_id(0) < N_PROMPT_TILES


def _prenorm_kernel(xp_ref, xs_ref, g_ref, h_ref, a_ref):
    def emit(x_ref):
        x = x_ref[...]
        h_ref[...] = x
        a_ref[...] = _rms_scale(x, g_ref[...]).astype(BF16)

    pl.when(_is_prompt_tile())(lambda: emit(xp_ref))
    pl.when(jnp.logical_not(_is_prompt_tile()))(lambda: emit(xs_ref))


def _prenorm(x_prompt, x_sample, g):
    return pl.pallas_call(
        _prenorm_kernel,
        grid=(N_TOK // TM_OUT,),
        in_specs=[
            pl.BlockSpec((TM_OUT, D_MODEL), lambda i: (jnp.minimum(i, N_PROMPT_TILES - 1), 0)),
            pl.BlockSpec((TM_OUT, D_MODEL), lambda i: (0, 0)),
            pl.BlockSpec((1, D_MODEL), lambda i: (0, 0)),
        ],
        out_specs=[
            pl.BlockSpec((TM_OUT, D_MODEL), lambda i: (i, 0)),
            pl.BlockSpec((TM_OUT, D_MODEL), lambda i: (i, 0)),
        ],
        out_shape=[
            jax.ShapeDtypeStruct((N_TOK, D_MODEL), F32),
            jax.ShapeDtypeStruct((N_TOK, D_MODEL), BF16),
        ],
        compiler_params=_params("parallel"),
        name="join_prenorm",
    )(x_prompt, x_sample, g.reshape(1, D_MODEL))


def _in_proj_kernel(a_ref, w_ref, o_ref, wb_ref):
    w = w_ref[...].astype(BF16)
    wb_ref[...] = w
    o_ref[...] = _dot(a_ref[...], w)


def _in_proj_sample(a, w, layer):
    n_out = w.shape[2]
    return pl.pallas_call(
        _in_proj_kernel,
        grid=(n_out // TN_IN,),
        in_specs=[
            pl.BlockSpec((N_SAMPLE, D_MODEL), lambda j: (N_PROMPT // N_SAMPLE, 0)),
            pl.BlockSpec((None, D_MODEL, TN_IN), lambda j: (layer, 0, j)),
        ],
        out_specs=[
            pl.BlockSpec((N_SAMPLE, TN_IN), lambda j: (0, j)),
            pl.BlockSpec((D_MODEL, TN_IN), lambda j: (0, j)),
        ],
        out_shape=[
            jax.ShapeDtypeStruct((N_SAMPLE, n_out), F32),
            jax.ShapeDtypeStruct((D_MODEL, n_out), BF16),
        ],
        compiler_params=_params("parallel"),
        name="in_proj_sample",
    )(a, w)


def _out_proj_kernel(yp_ref, ys_ref, w_ref, g_ref, gf_ref, h_ref, o_ref, f_ref):
    def finish(y_ref):
        for r in range(0, TM_OUT, SLAB):
            rows = slice(r, r + SLAB)
            y = y_ref[rows, :].astype(BF16)
            hn = h_ref[rows, :] + _rms_scale(_dot(y, w_ref[...]), g_ref[...])
            o_ref[rows, :] = hn
            f_ref[rows, :] = _rms_scale(hn, gf_ref[...]).astype(BF16)

    pl.when(_is_prompt_tile())(lambda: finish(yp_ref))
    pl.when(jnp.logical_not(_is_prompt_tile()))(lambda: finish(ys_ref))


def _out_proj(y_prompt, y_sample, w, layer, g, g_ffn, h):
    return pl.pallas_call(
        _out_proj_kernel,
        grid=(N_TOK // TM_OUT,),
        in_specs=[
            pl.BlockSpec((TM_OUT, D_MODEL), lambda i: (jnp.minimum(i, N_PROMPT_TILES - 1), 0)),
            pl.BlockSpec((TM_OUT, D_MODEL), lambda i: (0, 0)),
            pl.BlockSpec((None, D_MODEL, D_MODEL), lambda i: (layer, 0, 0)),
            pl.BlockSpec((1, D_MODEL), lambda i: (0, 0)),
            pl.BlockSpec((1, D_MODEL), lambda i: (0, 0)),
            pl.BlockSpec((TM_OUT, D_MODEL), lambda i: (i, 0)),
        ],
        out_specs=[
            pl.BlockSpec((TM_OUT, D_MODEL), lambda i: (i, 0)),
            pl.BlockSpec((TM_OUT, D_MODEL), lambda i: (i, 0)),
        ],
        out_shape=[
            jax.ShapeDtypeStruct((N_TOK, D_MODEL), F32),
            jax.ShapeDtypeStruct((N_TOK, D_MODEL), BF16),
        ],
        compiler_params=_params("parallel"),
        name="out_proj_norm_residual",
    )(y_prompt, y_sample, w, g.reshape(1, D_MODEL), g_ffn.reshape(1, D_MODEL), h)


N_FF_CHUNKS = D_FF // TF
HEAD_SPLIT = 2
TF_HEAD = TF // HEAD_SPLIT


def _ffn_accumulate(k, f_ref, wu_ref, wd_ref, acc_ref, wub_ref=None, wdb_ref=None):
    @pl.when(k == 0)
    def _():
        acc_ref[...] = jnp.zeros_like(acc_ref)

    wu = wu_ref[...].astype(BF16)
    wd = wd_ref[...].astype(BF16)
    if wub_ref is not None:
        wub_ref[...] = wu
        wdb_ref[...] = wd
    u = jnp.square(jnp.maximum(_dot(f_ref[...], wu), 0.0)).astype(BF16)
    acc_ref[...] += _dot(u, wd)


def _ffn_epilogue(h_ref, acc_ref, g2_ref, o_ref, gn_ref=None, a_ref=None):
    for r in range(0, TM_OUT, SLAB):
        rows = slice(r, r + SLAB)
        hn = h_ref[rows, :] + _rms_scale(acc_ref[rows, :], g2_ref[...])
        o_ref[rows, :] = hn
        if a_ref is not None:
            a_ref[rows, :] = _rms_scale(hn, gn_ref[...]).astype(BF16)


def _ffn_head_kernel(f_ref, h_ref, wu_ref, wd_ref, g2_ref, *refs, last_layer):
    if last_layer:
        o_ref, wub_ref, wdb_ref, acc_ref = refs
        gn_ref = a_ref = None
    else:
        gn_ref, o_ref, a_ref, wub_ref, wdb_ref, acc_ref = refs
    k = pl.program_id(0)
    _ffn_accumulate(k, f_ref, wu_ref, wd_ref, acc_ref, wub_ref, wdb_ref)
    pl.when(k == pl.num_programs(0) - 1)(
        functools.partial(_ffn_epilogue, h_ref, acc_ref, g2_ref, o_ref, gn_ref, a_ref))


def _ffn_body_kernel(f_ref, h_ref, wu_ref, wd_ref, g2_ref, *refs, last_layer):
    k = pl.program_id(1)
    is_last = k == pl.num_programs(1) - 1
    if last_layer:
        yp_ref, ys_ref, acc_ref = refs
        _ffn_accumulate(k, f_ref, wu_ref, wd_ref, acc_ref)
        is_prompt = pl.program_id(0) + 1 < N_PROMPT_TILES
        pl.when(jnp.logical_and(is_last, is_prompt))(
            functools.partial(_ffn_epilogue, h_ref, acc_ref, g2_ref, yp_ref))
        pl.when(jnp.logical_and(is_last, jnp.logical_not(is_prompt)))(
            functools.partial(_ffn_epilogue, h_ref, acc_ref, g2_ref, ys_ref))
    else:
        gn_ref, o_ref, a_ref, acc_ref = refs
        _ffn_accumulate(k, f_ref, wu_ref, wd_ref, acc_ref)
        pl.when(is_last)(functools.partial(_ffn_epilogue, h_ref, acc_ref, g2_ref, o_ref, gn_ref, a_ref))


def _ffn(f, h, w_up, w_down, layer, g2, g_next=None):
    last_layer = g_next is None
    gains = [g2.reshape(1, D_MODEL)] + ([] if last_layer else [g_next.reshape(1, D_MODEL)])
    acc = pltpu.VMEM((TM_OUT, D_MODEL), F32)
    rounded_shapes = [
        jax.ShapeDtypeStruct((N_FF_CHUNKS, D_MODEL, TF), BF16),
        jax.ShapeDtypeStruct((D_FF, D_MODEL), BF16),
    ]
    if last_layer:
        act_shapes = [jax.ShapeDtypeStruct((N_PROMPT, D_MODEL), F32)]
    else:
        act_shapes = [jax.ShapeDtypeStruct((N_TOK, D_MODEL), F32), jax.ShapeDtypeStruct((N_TOK, D_MODEL), BF16)]

    tile0 = lambda **kw: pl.BlockSpec((TM_OUT, D_MODEL), lambda k: (0, 0), **kw)
    head = pl.pallas_call(
        functools.partial(_ffn_head_kernel, last_layer=last_layer),
        grid=(N_FF_CHUNKS * HEAD_SPLIT,),
        in_specs=[
            tile0(pipeline_mode=pl.Buffered(1)), tile0(pipeline_mode=pl.Buffered(1)),
            pl.BlockSpec((None, D_MODEL, TF_HEAD), lambda k: (layer, 0, k)),
            pl.BlockSpec((None, TF_HEAD, D_MODEL), lambda k: (layer, k, 0)),
        ] + [pl.BlockSpec((1, D_MODEL), lambda k: (0, 0))] * len(gains),
        out_specs=[tile0() for _ in act_shapes] + [
            pl.BlockSpec((None, D_MODEL, TF_HEAD), lambda k: (k // HEAD_SPLIT, 0, k % HEAD_SPLIT)),
            pl.BlockSpec((TF_HEAD, D_MODEL), lambda k: (k, 0)),
        ],
        out_shape=act_shapes + rounded_shapes,
        scratch_shapes=[acc],
        compiler_params=_params("arbitrary"),
        name="relu2_mlp_head",
    )(f, h, w_up, w_down, *gains)
    *acts, wu_b, wd_b = head

    tile = lambda: pl.BlockSpec((TM_OUT, D_MODEL), lambda i, k: (i + 1, 0))
    in_specs = [
        tile(), tile(),
        pl.BlockSpec((None, D_MODEL, TF), lambda i, k: (k, 0, 0)),
        pl.BlockSpec((TF, D_MODEL), lambda i, k: (k, 0)),
    ] + [pl.BlockSpec((1, D_MODEL), lambda i, k: (0, 0))] * len(gains)
    if last_layer:
        out_specs = [
            pl.BlockSpec((TM_OUT, D_MODEL), lambda i, k: (jnp.minimum(i + 1, N_PROMPT_TILES - 1), 0)),
            pl.BlockSpec((TM_OUT, D_MODEL), lambda i, k: (0, 0)),
        ]
        out_shape = act_shapes + [jax.ShapeDtypeStruct((N_SAMPLE, D_MODEL), F32)]
    else:
        out_specs = [tile(), tile()]
        out_shape = act_shapes
    n_in = len(in_specs)
    return pl.pallas_call(
        _skip_aliased(functools.partial(_ffn_body_kernel, last_layer=last_layer), n_in, len(acts)),
        grid=(N_TOK // TM_OUT - 1, N_FF_CHUNKS),
        in_specs=in_specs + [pl.BlockSpec(memory_space=pl.ANY)] * len(acts),
        out_specs=out_specs,
        out_shape=out_shape,
        input_output_aliases={n_in + j: j for j in range(len(acts))},
        scratch_shapes=[acc],
        compiler_params=_params("arbitrary", "arbitrary"),
        name="relu2_mlp_body",
    )(f, h, wu_b, wd_b, *gains, *acts)


def _rotary_tables(pos):
    half = RET_DK // 2
    inv = ROPE_BASE ** (-np.arange(half, dtype=np.float64) / half)
    ang = np.asarray(pos, np.float64)[:, None] * inv[None, :]
    cos = np.concatenate([np.cos(ang), np.cos(ang)], axis=-1)
    sin = np.concatenate([-np.sin(ang), np.sin(ang)], axis=-1)
    return jnp.asarray(cos, F32), jnp.asarray(sin, F32)


def _log_gamma():
    return np.log1p(-np.exp2(-5.0 - np.arange(RET_HEADS, dtype=np.float64)))


def _retention_tables(length):
    log_g = _log_gamma()
    idx = np.arange(length, dtype=np.float64)
    diff = idx[:, None] - idx[None, :]
    mask = np.where(diff[None] >= 0, np.exp(log_g[:, None, None] * np.maximum(diff, 0.0)[None]), 0.0)
    qd = np.exp(log_g[:, None] * (idx + 1.0))
    kd = np.exp(log_g[:, None] * (length - 1.0 - idx))
    cd = np.exp(log_g * length)
    return mask, qd, kd, cd


def _lane_bcast(a, width):
    return np.repeat(a[..., None], width, axis=-1)


N_PROMPT_CHUNKS = N_PROMPT // CHUNK
PROJ_COLS = 512


def _projected_chunk(step):
    return jnp.minimum(step, N_PROMPT_CHUNKS - 1)


def _mixed_chunk(step):
    return jnp.maximum(step - 1, 0)


def _interleave(first, second):
    i = j = 0
    while i < len(first) or j < len(second):
        if j >= len(second) or (i < len(first) and i * len(second) <= j * len(first)):
            first[i]()
            i += 1
        else:
            second[j]()
            j += 1


def _skewed(a_ref, w_ref, z_refs, n_cols, mix):
    step = pl.program_id(0)
    n = (step + N_CHUNKS - 1) % N_CHUNKS

    @pl.when(step == 0)
    def _():
        z_refs[1][...] = jnp.zeros_like(z_refs[1])

    for parity in range(2):
        z_next, z_mixed = z_refs[parity], z_refs[1 - parity]

        def body(z_next=z_next, z_mixed=z_mixed):
            a = a_ref[...]

            def project(c):
                z_next[:, c:c + PROJ_COLS] = _dot(a, w_ref[:, c:c + PROJ_COLS])

            mix(step, n, z_mixed, [functools.partial(project, c) for c in range(0, n_cols, PROJ_COLS)])

        pl.when(step % 2 == parity)(body)


def _retention_head_out(o, gate):
    o = o * lax.rsqrt(jnp.mean(o * o, axis=-1, keepdims=True) + EPS)
    return gate * jax.nn.sigmoid(gate) * o


def _rotate(x, cos, sin):
    return x * cos + pltpu.roll(x, RET_DK // 2, 1) * sin


def _even_prompt_kernel(a_ref, w_ref, cos_ref, sin_ref, mask_ref, qd_ref, kd_ref, wp_ref, sp_ref,
                        y_ref, hist_ref, st_ref, z0_ref, z1_ref, pext_ref, s_ref, *, chunk_decay):
    step = pl.program_id(0)
    n = (step + N_CHUNKS - 1) % N_CHUNKS

    @pl.when(jnp.logical_or(n == 0, step == 0))
    def _():
        pext_ref[0:POOL_PAD, :] = jnp.zeros((POOL_PAD, POOL_WIDTH), F32)
        s_ref[...] = jnp.zeros_like(s_ref)

    _skewed(a_ref, w_ref, (z0_ref, z1_ref), EVEN_IN,
            functools.partial(_even_prompt_mix, cos_ref, sin_ref, mask_ref, qd_ref, kd_ref, wp_ref, sp_ref, y_ref,
                              pext_ref, s_ref, chunk_decay))

    @pl.when(jnp.logical_and(n == N_CHUNKS - 1, step > 0))
    def _():
        hist_ref[0] = pext_ref[POOL_PAD + CHUNK - POOL_HIST:POOL_PAD + CHUNK, :]
        st_ref[0] = s_ref[...]


def _even_prompt_mix(cos_ref, sin_ref, mask_ref, qd_ref, kd_ref, wp_ref, sp_ref, y_ref, pext_ref, s_ref,
                     chunk_decay, step, n, zc, project):
    def pool():
        pext_ref[POOL_PAD:POOL_PAD + CHUNK, :] = zc[:, 0:POOL_WIDTH]
        pos = n * CHUNK + lax.broadcasted_iota(jnp.int32, (CHUNK, 1), 0)
        for g, w in enumerate(POOL_WINDOWS):
            lanes = slice(g * POOL_GROUP_DIM, (g + 1) * POOL_GROUP_DIM)
            p = pext_ref[POOL_PAD:POOL_PAD + CHUNK, lanes]
            acc = p
            for i in range(1, w):
                acc = acc + pext_ref[POOL_PAD - i:POOL_PAD - i + CHUNK, lanes]
            cnt = jnp.minimum(w, pos + 1).astype(F32)
            d = acc / cnt - p
            yg = _dot(d.astype(BF16), wp_ref[g].astype(BF16)) * sp_ref[:, lanes]
            y_ref[:, lanes] = yg.astype(BF16)
        pext_ref[0:POOL_PAD, :] = pext_ref[CHUNK:CHUNK + POOL_PAD, :]

    live = {}

    def scores_stage(h):
        cos = cos_ref[...]
        sin = sin_ref[...]
        q = _rotate(zc[:, Q_OFF + h * RET_DK:Q_OFF + (h + 1) * RET_DK], cos, sin)
        k = _rotate(zc[:, K_OFF + h * RET_DK:K_OFF + (h + 1) * RET_DK], cos, sin) * (RET_DK ** -0.5)
        v = zc[:, V_OFF + h * RET_DV:V_OFF + (h + 1) * RET_DV].astype(BF16)
        live[h] = (_dot_nt(q.astype(BF16), k.astype(BF16)), (q * qd_ref[h]).astype(BF16),
                   (k * kd_ref[h]).astype(BF16), v)

    def output_stage(h):
        scores, q_dec, k_dec, v = live[h]
        s = s_ref[h]
        o = _dot((scores * mask_ref[h]).astype(BF16), v) + _dot(q_dec, s.astype(BF16))
        s_ref[h] = s * chunk_decay[h] + _dot_tn(k_dec, v)
        live[h] = o

    def norm_stage(h):
        gate = zc[:, G_OFF + h * RET_DV:G_OFF + (h + 1) * RET_DV]
        y_ref[:, POOL_WIDTH + h * RET_DV:POOL_WIDTH + (h + 1) * RET_DV] = (
            _retention_head_out(live.pop(h), gate).astype(BF16))

    def heads_step(i):
        for stage, h in ((norm_stage, i - 2), (output_stage, i - 1), (scores_stage, i)):
            if 0 <= h < RET_HEADS:
                stage(h)

    _interleave(project, [pool] + [functools.partial(heads_step, i) for i in range(RET_HEADS + 2)])


def _even_prompt(a, w_in, w_pool, s_pool, layer, prev_hist, prev_state):
    cos, sin = _rotary_tables(np.arange(SEQ))
    mask, qd, kd, cd = _retention_tables(CHUNK)
    const = lambda *shape: pl.BlockSpec(shape, lambda t: (0,) * len(shape))
    return _call_stacked(
        functools.partial(_even_prompt_kernel, chunk_decay=tuple(float(c) for c in cd)),
        name="even_mixer_prompt",
        grid=(N_PROMPT_CHUNKS + 1,),
        in_specs=[
            pl.BlockSpec((CHUNK, D_MODEL), lambda t: (_projected_chunk(t), 0)),
            pl.BlockSpec((D_MODEL, EVEN_IN), lambda t: (0, 0), pipeline_mode=pl.Buffered(1)),
            pl.BlockSpec((CHUNK, RET_DK), lambda t: (_mixed_chunk(t) % N_CHUNKS, 0)),
            pl.BlockSpec((CHUNK, RET_DK), lambda t: (_mixed_chunk(t) % N_CHUNKS, 0)),
            const(RET_HEADS, CHUNK, CHUNK),
            const(RET_HEADS, CHUNK, RET_DK),
            const(RET_HEADS, CHUNK, RET_DK),
            pl.BlockSpec((None, POOL_GROUPS, POOL_GROUP_DIM, POOL_GROUP_DIM), lambda t: (layer, 0, 0, 0)),
            pl.BlockSpec((None, 1, POOL_WIDTH), lambda t: (layer, 0, 0)),
        ],
        args=(a, w_in, cos, sin, jnp.asarray(mask, F32), jnp.asarray(_lane_bcast(qd, RET_DK), F32),
              jnp.asarray(_lane_bcast(kd, RET_DK), F32), w_pool, s_pool.reshape(N_EVEN, 1, POOL_WIDTH)),
        out_specs=[
            pl.BlockSpec((CHUNK, D_MODEL), lambda t: (_mixed_chunk(t), 0)),
            pl.BlockSpec((None, 1, POOL_HIST, POOL_WIDTH), lambda t: (layer, _mixed_chunk(t) // N_CHUNKS, 0, 0)),
            pl.BlockSpec((None, 1, RET_HEADS, RET_DK, RET_DV),
                         lambda t: (layer, _mixed_chunk(t) // N_CHUNKS, 0, 0, 0)),
        ],
        out_shape=[
            jax.ShapeDtypeStruct((N_PROMPT, D_MODEL), BF16),
            jax.ShapeDtypeStruct((N_EVEN, BATCH, POOL_HIST, POOL_WIDTH), F32),
            jax.ShapeDtypeStruct((N_EVEN, BATCH, RET_HEADS, RET_DK, RET_DV), F32),
        ],
        stacked={1: prev_hist, 2: prev_state},
        sem=("arbitrary",),
        scratch_shapes=[
            pltpu.VMEM((CHUNK, EVEN_IN), F32),
            pltpu.VMEM((CHUNK, EVEN_IN), F32),
            pltpu.VMEM((POOL_PAD + CHUNK, POOL_WIDTH), F32),
            pltpu.VMEM((RET_HEADS, RET_DK, RET_DV), F32),
        ],
    )


def _position_slabs(x):
    return x.reshape(SAMPLE_ROWS, x.shape[-1])


def _even_sample_kernel(z_ref, cos_ref, sin_ref, mask_ref, qd_ref, kd_ref, wp_ref, sp_ref, hist_ref, st_ref,
                        y_ref, nhist_ref, nst_ref, *, chunk_decay):
    def ext(r, lanes):
        return hist_ref[r, :, lanes] if r < POOL_HIST else z_ref[r - POOL_HIST, :, lanes]

    for g, w in enumerate(POOL_WINDOWS):
        lanes = slice(g * POOL_GROUP_DIM, (g + 1) * POOL_GROUP_DIM)
        d = []
        for t in range(DEC_SEQ):
            p = ext(POOL_HIST + t, lanes)
            acc = p
            for i in range(1, w):
                acc = acc + ext(POOL_HIST + t - i, lanes)
            d.append(acc / float(min(w, PAST_LEN + t + 1)) - p)
        d = jnp.concatenate(d, axis=0)
        yg = _dot(d.astype(BF16), wp_ref[g].astype(BF16)) * sp_ref[:, lanes]
        y_ref[:, :, lanes] = yg.reshape(DEC_SEQ, SAMPLE_BB, POOL_GROUP_DIM)
    for r in range(POOL_HIST):
        nhist_ref[r] = hist_ref[r + DEC_SEQ] if r + DEC_SEQ < POOL_HIST else z_ref[r + DEC_SEQ - POOL_HIST, :,
                                                                                 0:POOL_WIDTH]

    cos = cos_ref[...]
    sin = sin_ref[...]
    seq_of_row = lax.broadcasted_iota(jnp.int32, (SAMPLE_ROWS, 1), 0) % SAMPLE_BB
    for h in range(RET_HEADS):
        q = _rotate(_position_slabs(z_ref[:, :, Q_OFF + h * RET_DK:Q_OFF + (h + 1) * RET_DK]), cos, sin)
        k = _rotate(_position_slabs(z_ref[:, :, K_OFF + h * RET_DK:K_OFF + (h + 1) * RET_DK]), cos, sin)
        k = k * (RET_DK ** -0.5)
        v = _position_slabs(z_ref[:, :, V_OFF + h * RET_DV:V_OFF + (h + 1) * RET_DV]).astype(BF16)
        gate = _position_slabs(z_ref[:, :, G_OFF + h * RET_DV:G_OFF + (h + 1) * RET_DV])
        scores = _dot_nt(q.astype(BF16), k.astype(BF16)) * mask_ref[h]
        o = _dot(scores.astype(BF16), v)
        q_dec = (q * qd_ref[h]).astype(BF16)
        k_dec = k * kd_ref[h]
        for b in range(SAMPLE_BB):
            own = seq_of_row == b
            s = st_ref[b, h]
            o = o + jnp.where(own, _dot(q_dec, s.astype(BF16)), 0.0)
            nst_ref[b, h] = s * chunk_decay[h] + _dot_tn(jnp.where(own, k_dec, 0.0).astype(BF16), v)
        y_ref[:, :, POOL_WIDTH + h * RET_DV:POOL_WIDTH + (h + 1) * RET_DV] = (
            _retention_head_out(o, gate).reshape(DEC_SEQ, SAMPLE_BB, RET_DV))


def _even_sample(z, w_pool, s_pool, state_pool, state_ret, layer, prev_hist, prev_state):
    pos = PAST_LEN + np.repeat(np.arange(DEC_SEQ), SAMPLE_BB)
    cos, sin = _rotary_tables(pos)
    mask, qd, kd, cd = _retention_tables(DEC_SEQ)
    mask = np.stack([np.kron(m, np.eye(SAMPLE_BB)) for m in mask])
    qd = _lane_bcast(np.repeat(qd, SAMPLE_BB, axis=1), RET_DK)
    kd = _lane_bcast(np.repeat(kd, SAMPLE_BB, axis=1), RET_DK)
    const = lambda *shape: pl.BlockSpec(shape, lambda i: (0,) * len(shape))
    return _call_stacked(
        functools.partial(_even_sample_kernel, chunk_decay=tuple(float(c) for c in cd)),
        name="even_mixer_sample",
        grid=(DEC_BATCH // SAMPLE_BB,),
        in_specs=[
            pl.BlockSpec((DEC_SEQ, SAMPLE_BB, EVEN_IN), lambda i: (0, i, 0)),
            const(SAMPLE_ROWS, RET_DK),
            const(SAMPLE_ROWS, RET_DK),
            const(RET_HEADS, SAMPLE_ROWS, SAMPLE_ROWS),
            const(RET_HEADS, SAMPLE_ROWS, RET_DK),
            const(RET_HEADS, SAMPLE_ROWS, RET_DK),
            pl.BlockSpec((None, POOL_GROUPS, POOL_GROUP_DIM, POOL_GROUP_DIM), lambda i: (layer, 0, 0, 0)),
            pl.BlockSpec((None, 1, POOL_WIDTH), lambda i: (layer, 0, 0)),
            pl.BlockSpec((None, POOL_HIST, SAMPLE_BB, POOL_WIDTH), lambda i: (layer, 0, i, 0)),
            pl.BlockSpec((None, SAMPLE_BB, RET_HEADS, RET_DK, RET_DV), lambda i: (layer, i, 0, 0, 0)),
        ],
        args=(z, cos, sin, jnp.asarray(mask, F32), jnp.asarray(qd, F32), jnp.asarray(kd, F32),
              w_pool, s_pool.reshape(N_EVEN, 1, POOL_WIDTH), state_pool, state_ret),
        out_specs=[
            pl.BlockSpec((DEC_SEQ, SAMPLE_BB, D_MODEL), lambda i: (0, i, 0)),
            pl.BlockSpec((None, POOL_HIST, SAMPLE_BB, POOL_WIDTH), lambda i: (layer, 0, i, 0)),
            pl.BlockSpec((None, SAMPLE_BB, RET_HEADS, RET_DK, RET_DV), lambda i: (layer, i, 0, 0, 0)),
        ],
        out_shape=[
            jax.ShapeDtypeStruct((DEC_SEQ, DEC_BATCH, D_MODEL), F32),
            jax.ShapeDtypeStruct((N_EVEN, POOL_HIST, DEC_BATCH, POOL_WIDTH), F32),
            jax.ShapeDtypeStruct((N_EVEN, DEC_BATCH, RET_HEADS, RET_DK, RET_DV), F32),
        ],
        stacked={1: prev_hist, 2: prev_state},
        sem=("parallel",),
        scratch_shapes=[],
    )


CONV_LANES = 128
SHIFTED_ROWS = CONV_PAD + CHUNK - SUBLANES


def _odd_prompt_kernel(a_ref, w_ref, lng_ref, lnb_ref, sgw_ref, sgb_ref, dw_ref, dwb_ref, cvg_ref, cvb_ref,
                       y_ref, cst_ref, z0_ref, z1_ref, vb_ref, ext_ref, cv_ref, xs_ref):
    step = pl.program_id(0)
    n = (step + N_CHUNKS - 1) % N_CHUNKS

    @pl.when(jnp.logical_or(n == 0, step == 0))
    def _():
        ext_ref[0:CONV_PAD, :] = jnp.zeros((CONV_PAD, CONV_CH), F32)

    _skewed(a_ref, w_ref, (z0_ref, z1_ref), ODD_IN,
            functools.partial(_odd_prompt_mix, lng_ref, lnb_ref, sgw_ref, sgb_ref, dw_ref, dwb_ref, cvg_ref, cvb_ref,
                              y_ref, vb_ref, ext_ref, cv_ref, xs_ref))

    @pl.when(jnp.logical_and(n == N_CHUNKS - 1, step > 0))
    def _():
        cst_ref[0] = ext_ref[CONV_PAD + CHUNK - CONV_HIST:CONV_PAD + CHUNK, :]

    ext_ref[0:CONV_PAD, :] = ext_ref[CHUNK:CHUNK + CONV_PAD, :]


def _odd_prompt_mix(lng_ref, lnb_ref, sgw_ref, sgb_ref, dw_ref, dwb_ref, cvg_ref, cvb_ref,
                    y_ref, vb_ref, ext_ref, cv_ref, xs_ref, step, n, zc, project):
    def gate_values():
        v = _layer_norm(jax.nn.gelu(zc[:, SG_WIDTH:2 * SG_WIDTH]), lng_ref[...], lnb_ref[...])
        vb_ref[...] = v.astype(BF16)

    def gating(g):
        lanes = slice(g * SG_GROUP_DIM, (g + 1) * SG_GROUP_DIM)
        row = lax.broadcasted_iota(jnp.int32, (CHUNK, CHUNK), 0)
        col = lax.broadcasted_iota(jnp.int32, (CHUNK, CHUNK), 1)
        ws = jnp.where(col <= row, sgw_ref[g], 0.0).astype(BF16)
        mixed = _dot(ws, vb_ref[:, lanes]) + sgb_ref[:, g:g + 1]
        y_ref[:, lanes] = (jax.nn.gelu(zc[:, lanes]) * mixed).astype(BF16)

    def glu():
        a = zc[:, 2 * SG_WIDTH:2 * SG_WIDTH + CONV_CH]
        gate = zc[:, 2 * SG_WIDTH + CONV_CH:2 * SG_WIDTH + 2 * CONV_CH]
        ext_ref[CONV_PAD:CONV_PAD + CHUNK, :] = a * jax.nn.sigmoid(gate)

    def shifted_copy(s):
        xs_ref[s - 1] = ext_ref[s:s + SHIFTED_ROWS, :]

    def conv(c):
        lanes = slice(c, c + CONV_LANES)
        acc = jnp.broadcast_to(dwb_ref[:, lanes], (CHUNK, CONV_LANES))
        for j in range(CONV_K):
            tile, s = divmod(CONV_PAD - CONV_HIST + j, SUBLANES)
            rows = slice(tile * SUBLANES, tile * SUBLANES + CHUNK)
            window = ext_ref[rows, lanes] if s == 0 else xs_ref[s - 1, rows, lanes]
            acc = acc + window * dw_ref[j:j + 1, lanes]
        cv_ref[:, lanes] = acc

    def conv_out():
        yd = _layer_norm(cv_ref[...], cvg_ref[...], cvb_ref[...])
        y_ref[:, SG_WIDTH:SG_WIDTH + CONV_CH] = (yd * jax.nn.sigmoid(yd)).astype(BF16)

    _interleave(project, [gate_values, glu] + [functools.partial(shifted_copy, s) for s in range(1, SUBLANES)]
                + [functools.partial(conv, c) for c in range(0, CONV_CH, CONV_LANES)] + [conv_out])
    for g in range(SG_GROUPS):
        gating(g)


def _odd_weight_specs(layer):
    per_layer = lambda *shape: pl.BlockSpec((None,) + shape, lambda *_: (layer,) + (0,) * len(shape))
    return dict(
        ln=per_layer(1, SG_WIDTH),
        sgw=per_layer(SG_GROUPS, SG_CHUNK, SG_CHUNK),
        sgb=per_layer(SG_CHUNK, SG_GROUPS),
        dw=per_layer(CONV_K, CONV_CH),
        ch=per_layer(1, CONV_CH),
    )


def _odd_prompt(a, w_in, sg_ln_g, sg_ln_b, sg_w, sg_b, dw_w, dw_b, cv_ln_g, cv_ln_b, layer, prev_cst):
    spec = _odd_weight_specs(layer)
    row = lambda x: x.reshape(N_ODD, 1, -1)
    return _call_stacked(
        _odd_prompt_kernel,
        name="odd_mixer_prompt",
        grid=(N_PROMPT_CHUNKS + 1,),
        in_specs=[
            pl.BlockSpec((CHUNK, D_MODEL), lambda t: (_projected_chunk(t), 0)),
            pl.BlockSpec((D_MODEL, ODD_IN), lambda t: (0, 0), pipeline_mode=pl.Buffered(1)),
            spec["ln"], spec["ln"], spec["sgw"], spec["sgb"], spec["dw"], spec["ch"], spec["ch"], spec["ch"],
        ],
        args=(a, w_in, row(sg_ln_g), row(sg_ln_b), sg_w, jnp.swapaxes(sg_b, 1, 2), dw_w, row(dw_b),
              row(cv_ln_g), row(cv_ln_b)),
        out_specs=[
            pl.BlockSpec((CHUNK, D_MODEL), lambda t: (_mixed_chunk(t), 0)),
            pl.BlockSpec((None, 1, CONV_HIST, CONV_CH), lambda t: (layer, _mixed_chunk(t) // N_CHUNKS, 0, 0)),
        ],
        out_shape=[
            jax.ShapeDtypeStruct((N_PROMPT, D_MODEL), BF16),
            jax.ShapeDtypeStruct((N_ODD, BATCH, CONV_HIST, CONV_CH), F32),
        ],
        stacked={1: prev_cst},
        sem=("arbitrary",),
        scratch_shapes=[
            pltpu.VMEM((CHUNK, ODD_IN), F32),
            pltpu.VMEM((CHUNK, ODD_IN), F32),
            pltpu.VMEM((CHUNK, SG_WIDTH), BF16),
            pltpu.VMEM((CONV_PAD + CHUNK, CONV_CH), F32),
            pltpu.VMEM((CHUNK, CONV_CH), F32),
            pltpu.VMEM((SUBLANES - 1, SHIFTED_ROWS, CONV_CH), F32),
        ],
    )


def _odd_sample_kernel(sgw_ref, sgb_ref, z_ref, lng_ref, lnb_ref, dw_ref, dwb_ref, cvg_ref, cvb_ref, cst_ref,
                       y_ref, sgv_ref, ncst_ref, cv_ref):
    v = _layer_norm(jax.nn.gelu(_position_slabs(z_ref[:, :, SG_WIDTH:2 * SG_WIDTH])), lng_ref[...], lnb_ref[...])
    sgv_ref[...] = v.reshape(DEC_SEQ, SAMPLE_BB, SG_WIDTH)
    for g in range(SG_GROUPS):
        lanes = slice(g * SG_GROUP_DIM, (g + 1) * SG_GROUP_DIM)
        for t in range(DEC_SEQ):
            mixed = jnp.full((SAMPLE_BB, SG_GROUP_DIM), sgb_ref[g * DEC_SEQ + t], F32)
            for j in range(t + 1):
                mixed = mixed + sgw_ref[(g * DEC_SEQ + t) * DEC_SEQ + j] * sgv_ref[j, :, lanes]
            y_ref[t, :, lanes] = jax.nn.gelu(z_ref[t, :, lanes]) * mixed

    def ext(r, lanes):
        if r < CONV_HIST:
            return cst_ref[r, :, lanes]
        a = z_ref[r - CONV_HIST, :, pl.ds(2 * SG_WIDTH + lanes.start, CONV_LANES)]
        gate = z_ref[r - CONV_HIST, :, pl.ds(2 * SG_WIDTH + CONV_CH + lanes.start, CONV_LANES)]
        return a * jax.nn.sigmoid(gate)

    for c in range(0, CONV_CH, CONV_LANES):
        lanes = slice(c, c + CONV_LANES)
        rows = [ext(r, lanes) for r in range(CONV_HIST + DEC_SEQ)]
        for t in range(DEC_SEQ):
            acc = jnp.broadcast_to(dwb_ref[:, lanes], (SAMPLE_BB, CONV_LANES))
            for j in range(CONV_K):
                acc = acc + rows[t + j] * dw_ref[j:j + 1, lanes]
            cv_ref[t, :, lanes] = acc
        for r in range(CONV_HIST):
            ncst_ref[r, :, lanes] = rows[r + DEC_SEQ]
    yd = _layer_norm(_position_slabs(cv_ref[...]), cvg_ref[...], cvb_ref[...])
    y_ref[:, :, SG_WIDTH:SG_WIDTH + CONV_CH] = (yd * jax.nn.sigmoid(yd)).reshape(DEC_SEQ, SAMPLE_BB, CONV_CH)


def _odd_sample(z, sg_ln_g, sg_ln_b, sg_w, sg_b, dw_w, dw_b, cv_ln_g, cv_ln_b, state_conv, layer,
                prev_sgv, prev_cst):
    spec = _odd_weight_specs(layer)
    row = lambda x: x.reshape(N_ODD, 1, -1)
    smem = pl.BlockSpec(memory_space=pltpu.SMEM)
    return _call_stacked(
        _odd_sample_kernel,
        name="odd_mixer_sample",
        grid=(DEC_BATCH // SAMPLE_BB,),
        in_specs=[
            smem,
            smem,
            pl.BlockSpec((DEC_SEQ, SAMPLE_BB, ODD_IN), lambda i: (0, i, 0)),
            spec["ln"], spec["ln"], spec["dw"], spec["ch"], spec["ch"], spec["ch"],
            pl.BlockSpec((None, CONV_HIST, SAMPLE_BB, CONV_CH), lambda i: (layer, 0, i, 0)),
        ],
        args=(sg_w[layer, :, :DEC_SEQ, :DEC_SEQ].reshape(-1), sg_b[layer, :, :DEC_SEQ].reshape(-1), z,
              row(sg_ln_g), row(sg_ln_b), dw_w, row(dw_b), row(cv_ln_g), row(cv_ln_b), state_conv),
        out_specs=[
            pl.BlockSpec((DEC_SEQ, SAMPLE_BB, D_MODEL), lambda i: (0, i, 0)),
            pl.BlockSpec((None, DEC_SEQ, SAMPLE_BB, SG_WIDTH), lambda i: (layer, 0, i, 0)),
            pl.BlockSpec((None, CONV_HIST, SAMPLE_BB, CONV_CH), lambda i: (layer, 0, i, 0)),
        ],
        out_shape=[
            jax.ShapeDtypeStruct((DEC_SEQ, DEC_BATCH, D_MODEL), F32),
            jax.ShapeDtypeStruct((N_ODD, DEC_SEQ, DEC_BATCH, SG_WIDTH), F32),
            jax.ShapeDtypeStruct((N_ODD, CONV_HIST, DEC_BATCH, CONV_CH), F32),
        ],
        stacked={1: prev_sgv, 2: prev_cst},
        sem=("parallel",),
        scratch_shapes=[pltpu.VMEM((DEC_SEQ, SAMPLE_BB, CONV_CH), F32)],
    )


def _position_major(x):
    return jnp.swapaxes(x, -3, -2)


def kernel(x_prompt, x_sample, state_pool, state_ret, state_conv, norm_mix_pre, norm_mix_post, norm_ffn_pre, norm_ffn_post, w_in_even, w_pool, s_pool, w_out_even, w_in_odd, sg_ln_g, sg_ln_b, sg_w, sg_b, dw_w, dw_b, cv_ln_g, cv_ln_b, w_out_odd, w_up, w_down):
    w_out_even, w_out_odd = w_out_even.astype(BF16), w_out_odd.astype(BF16)

    state_pool = _position_major(state_pool)
    state_conv = _position_major(state_conv)
    h, a = _prenorm(x_prompt.reshape(N_PROMPT, D_MODEL), _position_major(x_sample).reshape(N_SAMPLE, D_MODEL),
                    norm_mix_pre[0])
    pool_p = pool_s = ret_p = ret_s = conv_p = conv_s = sgv_s = None
    for l in range(DEPTH):
        i = l // 2
        if l % 2 == 0:
            z_s, w_in = _in_proj_sample(a, w_in_even, i)
            y_p, pool_p, ret_p = _even_prompt(a, w_in, w_pool, s_pool, i, pool_p, ret_p)
            y_s, pool_s, ret_s = _even_sample(z_s.reshape(DEC_SEQ, DEC_BATCH, EVEN_IN), w_pool, s_pool, state_pool,
                                              state_ret, i, pool_s, ret_s)
            w_out = w_out_even
        else:
            z_s, w_in = _in_proj_sample(a, w_in_odd, i)
            y_p, conv_p = _odd_prompt(a, w_in, sg_ln_g, sg_ln_b, sg_w, sg_b, dw_w, dw_b, cv_ln_g, cv_ln_b,
                                      i, conv_p)
            y_s, sgv_s, conv_s = _odd_sample(z_s.reshape(DEC_SEQ, DEC_BATCH, ODD_IN), sg_ln_g, sg_ln_b, sg_w, sg_b,
                                             dw_w, dw_b, cv_ln_g, cv_ln_b, state_conv, i, sgv_s, conv_s)
            w_out = w_out_odd
        h, f = _out_proj(y_p, y_s.reshape(N_SAMPLE, D_MODEL), w_out, i, norm_mix_post[l], norm_ffn_pre[l], h)
        if l + 1 < DEPTH:
            h, a = _ffn(f, h, w_up, w_down, l, norm_ffn_post[l], norm_mix_pre[l + 1])
        else:
            y_prompt, y_sample = _ffn(f, h, w_up, w_down, l, norm_ffn_post[l])

    y_sample = _position_major(y_sample.reshape(DEC_SEQ, DEC_BATCH, D_MODEL))
    return (y_prompt.reshape(BATCH, SEQ, D_MODEL), y_sample, pool_p, _position_major(pool_s), ret_p, ret_s,
            conv_p, _position_major(conv_s), _position_major(sgv_s))
```

```python
import functools

import jax
import jax.numpy as jnp
import numpy as np
from jax import lax
from jax.experimental import pallas as pl
from jax.experimental.pallas import tpu as pltpu

F32 = jnp.float32
BF16 = jnp.bfloat16

D_MODEL = 2048
BATCH = 4
SEQ = 2048
DEPTH = 4
DEC_BATCH = 128
DEC_SEQ = 4
PAST_LEN = 16384

N_EVEN = (DEPTH + 1) // 2
N_ODD = DEPTH // 2

POOL_WINDOWS = (2, 4, 8, 16)
POOL_GROUPS = len(POOL_WINDOWS)
POOL_WIDTH = D_MODEL // 4
POOL_GROUP_DIM = POOL_WIDTH // POOL_GROUPS
POOL_HIST = max(POOL_WINDOWS) - 1
RET_WIDTH = D_MODEL - POOL_WIDTH
RET_HEADS = 6
RET_DV = RET_WIDTH // RET_HEADS
RET_DK = RET_DV // 2
RET_QK = RET_HEADS * RET_DK
RET_CHUNK = 128
ROPE_BASE = 10000.0
SG_WIDTH = D_MODEL // 2
SG_CHUNK = 128
SG_GROUPS = 4
SG_GROUP_DIM = SG_WIDTH // SG_GROUPS
CONV_CH = D_MODEL // 2
CONV_K = 31
CONV_HIST = CONV_K - 1
D_FF = 4 * D_MODEL
EPS = 1e-6

EVEN_IN = POOL_WIDTH + 2 * RET_QK + 2 * RET_WIDTH
ODD_IN = 2 * SG_WIDTH + 2 * CONV_CH

Q_OFF = POOL_WIDTH
K_OFF = Q_OFF + RET_QK
V_OFF = K_OFF + RET_QK
G_OFF = V_OFF + RET_WIDTH

N_PROMPT = BATCH * SEQ
N_SAMPLE = DEC_BATCH * DEC_SEQ
N_TOK = N_PROMPT + N_SAMPLE

SUBLANES = 8
CHUNK = 128
N_CHUNKS = SEQ // CHUNK
SAMPLE_BB = SUBLANES
SAMPLE_ROWS = SAMPLE_BB * DEC_SEQ
POOL_PAD = 16
CONV_PAD = 32

TN_IN = 1024
TM_OUT = 512
SLAB = 128
TF = 1024
VMEM_LIMIT = 56 * 1024 * 1024

N_PROMPT_TILES = N_PROMPT // TM_OUT
assert N_PROMPT % TM_OUT == 0 and N_SAMPLE == TM_OUT and TM_OUT % SLAB == 0
assert POOL_PAD >= POOL_HIST and CONV_PAD >= CONV_HIST and DEC_BATCH % SAMPLE_BB == 0


def _params(*sem):
    return pltpu.CompilerParams(dimension_semantics=sem, vmem_limit_bytes=VMEM_LIMIT)


def _rms_scale(x, g):
    return x * lax.rsqrt(jnp.mean(x * x, axis=-1, keepdims=True) + EPS) * g


def _layer_norm(x, g, b):
    xc = x - jnp.mean(x, axis=-1, keepdims=True)
    return xc * lax.rsqrt(jnp.mean(xc * xc, axis=-1, keepdims=True) + EPS) * g + b


def _dot(a, b):
    return jnp.dot(a, b, preferred_element_type=F32)


def _dot_nt(a, b):
    return lax.dot_general(a, b, (((1,), (1,)), ((), ())), preferred_element_type=F32)


def _dot_tn(a, b):
    return lax.dot_general(a, b, (((0,), (0,)), ((), ())), preferred_element_type=F32)


def _skip_aliased(body, n_in, n_aliased):
    def wrapped(*refs):
        return body(*refs[:n_in], *refs[n_in + n_aliased:])
    return wrapped


def _call_stacked(body, *, name, grid, in_specs, args, out_specs, out_shape, stacked, sem, scratch_shapes):
    prev = [(o, p) for o, p in sorted(stacked.items()) if p is not None]
    n_in = len(args)
    return pl.pallas_call(
        _skip_aliased(body, n_in, len(prev)),
        grid=grid,
        in_specs=list(in_specs) + [pl.BlockSpec(memory_space=pl.ANY)] * len(prev),
        out_specs=out_specs,
        out_shape=out_shape,
        input_output_aliases={n_in + j: o for j, (o, _) in enumerate(prev)},
        scratch_shapes=scratch_shapes,
        compiler_params=_params(*sem),
        name=name,
    )(*args, *[p for _, p in prev])


def _is_prompt_tile():
    return pl.program_id(0) < N_PROMPT_TILES


def _prenorm_kernel(xp_ref, xs_ref, g_ref, a_ref):
    def emit(x_ref):
        a_ref[...] = _rms_scale(x_ref[...], g_ref[...]).astype(BF16)

    pl.when(_is_prompt_tile())(lambda: emit(xp_ref))
    pl.when(jnp.logical_not(_is_prompt_tile()))(lambda: emit(xs_ref))


def _prenorm(x_prompt, x_sample, g):
    return pl.pallas_call(
        _prenorm_kernel,
        grid=(N_TOK // TM_OUT,),
        in_specs=[
            pl.BlockSpec((TM_OUT, D_MODEL), lambda i: (jnp.minimum(i, N_PROMPT_TILES - 1), 0)),
            pl.BlockSpec((TM_OUT, D_MODEL), lambda i: (0, 0)),
            pl.BlockSpec((1, D_MODEL), lambda i: (0, 0)),
        ],
        out_specs=pl.BlockSpec((TM_OUT, D_MODEL), lambda i: (i, 0)),
        out_shape=jax.ShapeDtypeStruct((N_TOK, D_MODEL), BF16),
        compiler_params=_params("parallel"),
        name="join_prenorm",
    )(x_prompt, x_sample, g.reshape(1, D_MODEL))


def _in_proj_kernel(a_ref, w_ref, o_ref, wb_ref):
    w = w_ref[...].astype(BF16)
    wb_ref[...] = w
    o_ref[...] = _dot(a_ref[...], w)


def _in_proj_sample(a, w, layer):
    n_out = w.shape[2]
    return pl.pallas_call(
        _in_proj_kernel,
        grid=(n_out // TN_IN,),
        in_specs=[
            pl.BlockSpec((N_SAMPLE, D_MODEL), lambda j: (N_PROMPT // N_SAMPLE, 0)),
            pl.BlockSpec((None, D_MODEL, TN_IN), lambda j: (layer, 0, j)),
        ],
        out_specs=[
            pl.BlockSpec((N_SAMPLE, TN_IN), lambda j: (0, j)),
            pl.BlockSpec((D_MODEL, TN_IN), lambda j: (0, j)),
        ],
        out_shape=[
            jax.ShapeDtypeStruct((N_SAMPLE, n_out), F32),
            jax.ShapeDtypeStruct((D_MODEL, n_out), BF16),
        ],
        compiler_params=_params("parallel"),
        name="in_proj_sample",
    )(a, w)


def _out_proj_kernel(yp_ref, ys_ref, w_ref, g_ref, gf_ref, *refs):
    *h_refs, o_ref, f_ref = refs

    def finish(y_ref, h_ref):
        for r in range(0, TM_OUT, SLAB):
            rows = slice(r, r + SLAB)
            y = y_ref[rows, :].astype(BF16)
            hn = h_ref[rows, :] + _rms_scale(_dot(y, w_ref[...]), g_ref[...])
            o_ref[rows, :] = hn
            f_ref[rows, :] = _rms_scale(hn, gf_ref[...]).astype(BF16)

    pl.when(_is_prompt_tile())(lambda: finish(yp_ref, h_refs[0]))
    pl.when(jnp.logical_not(_is_prompt_tile()))(lambda: finish(ys_ref, h_refs[-1]))


def _out_proj(y_prompt, y_sample, w, layer, g, g_ffn, h):
    prompt_tile = lambda: pl.BlockSpec((TM_OUT, D_MODEL), lambda i: (jnp.minimum(i, N_PROMPT_TILES - 1), 0))
    sample_tile = lambda **kw: pl.BlockSpec((TM_OUT, D_MODEL), lambda i: (0, 0), **kw)
    if isinstance(h, tuple):
        y_sample_spec = sample_tile(pipeline_mode=pl.Buffered(1))
        h_specs = [prompt_tile(), sample_tile(pipeline_mode=pl.Buffered(1))]
    else:
        y_sample_spec = sample_tile()
        h_specs = [pl.BlockSpec((TM_OUT, D_MODEL), lambda i: (i, 0))]
        h = (h,)
    return pl.pallas_call(
        _out_proj_kernel,
        grid=(N_TOK // TM_OUT,),
        in_specs=[
            prompt_tile(),
            y_sample_spec,
            pl.BlockSpec((None, D_MODEL, D_MODEL), lambda i: (layer, 0, 0)),
            pl.BlockSpec((1, D_MODEL), lambda i: (0, 0)),
            pl.BlockSpec((1, D_MODEL), lambda i: (0, 0)),
        ] + h_specs,
        out_specs=[
            pl.BlockSpec((TM_OUT, D_MODEL), lambda i: (i, 0)),
            pl.BlockSpec((TM_OUT, D_MODEL), lambda i: (i, 0)),
        ],
        out_shape=[
            jax.ShapeDtypeStruct((N_TOK, D_MODEL), F32),
            jax.ShapeDtypeStruct((N_TOK, D_MODEL), BF16),
        ],
        compiler_params=_params("parallel"),
        name="out_proj_norm_residual",
    )(y_prompt, y_sample, w, g.reshape(1, D_MODEL), g_ffn.reshape(1, D_MODEL), *h)


N_FF_CHUNKS = D_FF // TF
HEAD_SPLIT = 2
TF_HEAD = TF // HEAD_SPLIT


def _ffn_accumulate(k, f_ref, wu_ref, wd_ref, acc_ref, wub_ref=None, wdb_ref=None):
    @pl.when(k == 0)
    def _():
        acc_ref[...] = jnp.zeros_like(acc_ref)

    wu = wu_ref[...].astype(BF16)
    wd = wd_ref[...].astype(BF16)
    if wub_ref is not None:
        wub_ref[...] = wu
        wdb_ref[...] = wd
    u = jnp.square(jnp.maximum(_dot(f_ref[...], wu), 0.0)).astype(BF16)
    acc_ref[...] += _dot(u, wd)


def _ffn_epilogue(h_ref, acc_ref, g2_ref, o_ref, gn_ref=None, a_ref=None):
    for r in range(0, TM_OUT, SLAB):
        rows = slice(r, r + SLAB)
        hn = h_ref[rows, :] + _rms_scale(acc_ref[rows, :], g2_ref[...])
        o_ref[rows, :] = hn
        if a_ref is not None:
            a_ref[rows, :] = _rms_scale(hn, gn_ref[...]).astype(BF16)


def _ffn_head_kernel(f_ref, h_ref, wu_ref, wd_ref, g2_ref, *refs, last_layer):
    if last_layer:
        o_ref, wub_ref, wdb_ref, acc_ref = refs
        gn_ref = a_ref = None
    else:
        gn_ref, o_ref, a_ref, wub_ref, wdb_ref, acc_ref = refs
    k = pl.program_id(0)
    _ffn_accumulate(k, f_ref, wu_ref, wd_ref, acc_ref, wub_ref, wdb_ref)
    pl.when(k == pl.num_programs(0) - 1)(
        functools.partial(_ffn_epilogue, h_ref, acc_ref, g2_ref, o_ref, gn_ref, a_ref))


def _ffn_body_kernel(f_ref, h_ref, wu_ref, wd_ref, g2_ref, *refs, last_layer):
    k = pl.program_id(1)
    is_last = k == pl.num_programs(1) - 1
    if last_layer:
        yp_ref, ys_ref, acc_ref = refs
        _ffn_accumulate(k, f_ref, wu_ref, wd_ref, acc_ref)
        is_prompt = pl.program_id(0) + 1 < N_PROMPT_TILES
        pl.when(jnp.logical_and(is_last, is_prompt))(
            functools.partial(_ffn_epilogue, h_ref, acc_ref, g2_ref, yp_ref))
        pl.when(jnp.logical_and(is_last, jnp.logical_not(is_prompt)))(
            functools.partial(_ffn_epilogue, h_ref, acc_ref, g2_ref, ys_ref))
    else:
        gn_ref, o_ref, a_ref, acc_ref = refs
        _ffn_accumulate(k, f_ref, wu_ref, wd_ref, acc_ref)
        pl.when(is_last)(functools.partial(_ffn_epilogue, h_ref, acc_ref, g2_ref, o_ref, gn_ref, a_ref))


def _ffn(f, h, w_up, w_down, layer, g2, g_next=None):
    last_layer = g_next is None
    gains = [g2.reshape(1, D_MODEL)] + ([] if last_layer else [g_next.reshape(1, D_MODEL)])
    acc = pltpu.VMEM((TM_OUT, D_MODEL), F32)
    rounded_shapes = [
        jax.ShapeDtypeStruct((N_FF_CHUNKS, D_MODEL, TF), BF16),
        jax.ShapeDtypeStruct((D_FF, D_MODEL), BF16),
    ]
    if last_layer:
        act_shapes = [jax.ShapeDtypeStruct((N_PROMPT, D_MODEL), F32)]
    else:
        act_shapes = [jax.ShapeDtypeStruct((N_TOK, D_MODEL), F32), jax.ShapeDtypeStruct((N_TOK, D_MODEL), BF16)]

    tile0 = lambda **kw: pl.BlockSpec((TM_OUT, D_MODEL), lambda k: (0, 0), **kw)
    head = pl.pallas_call(
        functools.partial(_ffn_head_kernel, last_layer=last_layer),
        grid=(N_FF_CHUNKS * HEAD_SPLIT,),
        in_specs=[
            tile0(pipeline_mode=pl.Buffered(1)), tile0(pipeline_mode=pl.Buffered(1)),
            pl.BlockSpec((None, D_MODEL, TF_HEAD), lambda k: (layer, 0, k)),
            pl.BlockSpec((None, TF_HEAD, D_MODEL), lambda k: (layer, k, 0)),
        ] + [pl.BlockSpec((1, D_MODEL), lambda k: (0, 0))] * len(gains),
        out_specs=[tile0() for _ in act_shapes] + [
            pl.BlockSpec((None, D_MODEL, TF_HEAD), lambda k: (k // HEAD_SPLIT, 0, k % HEAD_SPLIT)),
            pl.BlockSpec((TF_HEAD, D_MODEL), lambda k: (k, 0)),
        ],
        out_shape=act_shapes + rounded_shapes,
        scratch_shapes=[acc],
        compiler_params=_params("arbitrary"),
        name="relu2_mlp_head",
    )(f, h, w_up, w_down, *gains)
    *acts, wu_b, wd_b = head

    tile = lambda: pl.BlockSpec((TM_OUT, D_MODEL), lambda i, k: (i + 1, 0))
    in_specs = [
        tile(), tile(),
        pl.BlockSpec((None, D_MODEL, TF), lambda i, k: (k, 0, 0)),
        pl.BlockSpec((TF, D_MODEL), lambda i, k: (k, 0)),
    ] + [pl.BlockSpec((1, D_MODEL), lambda i, k: (0, 0))] * len(gains)
    if last_layer:
        out_specs = [
            pl.BlockSpec((TM_OUT, D_MODEL), lambda i, k: (jnp.minimum(i + 1, N_PROMPT_TILES - 1), 0)),
            pl.BlockSpec((TM_OUT, D_MODEL), lambda i, k: (0, 0)),
        ]
        out_shape = act_shapes + [jax.ShapeDtypeStruct((N_SAMPLE, D_MODEL), F32)]
    else:
        out_specs = [tile(), tile()]
        out_shape = act_shapes
    n_in = len(in_specs)
    return pl.pallas_call(
        _skip_aliased(functools.partial(_ffn_body_kernel, last_layer=last_layer), n_in, len(acts)),
        grid=(N_TOK // TM_OUT - 1, N_FF_CHUNKS),
        in_specs=in_specs + [pl.BlockSpec(memory_space=pl.ANY)] * len(acts),
        out_specs=out_specs,
        out_shape=out_shape,
        input_output_aliases={n_in + j: j for j in range(len(acts))},
        scratch_shapes=[acc],
        compiler_params=_params("arbitrary", "arbitrary"),
        name="relu2_mlp_body",
    )(f, h, wu_b, wd_b, *gains, *acts)


def _rotary_tables(pos):
    half = RET_DK // 2
    inv = ROPE_BASE ** (-np.arange(half, dtype=np.float64) / half)
    ang = np.asarray(pos, np.float64)[:, None] * inv[None, :]
    cos = np.concatenate([np.cos(ang), np.cos(ang)], axis=-1)
    sin = np.concatenate([-np.sin(ang), np.sin(ang)], axis=-1)
    return jnp.asarray(cos, F32), jnp.asarray(sin, F32)


def _log_gamma():
    return np.log1p(-np.exp2(-5.0 - np.arange(RET_HEADS, dtype=np.float64)))


def _retention_tables(length):
    log_g = _log_gamma()
    idx = np.arange(length, dtype=np.float64)
    diff = idx[:, None] - idx[None, :]
    mask = np.where(diff[None] >= 0, np.exp(log_g[:, None, None] * np.maximum(diff, 0.0)[None]), 0.0)
    qd = np.exp(log_g[:, None] * (idx + 1.0))
    kd = np.exp(log_g[:, None] * (length - 1.0 - idx))
    cd = np.exp(log_g * length)
    return mask, qd, kd, cd


def _lane_bcast(a, width):
    return np.repeat(a[..., None], width, axis=-1)


N_PROMPT_CHUNKS = N_PROMPT // CHUNK
PROJ_COLS = 512


def _projected_chunk(step):
    return jnp.minimum(step, N_PROMPT_CHUNKS - 1)


def _mixed_chunk(step):
    return jnp.maximum(step - 1, 0)


def _interleave(first, second):
    i = j = 0
    while i < len(first) or j < len(second):
        if j >= len(second) or (i < len(first) and i * len(second) <= j * len(first)):
            first[i]()
            i += 1
        else:
            second[j]()
            j += 1


def _skewed(a_ref, w_ref, z_refs, n_cols, mix):
    step = pl.program_id(0)
    n = (step + N_CHUNKS - 1) % N_CHUNKS

    @pl.when(step == 0)
    def _():
        z_refs[1][...] = jnp.zeros_like(z_refs[1])

    for parity in range(2):
        z_next, z_mixed = z_refs[parity], z_refs[1 - parity]

        def body(z_next=z_next, z_mixed=z_mixed):
            a = a_ref[...]

            def project(c):
                z_next[:, c:c + PROJ_COLS] = _dot(a, w_ref[:, c:c + PROJ_COLS])

            mix(step, n, z_mixed, [functools.partial(project, c) for c in range(0, n_cols, PROJ_COLS)])

        pl.when(step % 2 == parity)(body)


def _retention_head_out(o, gate):
    o = o * lax.rsqrt(jnp.mean(o * o, axis=-1, keepdims=True) + EPS)
    return gate * jax.nn.sigmoid(gate) * o


def _rotate(x, cos, sin):
    return x * cos + pltpu.roll(x, RET_DK // 2, 1) * sin


def _even_prompt_kernel(a_ref, w_ref, cos_ref, sin_ref, mask_ref, qd_ref, kd_ref, wp_ref, sp_ref,
                        y_ref, hist_ref, st_ref, z0_ref, z1_ref, pext_ref, s_ref, *, chunk_decay):
    step = pl.program_id(0)
    n = (step + N_CHUNKS - 1) % N_CHUNKS

    @pl.when(jnp.logical_or(n == 0, step == 0))
    def _():
        pext_ref[0:POOL_PAD, :] = jnp.zeros((POOL_PAD, POOL_WIDTH), F32)
        s_ref[...] = jnp.zeros_like(s_ref)

    _skewed(a_ref, w_ref, (z0_ref, z1_ref), EVEN_IN,
            functools.partial(_even_prompt_mix, cos_ref, sin_ref, mask_ref, qd_ref, kd_ref, wp_ref, sp_ref, y_ref,
                              pext_ref, s_ref, chunk_decay))

    @pl.when(jnp.logical_and(n == N_CHUNKS - 1, step > 0))
    def _():
        hist_ref[0] = pext_ref[POOL_PAD + CHUNK - POOL_HIST:POOL_PAD + CHUNK, :]
        st_ref[0] = s_ref[...]


def _even_prompt_mix(cos_ref, sin_ref, mask_ref, qd_ref, kd_ref, wp_ref, sp_ref, y_ref, pext_ref, s_ref,
                     chunk_decay, step, n, zc, project):
    def pool():
        pext_ref[POOL_PAD:POOL_PAD + CHUNK, :] = zc[:, 0:POOL_WIDTH]
        pos = n * CHUNK + lax.broadcasted_iota(jnp.int32, (CHUNK, 1), 0)
        for g, w in enumerate(POOL_WINDOWS):
            lanes = slice(g * POOL_GROUP_DIM, (g + 1) * POOL_GROUP_DIM)
            p = pext_ref[POOL_PAD:POOL_PAD + CHUNK, lanes]
            acc = p
            for i in range(1, w):
                acc = acc + pext_ref[POOL_PAD - i:POOL_PAD - i + CHUNK, lanes]
            cnt = jnp.minimum(w, pos + 1).astype(F32)
            d = acc / cnt - p
            yg = _dot(d.astype(BF16), wp_ref[g].astype(BF16)) * sp_ref[:, lanes]
            y_ref[:, lanes] = yg.astype(BF16)
        pext_ref[0:POOL_PAD, :] = pext_ref[CHUNK:CHUNK + POOL_PAD, :]

    live = {}

    def scores_stage(h):
        cos = cos_ref[...]
        sin = sin_ref[...]
        q = _rotate(zc[:, Q_OFF + h * RET_DK:Q_OFF + (h + 1) * RET_DK], cos, sin)
        k = _rotate(zc[:, K_OFF + h * RET_DK:K_OFF + (h + 1) * RET_DK], cos, sin) * (RET_DK ** -0.5)
        v = zc[:, V_OFF + h * RET_DV:V_OFF + (h + 1) * RET_DV].astype(BF16)
        live[h] = (_dot_nt(q.astype(BF16), k.astype(BF16)), (q * qd_ref[h]).astype(BF16),
                   (k * kd_ref[h]).astype(BF16), v)

    def output_stage(h):
        scores, q_dec, k_dec, v = live[h]
        s = s_ref[h]
        o = _dot((scores * mask_ref[h]).astype(BF16), v) + _dot(q_dec, s.astype(BF16))
        s_ref[h] = s * chunk_decay[h] + _dot_tn(k_dec, v)
        live[h] = o

    def norm_stage(h):
        gate = zc[:, G_OFF + h * RET_DV:G_OFF + (h + 1) * RET_DV]
        y_ref[:, POOL_WIDTH + h * RET_DV:POOL_WIDTH + (h + 1) * RET_DV] = (
            _retention_head_out(live.pop(h), gate).astype(BF16))

    def heads_step(i):
        for stage, h in ((norm_stage, i - 2), (output_stage, i - 1), (scores_stage, i)):
            if 0 <= h < RET_HEADS:
                stage(h)

    _interleave(project, [pool] + [functools.partial(heads_step, i) for i in range(RET_HEADS + 2)])


def _even_prompt(a, w_in, w_pool, s_pool, layer, prev_hist, prev_state):
    cos, sin = _rotary_tables(np.arange(SEQ))
    mask, qd, kd, cd = _retention_tables(CHUNK)
    const = lambda *shape: pl.BlockSpec(shape, lambda t: (0,) * len(shape))
    return _call_stacked(
        functools.partial(_even_prompt_kernel, chunk_decay=tuple(float(c) for c in cd)),
        name="even_mixer_prompt",
        grid=(N_PROMPT_CHUNKS + 1,),
        in_specs=[
            pl.BlockSpec((CHUNK, D_MODEL), lambda t: (_projected_chunk(t), 0)),
            pl.BlockSpec((D_MODEL, EVEN_IN), lambda t: (0, 0), pipeline_mode=pl.Buffered(1)),
            pl.BlockSpec((CHUNK, RET_DK), lambda t: (_mixed_chunk(t) % N_CHUNKS, 0)),
            pl.BlockSpec((CHUNK, RET_DK), lambda t: (_mixed_chunk(t) % N_CHUNKS, 0)),
            const(RET_HEADS, CHUNK, CHUNK),
            const(RET_HEADS, CHUNK, RET_DK),
            const(RET_HEADS, CHUNK, RET_DK),
            pl.BlockSpec((None, POOL_GROUPS, POOL_GROUP_DIM, POOL_GROUP_DIM), lambda t: (layer, 0, 0, 0)),
            pl.BlockSpec((None, 1, POOL_WIDTH), lambda t: (layer, 0, 0)),
        ],
        args=(a, w_in, cos, sin, jnp.asarray(mask, F32), jnp.asarray(_lane_bcast(qd, RET_DK), F32),
              jnp.asarray(_lane_bcast(kd, RET_DK), F32), w_pool, s_pool.reshape(N_EVEN, 1, POOL_WIDTH)),
        out_specs=[
            pl.BlockSpec((CHUNK, D_MODEL), lambda t: (_mixed_chunk(t), 0)),
            pl.BlockSpec((None, 1, POOL_HIST, POOL_WIDTH), lambda t: (layer, _mixed_chunk(t) // N_CHUNKS, 0, 0)),
            pl.BlockSpec((None, 1, RET_HEADS, RET_DK, RET_DV),
                         lambda t: (layer, _mixed_chunk(t) // N_CHUNKS, 0, 0, 0)),
        ],
        out_shape=[
            jax.ShapeDtypeStruct((N_PROMPT, D_MODEL), BF16),
            jax.ShapeDtypeStruct((N_EVEN, BATCH, POOL_HIST, POOL_WIDTH), F32),
            jax.ShapeDtypeStruct((N_EVEN, BATCH, RET_HEADS, RET_DK, RET_DV), F32),
        ],
        stacked={1: prev_hist, 2: prev_state},
        sem=("arbitrary",),
        scratch_shapes=[
            pltpu.VMEM((CHUNK, EVEN_IN), F32),
            pltpu.VMEM((CHUNK, EVEN_IN), F32),
            pltpu.VMEM((POOL_PAD + CHUNK, POOL_WIDTH), F32),
            pltpu.VMEM((RET_HEADS, RET_DK, RET_DV), F32),
        ],
    )


def _position_slabs(x):
    return x.reshape(SAMPLE_ROWS, x.shape[-1])


def _even_sample_kernel(z_ref, cos_ref, sin_ref, mask_ref, qd_ref, kd_ref, wp_ref, sp_ref, hist_ref, st_ref,
                        y_ref, nhist_ref, nst_ref, *, chunk_decay):
    def ext(r, lanes):
        return hist_ref[r, :, lanes] if r < POOL_HIST else z_ref[r - POOL_HIST, :, lanes]

    for g, w in enumerate(POOL_WINDOWS):
        lanes = slice(g * POOL_GROUP_DIM, (g + 1) * POOL_GROUP_DIM)
        d = []
        for t in range(DEC_SEQ):
            p = ext(POOL_HIST + t, lanes)
            acc = p
            for i in range(1, w):
                acc = acc + ext(POOL_HIST + t - i, lanes)
            d.append(acc / float(min(w, PAST_LEN + t + 1)) - p)
        d = jnp.concatenate(d, axis=0)
        yg = _dot(d.astype(BF16), wp_ref[g].astype(BF16)) * sp_ref[:, lanes]
        y_ref[:, :, lanes] = yg.reshape(DEC_SEQ, SAMPLE_BB, POOL_GROUP_DIM)
    for r in range(POOL_HIST):
        nhist_ref[r] = hist_ref[r + DEC_SEQ] if r + DEC_SEQ < POOL_HIST else z_ref[r + DEC_SEQ - POOL_HIST, :,
                                                                                 0:POOL_WIDTH]

    cos = cos_ref[...]
    sin = sin_ref[...]
    seq_of_row = lax.broadcasted_iota(jnp.int32, (SAMPLE_ROWS, 1), 0) % SAMPLE_BB
    for h in range(RET_HEADS):
        q = _rotate(_position_slabs(z_ref[:, :, Q_OFF + h * RET_DK:Q_OFF + (h + 1) * RET_DK]), cos, sin)
        k = _rotate(_position_slabs(z_ref[:, :, K_OFF + h * RET_DK:K_OFF + (h + 1) * RET_DK]), cos, sin)
        k = k * (RET_DK ** -0.5)
        v = _position_slabs(z_ref[:, :, V_OFF + h * RET_DV:V_OFF + (h + 1) * RET_DV]).astype(BF16)
        gate = _position_slabs(z_ref[:, :, G_OFF + h * RET_DV:G_OFF + (h + 1) * RET_DV])
        scores = _dot_nt(q.astype(BF16), k.astype(BF16)) * mask_ref[h]
        o = _dot(scores.astype(BF16), v)
        q_dec = (q * qd_ref[h]).astype(BF16)
        k_dec = k * kd_ref[h]
        for b in range(SAMPLE_BB):
            own = seq_of_row == b
            s = st_ref[b, h]
            o = o + jnp.where(own, _dot(q_dec, s.astype(BF16)), 0.0)
            nst_ref[b, h] = s * chunk_decay[h] + _dot_tn(jnp.where(own, k_dec, 0.0).astype(BF16), v)
        y_ref[:, :, POOL_WIDTH + h * RET_DV:POOL_WIDTH + (h + 1) * RET_DV] = (
            _retention_head_out(o, gate).reshape(DEC_SEQ, SAMPLE_BB, RET_DV))


def _even_sample(z, w_pool, s_pool, state_pool, state_ret, layer, prev_hist, prev_state):
    pos = PAST_LEN + np.repeat(np.arange(DEC_SEQ), SAMPLE_BB)
    cos, sin = _rotary_tables(pos)
    mask, qd, kd, cd = _retention_tables(DEC_SEQ)
    mask = np.stack([np.kron(m, np.eye(SAMPLE_BB)) for m in mask])
    qd = _lane_bcast(np.repeat(qd, SAMPLE_BB, axis=1), RET_DK)
    kd = _lane_bcast(np.repeat(kd, SAMPLE_BB, axis=1), RET_DK)
    const = lambda *shape: pl.BlockSpec(shape, lambda i: (0,) * len(shape))
    return _call_stacked(
        functools.partial(_even_sample_kernel, chunk_decay=tuple(float(c) for c in cd)),
        name="even_mixer_sample",
        grid=(DEC_BATCH // SAMPLE_BB,),
        in_specs=[
            pl.BlockSpec((DEC_SEQ, SAMPLE_BB, EVEN_IN), lambda i: (0, i, 0)),
            const(SAMPLE_ROWS, RET_DK),
            const(SAMPLE_ROWS, RET_DK),
            const(RET_HEADS, SAMPLE_ROWS, SAMPLE_ROWS),
            const(RET_HEADS, SAMPLE_ROWS, RET_DK),
            const(RET_HEADS, SAMPLE_ROWS, RET_DK),
            pl.BlockSpec((None, POOL_GROUPS, POOL_GROUP_DIM, POOL_GROUP_DIM), lambda i: (layer, 0, 0, 0)),
            pl.BlockSpec((None, 1, POOL_WIDTH), lambda i: (layer, 0, 0)),
            pl.BlockSpec((None, POOL_HIST, SAMPLE_BB, POOL_WIDTH), lambda i: (layer, 0, i, 0)),
            pl.BlockSpec((None, SAMPLE_BB, RET_HEADS, RET_DK, RET_DV), lambda i: (layer, i, 0, 0, 0)),
        ],
        args=(z, cos, sin, jnp.asarray(mask, F32), jnp.asarray(qd, F32), jnp.asarray(kd, F32),
              w_pool, s_pool.reshape(N_EVEN, 1, POOL_WIDTH), state_pool, state_ret),
        out_specs=[
            pl.BlockSpec((DEC_SEQ, SAMPLE_BB, D_MODEL), lambda i: (0, i, 0)),
            pl.BlockSpec((None, POOL_HIST, SAMPLE_BB, POOL_WIDTH), lambda i: (layer, 0, i, 0)),
            pl.BlockSpec((None, SAMPLE_BB, RET_HEADS, RET_DK, RET_DV), lambda i: (layer, i, 0, 0, 0)),
        ],
        out_shape=[
            jax.ShapeDtypeStruct((DEC_SEQ, DEC_BATCH, D_MODEL), F32),
            jax.ShapeDtypeStruct((N_EVEN, POOL_HIST, DEC_BATCH, POOL_WIDTH), F32),
            jax.ShapeDtypeStruct((N_EVEN, DEC_BATCH, RET_HEADS, RET_DK, RET_DV), F32),
        ],
        stacked={1: prev_hist, 2: prev_state},
        sem=("parallel",),
        scratch_shapes=[],
    )


CONV_LANES = 128
SHIFTED_ROWS = CONV_PAD + CHUNK - SUBLANES


def _odd_prompt_kernel(a_ref, w_ref, lng_ref, lnb_ref, sgw_ref, sgb_ref, dw_ref, dwb_ref, cvg_ref, cvb_ref,
                       y_ref, cst_ref, z0_ref, z1_ref, vb_ref, ext_ref, cv_ref, xs_ref):
    step = pl.program_id(0)
    n = (step + N_CHUNKS - 1) % N_CHUNKS

    @pl.when(jnp.logical_or(n == 0, step == 0))
    def _():
        ext_ref[0:CONV_PAD, :] = jnp.zeros((CONV_PAD, CONV_CH), F32)

    _skewed(a_ref, w_ref, (z0_ref, z1_ref), ODD_IN,
            functools.partial(_odd_prompt_mix, lng_ref, lnb_ref, sgw_ref, sgb_ref, dw_ref, dwb_ref, cvg_ref, cvb_ref,
                              y_ref, vb_ref, ext_ref, cv_ref, xs_ref))

    @pl.when(jnp.logical_and(n == N_CHUNKS - 1, step > 0))
    def _():
        cst_ref[0] = ext_ref[CONV_PAD + CHUNK - CONV_HIST:CONV_PAD + CHUNK, :]

    ext_ref[0:CONV_PAD, :] = ext_ref[CHUNK:CHUNK + CONV_PAD, :]


def _odd_prompt_mix(lng_ref, lnb_ref, sgw_ref, sgb_ref, dw_ref, dwb_ref, cvg_ref, cvb_ref,
                    y_ref, vb_ref, ext_ref, cv_ref, xs_ref, step, n, zc, project):
    def gate_values():
        v = _layer_norm(jax.nn.gelu(zc[:, SG_WIDTH:2 * SG_WIDTH]), lng_ref[...], lnb_ref[...])
        vb_ref[...] = v.astype(BF16)

    def gating(g):
        lanes = slice(g * SG_GROUP_DIM, (g + 1) * SG_GROUP_DIM)
        row = lax.broadcasted_iota(jnp.int32, (CHUNK, CHUNK), 0)
        col = lax.broadcasted_iota(jnp.int32, (CHUNK, CHUNK), 1)
        ws = jnp.where(col <= row, sgw_ref[g], 0.0).astype(BF16)
        mixed = _dot(ws, vb_ref[:, lanes]) + sgb_ref[:, g:g + 1]
        y_ref[:, lanes] = (jax.nn.gelu(zc[:, lanes]) * mixed).astype(BF16)

    def glu():
        a = zc[:, 2 * SG_WIDTH:2 * SG_WIDTH + CONV_CH]
        gate = zc[:, 2 * SG_WIDTH + CONV_CH:2 * SG_WIDTH + 2 * CONV_CH]
        ext_ref[CONV_PAD:CONV_PAD + CHUNK, :] = a * jax.nn.sigmoid(gate)

    def shifted_copy(s):
        xs_ref[s - 1] = ext_ref[s:s + SHIFTED_ROWS, :]

    def conv(c):
        lanes = slice(c, c + CONV_LANES)
        acc = jnp.broadcast_to(dwb_ref[:, lanes], (CHUNK, CONV_LANES))
        for j in range(CONV_K):
            tile, s = divmod(CONV_PAD - CONV_HIST + j, SUBLANES)
            rows = slice(tile * SUBLANES, tile * SUBLANES + CHUNK)
            window = ext_ref[rows, lanes] if s == 0 else xs_ref[s - 1, rows, lanes]
            acc = acc + window * dw_ref[j:j + 1, lanes]
        cv_ref[:, lanes] = acc

    def conv_out():
        yd = _layer_norm(cv_ref[...], cvg_ref[...], cvb_ref[...])
        y_ref[:, SG_WIDTH:SG_WIDTH + CONV_CH] = (yd * jax.nn.sigmoid(yd)).astype(BF16)

    _interleave(project, [gate_values, glu] + [functools.partial(shifted_copy, s) for s in range(1, SUBLANES)]
                + [functools.partial(conv, c) for c in range(0, CONV_CH, CONV_LANES)] + [conv_out])
    for g in range(SG_GROUPS):
        gating(g)


def _odd_weight_specs(layer):
    per_layer = lambda *shape: pl.BlockSpec((None,) + shape, lambda *_: (layer,) + (0,) * len(shape))
    return dict(
        ln=per_layer(1, SG_WIDTH),
        sgw=per_layer(SG_GROUPS, SG_CHUNK, SG_CHUNK),
        sgb=per_layer(SG_CHUNK, SG_GROUPS),
        dw=per_layer(CONV_K, CONV_CH),
        ch=per_layer(1, CONV_CH),
    )


def _odd_prompt(a, w_in, sg_ln_g, sg_ln_b, sg_w, sg_b, dw_w, dw_b, cv_ln_g, cv_ln_b, layer, prev_cst):
    spec = _odd_weight_specs(layer)
    row = lambda x: x.reshape(N_ODD, 1, -1)
    return _call_stacked(
        _odd_prompt_kernel,
        name="odd_mixer_prompt",
        grid=(N_PROMPT_CHUNKS + 1,),
        in_specs=[
            pl.BlockSpec((CHUNK, D_MODEL), lambda t: (_projected_chunk(t), 0)),
            pl.BlockSpec((D_MODEL, ODD_IN), lambda t: (0, 0), pipeline_mode=pl.Buffered(1)),
            spec["ln"], spec["ln"], spec["sgw"], spec["sgb"], spec["dw"], spec["ch"], spec["ch"], spec["ch"],
        ],
        args=(a, w_in, row(sg_ln_g), row(sg_ln_b), sg_w, jnp.swapaxes(sg_b, 1, 2), dw_w, row(dw_b),
              row(cv_ln_g), row(cv_ln_b)),
        out_specs=[
            pl.BlockSpec((CHUNK, D_MODEL), lambda t: (_mixed_chunk(t), 0)),
            pl.BlockSpec((None, 1, CONV_HIST, CONV_CH), lambda t: (layer, _mixed_chunk(t) // N_CHUNKS, 0, 0)),
        ],
        out_shape=[
            jax.ShapeDtypeStruct((N_PROMPT, D_MODEL), BF16),
            jax.ShapeDtypeStruct((N_ODD, BATCH, CONV_HIST, CONV_CH), F32),
        ],
        stacked={1: prev_cst},
        sem=("arbitrary",),
        scratch_shapes=[
            pltpu.VMEM((CHUNK, ODD_IN), F32),
            pltpu.VMEM((CHUNK, ODD_IN), F32),
            pltpu.VMEM((CHUNK, SG_WIDTH), BF16),
            pltpu.VMEM((CONV_PAD + CHUNK, CONV_CH), F32),
            pltpu.VMEM((CHUNK, CONV_CH), F32),
            pltpu.VMEM((SUBLANES - 1, SHIFTED_ROWS, CONV_CH), F32),
        ],
    )


def _odd_sample_kernel(sgw_ref, sgb_ref, z_ref, lng_ref, lnb_ref, dw_ref, dwb_ref, cvg_ref, cvb_ref, cst_ref,
                       y_ref, sgv_ref, ncst_ref, cv_ref):
    v = _layer_norm(jax.nn.gelu(_position_slabs(z_ref[:, :, SG_WIDTH:2 * SG_WIDTH])), lng_ref[...], lnb_ref[...])
    sgv_ref[...] = v.reshape(DEC_SEQ, SAMPLE_BB, SG_WIDTH)
    for g in range(SG_GROUPS):
        lanes = slice(g * SG_GROUP_DIM, (g + 1) * SG_GROUP_DIM)
        for t in range(DEC_SEQ):
            mixed = jnp.full((SAMPLE_BB, SG_GROUP_DIM), sgb_ref[g * DEC_SEQ + t], F32)
            for j in range(t + 1):
                mixed = mixed + sgw_ref[(g * DEC_SEQ + t) * DEC_SEQ + j] * sgv_ref[j, :, lanes]
            y_ref[t, :, lanes] = jax.nn.gelu(z_ref[t, :, lanes]) * mixed

    def ext(r, lanes):
        if r < CONV_HIST:
            return cst_ref[r, :, lanes]
        a = z_ref[r - CONV_HIST, :, pl.ds(2 * SG_WIDTH + lanes.start, CONV_LANES)]
        gate = z_ref[r - CONV_HIST, :, pl.ds(2 * SG_WIDTH + CONV_CH + lanes.start, CONV_LANES)]
        return a * jax.nn.sigmoid(gate)

    for c in range(0, CONV_CH, CONV_LANES):
        lanes = slice(c, c + CONV_LANES)
        rows = [ext(r, lanes) for r in range(CONV_HIST + DEC_SEQ)]
        for t in range(DEC_SEQ):
            acc = jnp.broadcast_to(dwb_ref[:, lanes], (SAMPLE_BB, CONV_LANES))
            for j in range(CONV_K):
                acc = acc + rows[t + j] * dw_ref[j:j + 1, lanes]
            cv_ref[t, :, lanes] = acc
        for r in range(CONV_HIST):
            ncst_ref[r, :, lanes] = rows[r + DEC_SEQ]
    yd = _layer_norm(_position_slabs(cv_ref[...]), cvg_ref[...], cvb_ref[...])
    y_ref[:, :, SG_WIDTH:SG_WIDTH + CONV_CH] = (yd * jax.nn.sigmoid(yd)).reshape(DEC_SEQ, SAMPLE_BB, CONV_CH)


def _odd_sample(z, sg_ln_g, sg_ln_b, sg_w, sg_b, dw_w, dw_b, cv_ln_g, cv_ln_b, state_conv, layer,
                prev_sgv, prev_cst):
    spec = _odd_weight_specs(layer)
    row = lambda x: x.reshape(N_ODD, 1, -1)
    smem = pl.BlockSpec(memory_space=pltpu.SMEM)
    return _call_stacked(
        _odd_sample_kernel,
        name="odd_mixer_sample",
        grid=(DEC_BATCH // SAMPLE_BB,),
        in_specs=[
            smem,
            smem,
            pl.BlockSpec((DEC_SEQ, SAMPLE_BB, ODD_IN), lambda i: (0, i, 0)),
            spec["ln"], spec["ln"], spec["dw"], spec["ch"], spec["ch"], spec["ch"],
            pl.BlockSpec((None, CONV_HIST, SAMPLE_BB, CONV_CH), lambda i: (layer, 0, i, 0)),
        ],
        args=(sg_w[layer, :, :DEC_SEQ, :DEC_SEQ].reshape(-1), sg_b[layer, :, :DEC_SEQ].reshape(-1), z,
              row(sg_ln_g), row(sg_ln_b), dw_w, row(dw_b), row(cv_ln_g), row(cv_ln_b), state_conv),
        out_specs=[
            pl.BlockSpec((DEC_SEQ, SAMPLE_BB, D_MODEL), lambda i: (0, i, 0)),
            pl.BlockSpec((None, DEC_SEQ, SAMPLE_BB, SG_WIDTH), lambda i: (layer, 0, i, 0)),
            pl.BlockSpec((None, CONV_HIST, SAMPLE_BB, CONV_CH), lambda i: (layer, 0, i, 0)),
        ],
        out_shape=[
            jax.ShapeDtypeStruct((DEC_SEQ, DEC_BATCH, D_MODEL), F32),
            jax.ShapeDtypeStruct((N_ODD, DEC_SEQ, DEC_BATCH, SG_WIDTH), F32),
            jax.ShapeDtypeStruct((N_ODD, CONV_HIST, DEC_BATCH, CONV_CH), F32),
        ],
        stacked={1: prev_sgv, 2: prev_cst},
        sem=("parallel",),
        scratch_shapes=[pltpu.VMEM((DEC_SEQ, SAMPLE_BB, CONV_CH), F32)],
    )


def _position_major(x):
    return jnp.swapaxes(x, -3, -2)


def kernel(x_prompt, x_sample, state_pool, state_ret, state_conv, norm_mix_pre, norm_mix_post, norm_ffn_pre, norm_ffn_post, w_in_even, w_pool, s_pool, w_out_even, w_in_odd, sg_ln_g, sg_ln_b, sg_w, sg_b, dw_w, dw_b, cv_ln_g, cv_ln_b, w_out_odd, w_up, w_down):
    w_out_even, w_out_odd = w_out_even.astype(BF16), w_out_odd.astype(BF16)

    state_pool = _position_major(state_pool)
    state_conv = _position_major(state_conv)
    h = (x_prompt.reshape(N_PROMPT, D_MODEL), _position_major(x_sample).reshape(N_SAMPLE, D_MODEL))
    a = _prenorm(*h, norm_mix_pre[0])
    pool_p = pool_s = ret_p = ret_s = conv_p = conv_s = sgv_s = None
    for l in range(DEPTH):
        i = l // 2
        if l % 2 == 0:
            z_s, w_in = _in_proj_sample(a, w_in_even, i)
            y_p, pool_p, ret_p = _even_prompt(a, w_in, w_pool, s_pool, i, pool_p, ret_p)
            y_s, pool_s, ret_s = _even_sample(z_s.reshape(DEC_SEQ, DEC_BATCH, EVEN_IN), w_pool, s_pool, state_pool,
                                              state_ret, i, pool_s, ret_s)
            w_out = w_out_even
        else:
            z_s, w_in = _in_proj_sample(a, w_in_odd, i)
            y_p, conv_p = _odd_prompt(a, w_in, sg_ln_g, sg_ln_b, sg_w, sg_b, dw_w, dw_b, cv_ln_g, cv_ln_b,
                                      i, conv_p)
            y_s, sgv_s, conv_s = _odd_sample(z_s.reshape(DEC_SEQ, DEC_BATCH, ODD_IN), sg_ln_g, sg_ln_b, sg_w, sg_b,
                                             dw_w, dw_b, cv_ln_g, cv_ln_b, state_conv, i, sgv_s, conv_s)
            w_out = w_out_odd
        h, f = _out_proj(y_p, y_s.reshape(N_SAMPLE, D_MODEL), w_out, i, norm_mix_post[l], norm_ffn_pre[l], h)
        if l + 1 < DEPTH:
            h, a = _ffn(f, h, w_up, w_down, l, norm_ffn_post[l], norm_mix_pre[l + 1])
        else:
            y_prompt, y_sample = _ffn(f, h, w_up, w_down, l, norm_ffn_post[l])

    y_sample = _position_major(y_sample.reshape(DEC_SEQ, DEC_BATCH, D_MODEL))
    return (y_prompt.reshape(BATCH, SEQ, D_MODEL), y_sample, pool_p, _position_major(pool_s), ret_p, ret_s,
            conv_p, _position_major(conv_s), _position_major(sgv_s))
```
